```python
import math
import jax, jax.numpy as jnp
from jax import lax
import numpy as np

D_MODEL = 2048
BATCH = 2
SEQ = 4096
DEPTH = 1

CHUNK = 64
N_META = 16
QB = 128
N_BUCKETS = 32
MAX_DISTANCE = 128

A_HEADS = 8
A_HEAD_DIM = 128
KV_RANK = 256
IDX_HEADS = 16
IDX_DIM = 64
TOPK_MAX = 256

B_HEADS = 8
B_QK_DIM = 64
B_V_DIM = 128

A_WIDTH = A_HEADS * A_HEAD_DIM
B_WIDTH = B_HEADS * B_V_DIM
IN_SIZES = (A_WIDTH, KV_RANK, A_WIDTH, IDX_HEADS * IDX_DIM, IDX_DIM, IDX_HEADS,
            2 * B_HEADS * B_QK_DIM, 2 * B_HEADS * B_QK_DIM, B_WIDTH, B_WIDTH,
            D_MODEL, D_MODEL)
IN_WIDTH = (3 * A_WIDTH + KV_RANK + IDX_HEADS * IDX_DIM + IDX_DIM + IDX_HEADS
            - A_WIDTH + 4 * B_HEADS * B_QK_DIM + 2 * B_WIDTH + 2 * D_MODEL)
EPS = 1e-6

kernel_name = "chunk_causal_dsa_diffattn_gated_hybrid"


def rms_norm(x, g):
    xf = x.astype(jnp.float32)
    y = xf * lax.rsqrt(jnp.mean(xf * xf, axis=-1, keepdims=True) + EPS)
    return (y * g.astype(jnp.float32)).astype(x.dtype)


def layer_norm(x, g, b):
    xf = x.astype(jnp.float32)
    mu = jnp.mean(xf, axis=-1, keepdims=True)
    var = jnp.mean(jnp.square(xf - mu), axis=-1, keepdims=True)
    y = (xf - mu) * lax.rsqrt(var + EPS)
    return (y * g.astype(jnp.float32) + b.astype(jnp.float32)).astype(x.dtype)


def chunk_id(pos):
    return jnp.where(pos < N_META, 0, 1 + (pos - N_META) // CHUNK)


def t5_bucket(rel):
    nb = N_BUCKETS // 2
    max_exact = nb // 2
    ret = jnp.where(rel > 0, nb, 0)
    n = jnp.abs(rel)
    nf = jnp.maximum(n, 1).astype(jnp.float32)
    large = max_exact + (jnp.log(nf / max_exact) / math.log(MAX_DISTANCE / max_exact)
                         * (nb - max_exact)).astype(jnp.int32)
    large = jnp.minimum(large, nb - 1)
    return ret + jnp.where(n < max_exact, n, large)


def hybrid_layer(h, layer, rel_bias, pre_w, w_in, kv_norm_w, w_uk, w_uv, ikn_w, ikn_b,
                 lam_p, subln_w, w_o_a, w_o_b, w_out, post_w):
    Bsz, T, _ = h.shape
    u = rms_norm(h, pre_w)
    proj = u @ w_in
    split_at = np.cumsum(np.array(IN_SIZES))[:-1].tolist()
    q_a, ckv, z_a, iq, ik, iw, q_b, k_b, v_b, z_b, g_a, g_b = jnp.split(proj, split_at, axis=-1)

    q_a = q_a.reshape(Bsz, T, A_HEADS, A_HEAD_DIM)
    ckv = rms_norm(ckv, kv_norm_w)
    iq = iq.reshape(Bsz, T, IDX_HEADS, IDX_DIM)
    ik = layer_norm(ik, ikn_w, ikn_b)
    iw = iw * (IDX_HEADS ** -0.5 * IDX_DIM ** -0.5)

    q_b = q_b.reshape(Bsz, T, B_HEADS, 2, B_QK_DIM)
    k_b = k_b.reshape(Bsz, T, B_HEADS, 2, B_QK_DIM)
    v_b = v_b.reshape(Bsz, T, B_HEADS, B_V_DIM)
    k1, k2 = k_b[..., 0, :], k_b[..., 1, :]
    lam_init = 0.8 - 0.6 * math.exp(-0.3 * layer)
    lp = lam_p.astype(jnp.float32)
    lam = jnp.exp(jnp.sum(lp[0] * lp[1])) - jnp.exp(jnp.sum(lp[2] * lp[3])) + lam_init

    bias_a = rel_bias[:, :A_HEADS]
    bias_b = rel_bias[:, A_HEADS:]

    kpos = jnp.arange(T)
    k_cid = chunk_id(kpos)
    n_blk = -(-T // QB)
    Tp = n_blk * QB
    q_pos = jnp.arange(Tp).reshape(n_blk, QB)
    topk = min(TOPK_MAX, T // 4)

    def to_blocks(a):
        a = jnp.pad(a, [(0, 0), (0, Tp - T)] + [(0, 0)] * (a.ndim - 2))
        return a.reshape((Bsz, n_blk, QB) + a.shape[2:]).swapaxes(0, 1)

    def from_blocks(a):
        return a.swapaxes(0, 1).reshape((Bsz, Tp) + a.shape[3:])[:, :T]

    a_scale = A_HEAD_DIM ** -0.5

    def dsa_block(args):
        q, bq, bw, qp = args
        qc = chunk_id(qp)
        s = jnp.einsum('bqhd,bsd->bqhs', bq, ik)
        score = jnp.einsum('bqhs,bqh->bqs', jax.nn.relu(s), bw).astype(jnp.float32)
        allowed = k_cid[None, :] <= qc[:, None]
        score = jnp.where(allowed[None], score, -jnp.inf)
        _, idx = lax.top_k(score, topk)
        valid = k_cid[idx] <= qc[None, :, None]
        sel = jax.vmap(lambda c, i: c[i])(ckv, idx)
        q_lat = jnp.einsum('bqhd,rhd->bqhr', q, w_uk)
        logits = jnp.einsum('bqhr,bqkr->bqhk', q_lat, sel).astype(jnp.float32) * a_scale
        bias = bias_a[t5_bucket(idx - qp[None, :, None])]
        logits = logits + jnp.swapaxes(bias, 2, 3).astype(jnp.float32)
        logits = jnp.where(valid[:, :, None, :], logits, -jnp.inf)
        p = jax.nn.softmax(logits, axis=-1).astype(sel.dtype)
        o_lat = jnp.einsum('bqhk,bqkr->bqhr', p, sel)
        o = jnp.einsum('bqhr,rhd->bqhd', o_lat, w_uv)
        return o.reshape(Bsz, QB, A_WIDTH)

    b_scale = B_QK_DIM ** -0.5

    def diff_block(args):
        q1, q2, qp = args
        qc = chunk_id(qp)
        allowed = (k_cid[None, :] <= qc[:, None])[None, None]
        bias = bias_b[t5_bucket(kpos[None, :] - qp[:, None])]
        bias = jnp.transpose(bias, (2, 0, 1))[None].astype(jnp.float32)

        def probs(q, k):
            l = jnp.einsum('bqhd,bshd->bhqs', q, k).astype(jnp.float32) * b_scale + bias
            return jax.nn.softmax(jnp.where(allowed, l, -jnp.inf), axis=-1)

        attn = probs(q1, k1) - lam * probs(q2, k2)
        return jnp.einsum('bhqs,bshd->bqhd', attn.astype(v_b.dtype), v_b)

    o_a = from_blocks(lax.map(dsa_block, (to_blocks(q_a), to_blocks(iq), to_blocks(iw), q_pos)))
    o_b = from_blocks(lax.map(diff_block, (to_blocks(q_b[..., 0, :]), to_blocks(q_b[..., 1, :]), q_pos)))
    o_b = (rms_norm(o_b, subln_w) * (1.0 - lam_init)).reshape(Bsz, T, B_WIDTH)

    y_a = (o_a * jax.nn.silu(z_a)) @ w_o_a
    y_b = (o_b * jax.nn.silu(z_b)) @ w_o_b
    mix = jax.nn.sigmoid(g_a) * y_a + jax.nn.sigmoid(g_b) * y_b
    out = mix @ w_out
    return h + rms_norm(out, post_w)


def setup_inputs(seed: int = 0) -> dict:
    key = jax.random.key(seed)
    ks = jax.random.split(key, 18)
    f32 = jnp.float32
    nrm = lambda k, shape, s: jax.random.normal(k, shape, f32) * s
    L = DEPTH
    return {
        "x": nrm(ks[0], (BATCH, SEQ, D_MODEL), 1.0),
        "meta_tokens": nrm(ks[1], (N_META, D_MODEL), 1.0),
        "rel_bias": nrm(ks[2], (N_BUCKETS, A_HEADS + B_HEADS), 0.5),
        "pre_norm_w": 1.0 + nrm(ks[3], (L, D_MODEL), 0.02),
        "w_in": nrm(ks[4], (L, D_MODEL, IN_WIDTH), D_MODEL ** -0.5),
        "kv_norm_w": 1.0 + nrm(ks[5], (L, KV_RANK), 0.02),
        "w_uk": nrm(ks[6], (L, KV_RANK, A_HEADS, A_HEAD_DIM), KV_RANK ** -0.5),
        "w_uv": nrm(ks[7], (L, KV_RANK, A_HEADS, A_HEAD_DIM), KV_RANK ** -0.5),
        "idx_k_norm_w": 1.0 + nrm(ks[8], (L, IDX_DIM), 0.02),
        "idx_k_norm_b": nrm(ks[9], (L, IDX_DIM), 0.02),
        "diff_lambda": nrm(ks[10], (L, 4, B_QK_DIM), 0.1),
        "diff_subln_w": 1.0 + nrm(ks[11], (L, B_V_DIM), 0.02),
        "w_o_a": nrm(ks[12], (L, A_WIDTH, D_MODEL), A_WIDTH ** -0.5),
        "w_o_b": nrm(ks[13], (L, B_WIDTH, D_MODEL), B_WIDTH ** -0.5),
        "w_out": nrm(ks[14], (L, D_MODEL, D_MODEL), D_MODEL ** -0.5),
        "post_norm_w": 1.0 + nrm(ks[15], (L, D_MODEL), 0.02),
    }


def reference(x, meta_tokens, rel_bias, pre_norm_w, w_in, kv_norm_w, w_uk, w_uv,
              idx_k_norm_w, idx_k_norm_b, diff_lambda, diff_subln_w, w_o_a, w_o_b,
              w_out, post_norm_w):
    Bsz = x.shape[0]
    meta = jnp.broadcast_to(meta_tokens[None].astype(x.dtype), (Bsz, N_META, x.shape[-1]))
    h = jnp.concatenate([meta, x], axis=1)
    for l in range(DEPTH):
        h = hybrid_layer(h, l, rel_bias, pre_norm_w[l], w_in[l], kv_norm_w[l], w_uk[l], w_uv[l],
                         idx_k_norm_w[l], idx_k_norm_b[l], diff_lambda[l], diff_subln_w[l],
                         w_o_a[l], w_o_b[l], w_out[l], post_norm_w[l])
    return h[:, N_META:]
```

```python
import functools
import math

import numpy as np
import jax
import jax.numpy as jnp
from jax import lax
from jax.experimental import pallas as pl
from jax.experimental.pallas import tpu as pltpu

D_MODEL = 2048
BATCH = 2
SEQ = 4096
CHUNK = 64
N_META = 16
N_BUCKETS = 32
MAX_DISTANCE = 128
A_HEADS = 8
A_HEAD_DIM = 128
KV_RANK = 256
IDX_HEADS = 16
IDX_DIM = 64
TOPK = 256
B_HEADS = 8
B_QK_DIM = 64
B_V_DIM = 128
A_WIDTH = A_HEADS * A_HEAD_DIM
B_WIDTH = B_HEADS * B_V_DIM
IN_SIZES = (A_WIDTH, KV_RANK, A_WIDTH, IDX_HEADS * IDX_DIM, IDX_DIM, IDX_HEADS,
            2 * B_HEADS * B_QK_DIM, 2 * B_HEADS * B_QK_DIM, B_WIDTH, B_WIDTH,
            D_MODEL, D_MODEL)
EPS = 1e-6

BLK = 128
NQB = SEQ // BLK
NKB = NQB + 1
TP = NKB * BLK
ROWS = BATCH * TP
NEG = -1e30
INT_MIN = -2 ** 31
VMEM_LIMIT = 56 * 1024 * 1024

F32 = jnp.float32
BF16 = jnp.bfloat16
NT_DIMS = (((1,), (1,)), ((), ()))


def _t5_bucket_np(rel):
    nb = N_BUCKETS // 2
    max_exact = nb // 2
    ret = np.where(rel > 0, nb, 0)
    n = np.abs(rel)
    nf = np.maximum(n, 1).astype(np.float32)
    large = max_exact + (np.log(nf / np.float32(max_exact))
                         / np.float32(math.log(MAX_DISTANCE / max_exact))
                         * np.float32(nb - max_exact)).astype(np.int32)
    large = np.minimum(large, nb - 1)
    return ret + np.where(n < max_exact, n, large)


def _bias_tiles(rel_bias):
    a = np.arange(BLK)[:, None]
    b = np.arange(BLK)[None, :]
    far = np.full((BLK, BLK), -4 * BLK)
    rels = np.stack([a - b, a - b - BLK, a - N_META - b, far, far])
    dis = np.stack([(a >= CHUNK) & (b < CHUNK), np.zeros((BLK, BLK), bool), (a >= N_META) | (b < 0),
                    np.zeros((BLK, BLK), bool), (a >= N_META) | (b < 0)])
    idx = _t5_bucket_np(rels)
    tiles = jnp.transpose(rel_bias.astype(F32)[idx], (0, 3, 1, 2))
    return jnp.where(dis[:, None], NEG, tiles)


def _prenorm_kernel(x_ref, w_ref, o_ref):
    x = x_ref[...]
    ms = jnp.mean(x * x, axis=-1, keepdims=True)
    o_ref[...] = (x * lax.rsqrt(ms + EPS) * w_ref[...]).astype(o_ref.dtype)


def _prenorm(hp, w):
    tm = 256
    return pl.pallas_call(
        _prenorm_kernel,
        grid=(ROWS // tm,),
        in_specs=[pl.BlockSpec((tm, D_MODEL), lambda i: (i, 0)),
                  pl.BlockSpec((1, D_MODEL), lambda i: (0, 0))],
        out_specs=pl.BlockSpec((tm, D_MODEL), lambda i: (i, 0)),
        out_shape=jax.ShapeDtypeStruct((ROWS, D_MODEL), BF16),
        name="prenorm",
    )(hp, w)


def _mm_kernel(a_ref, w_ref, o_ref):
    o_ref[...] = jnp.dot(a_ref[...], w_ref[...], preferred_element_type=F32).astype(o_ref.dtype)


def _matmul(a, w, out_dtype, tm, tn, name):
    m, k = a.shape
    n = w.shape[1]
    return pl.pallas_call(
        _mm_kernel,
        grid=(m // tm, n // tn),
        in_specs=[pl.BlockSpec((tm, k), lambda i, j: (i, 0)),
                  pl.BlockSpec((k, tn), lambda i, j: (0, j))],
        out_specs=pl.BlockSpec((tm, tn), lambda i, j: (i, j)),
        out_shape=jax.ShapeDtypeStruct((m, n), out_dtype),
        compiler_params=pltpu.CompilerParams(
            dimension_semantics=("arbitrary", "arbitrary"), vmem_limit_bytes=VMEM_LIMIT),
        name=name,
    )(a, w)


def _kvprep_kernel(c_ref, kvw_ref, ikw_ref, ikb_ref, ckv_ref, ckvt_ref, ik_ref, iwt_ref):
    c = c_ref[...]
    ckv = c[:, :KV_RANK]
    ms = jnp.mean(ckv * ckv, axis=-1, keepdims=True)
    ckvn = ckv * lax.rsqrt(ms + EPS) * kvw_ref[...]
    ckv_ref[...] = ckvn.astype(BF16)
    ckvt_ref[0] = ckvn[:BLK].T.astype(BF16)
    ckvt_ref[1] = ckvn[BLK:].T.astype(BF16)
    ik = c[:, KV_RANK:KV_RANK + IDX_DIM]
    mu = jnp.mean(ik, axis=-1, keepdims=True)
    var = jnp.mean(jnp.square(ik - mu), axis=-1, keepdims=True)
    ikn = (ik - mu) * lax.rsqrt(var + EPS) * ikw_ref[...] + ikb_ref[...]
    ik_ref[...] = ikn.astype(BF16)
    iw = c[:, 3 * BLK:4 * BLK] * (IDX_HEADS ** -0.5 * IDX_DIM ** -0.5)
    iwt_ref[...] = iw.T[:IDX_HEADS, :]


def _kvprep(c, kvw, ikw, ikb):
    tm = 2 * BLK
    return pl.pallas_call(
        _kvprep_kernel,
        grid=(ROWS // tm,),
        in_specs=[pl.BlockSpec((tm, 4 * BLK), lambda i: (i, 0)),
                  pl.BlockSpec((1, KV_RANK), lambda i: (0, 0)),
                  pl.BlockSpec((1, IDX_DIM), lambda i: (0, 0)),
                  pl.BlockSpec((1, IDX_DIM), lambda i: (0, 0))],
        out_specs=[pl.BlockSpec((tm, KV_RANK), lambda i: (i, 0)),
                   pl.BlockSpec((2, KV_RANK, BLK), lambda i: (i, 0, 0)),
                   pl.BlockSpec((tm, IDX_DIM), lambda i: (i, 0)),
                   pl.BlockSpec((IDX_HEADS, tm), lambda i: (0, i))],
        out_shape=[jax.ShapeDtypeStruct((ROWS, KV_RANK), BF16),
                   jax.ShapeDtypeStruct((ROWS // BLK, KV_RANK, BLK), BF16),
                   jax.ShapeDtypeStruct((ROWS, IDX_DIM), BF16),
                   jax.ShapeDtypeStruct((IDX_HEADS, ROWS), F32)],
        name="kvprep",
    )(c, kvw, ikw, ikb)


def _tile_index(kb, i):
    return jnp.where(kb == i + 1, 0, jnp.where(kb == 0, jnp.where(i == 0, 2, 4),
                                               jnp.where(kb == i, 1, 3)))


def _attn_a_kernel(qa_ref, iq_ref, iwt_ref, ckv_ref, ckvt_ref, ik_ref, wuk_ref, wuvt_ref, bias_ref,
                   o_ref, keys_ref, qlat_ref, acc_ref):
    i = pl.program_id(1)
    nkb = i + 2
    lanes = A_HEADS * BLK

    for h in range(A_HEADS):
        qh = qa_ref[:, h * BLK:(h + 1) * BLK]
        ql = lax.dot_general(wuk_ref[h], qh, NT_DIMS, preferred_element_type=F32)
        qlat_ref[:, h * BLK:(h + 1) * BLK] = ql.astype(BF16)

    iwt = iwt_ref[...]
    row = lax.broadcasted_iota(jnp.int32, (BLK, BLK), 0)
    lane = lax.broadcasted_iota(jnp.int32, (BLK, BLK), 1)

    def idx_body(kb, carry):
        off = pl.multiple_of(kb * BLK, BLK)
        ikb = ik_ref[pl.ds(off, BLK), :]
        sc = jnp.zeros((BLK, BLK), F32)
        for h in range(IDX_HEADS):
            s = lax.dot_general(ikb, iq_ref[:, h * IDX_DIM:(h + 1) * IDX_DIM], NT_DIMS,
                                preferred_element_type=F32)
            sc = sc + jnp.maximum(s, 0.0) * iwt[h:h + 1, :]
        bits = lax.bitcast_convert_type(sc, jnp.int32)
        key = bits ^ ((bits >> 31) & 0x7FFFFFFF)
        dis = ((kb == i + 1) & (row >= CHUNK) & (lane < CHUNK)) | ((kb == 0) & (row >= N_META))
        keys_ref[pl.ds(off, BLK), :] = jnp.where(dis, INT_MIN, key)
        return carry

    lax.fori_loop(0, nkb, idx_body, 0)

    def count(pred_fn):
        def body(kb, c):
            off = pl.multiple_of(kb * BLK, BLK)
            k = keys_ref[pl.ds(off, BLK), :]
            return c + jnp.sum(pred_fn(k, off).astype(jnp.int32), axis=0, keepdims=True)
        return lax.fori_loop(0, nkb, body, jnp.zeros((1, BLK), jnp.int32))

    zero = jnp.zeros((1, BLK), jnp.int32)
    c0 = count(lambda k, off: k >= zero)
    prefix = jnp.where(c0 >= TOPK, 0, INT_MIN).astype(jnp.int32)

    def bit_body(t, prefix):
        cand = prefix | jnp.left_shift(jnp.int32(1), 30 - t)
        c = count(lambda k, off: k >= cand)
        return jnp.where(c >= TOPK, cand, prefix)

    thr = lax.fori_loop(0, 31, bit_body, prefix)

    full = thr == INT_MIN
    c_gt = count(lambda k, off: k > thr)
    c_eq = count(lambda k, off: k == thr)
    need = TOPK - c_gt
    tied = jnp.logical_and(jnp.logical_not(full), c_eq > need)
    j_default = jnp.where(full, -1, TP).astype(jnp.int32)

    def tie_search():
        def jbit(t, j):
            cand = j | jnp.left_shift(jnp.int32(1), 12 - t)
            c = count(lambda k, off: (k == thr) & ((off + row) < cand))
            return jnp.where(c < need, cand, j)
        j = lax.fori_loop(0, 13, jbit, jnp.zeros((1, BLK), jnp.int32))
        return jnp.where(tied, j, j_default)

    any_tied = jnp.max(tied.astype(jnp.int32)) > 0
    jmax = lax.cond(any_tied, tie_search, lambda: j_default)

    acc_ref[...] = jnp.zeros_like(acc_ref)

    def att_body(kb, carry):
        m, l = carry
        off = pl.multiple_of(kb * BLK, BLK)
        s = jnp.dot(ckv_ref[pl.ds(off, BLK), :], qlat_ref[...], preferred_element_type=F32)
        s = s * (A_HEAD_DIM ** -0.5)
        key = keys_ref[pl.ds(off, BLK), :]
        sel = (key > thr) | ((key == thr) & ((off + row) <= jmax))
        selb = jnp.where(sel, 0.0, NEG)
        s = s + bias_ref[_tile_index(kb, i)] + jnp.concatenate([selb] * A_HEADS, axis=1)
        m_new = jnp.maximum(m, jnp.max(s, axis=0, keepdims=True))
        alpha = jnp.exp(m - m_new)
        p = jnp.exp(s - m_new)
        l_new = alpha * l + jnp.sum(p, axis=0, keepdims=True)
        pv = jnp.dot(ckvt_ref[kb], p.astype(BF16), preferred_element_type=F32)
        acc_ref[...] = acc_ref[...] * alpha + pv
        return m_new, l_new

    m0 = jnp.full((1, lanes), NEG, F32)
    l0 = jnp.zeros((1, lanes), F32)
    _, l = lax.fori_loop(0, nkb, att_body, (m0, l0))

    olat = (acc_ref[...] / l).astype(BF16)
    for h in range(A_HEADS):
        ot = jnp.dot(wuvt_ref[h], olat[:, h * BLK:(h + 1) * BLK], preferred_element_type=F32)
        o_ref[:, h * BLK:(h + 1) * BLK] = ot.T


def _attn_a(proj_a, iwt, ckv, ckvt, ik, wuk, wuvt, bias_a):
    qrow = lambda b, i: b * NKB + 1 + i
    return pl.pallas_call(
        _attn_a_kernel,
        grid=(BATCH, NQB),
        in_specs=[
            pl.BlockSpec((BLK, A_WIDTH), lambda b, i: (qrow(b, i), 0)),
            pl.BlockSpec((BLK, IDX_HEADS * IDX_DIM), lambda b, i: (qrow(b, i), 1)),
            pl.BlockSpec((IDX_HEADS, BLK), lambda b, i: (0, qrow(b, i))),
            pl.BlockSpec((None, TP, KV_RANK), lambda b, i: (b, 0, 0)),
            pl.BlockSpec((None, NKB, KV_RANK, BLK), lambda b, i: (b, 0, 0, 0)),
            pl.BlockSpec((None, TP, IDX_DIM), lambda b, i: (b, 0, 0)),
            pl.BlockSpec((A_HEADS, KV_RANK, A_HEAD_DIM), lambda b, i: (0, 0, 0)),
            pl.BlockSpec((A_HEADS, A_HEAD_DIM, KV_RANK), lambda b, i: (0, 0, 0)),
            pl.BlockSpec((5, BLK, A_HEADS * BLK), lambda b, i: (0, 0, 0)),
        ],
        out_specs=pl.BlockSpec((BLK, A_WIDTH), lambda b, i: (b * NQB + i, 0)),
        out_shape=jax.ShapeDtypeStruct((BATCH * SEQ, A_WIDTH), F32),
        scratch_shapes=[pltpu.VMEM((TP, BLK), jnp.int32),
                        pltpu.VMEM((KV_RANK, A_HEADS * BLK), BF16),
                        pltpu.VMEM((KV_RANK, A_HEADS * BLK), F32)],
        compiler_params=pltpu.CompilerParams(
            dimension_semantics=("arbitrary", "arbitrary"), vmem_limit_bytes=VMEM_LIMIT),
        name="attn_a",
    )(proj_a, proj_a, iwt, ckv.reshape(BATCH, TP, KV_RANK), ckvt.reshape(BATCH, NKB, KV_RANK, BLK),
      ik.reshape(BATCH, TP, IDX_DIM), wuk, wuvt, bias_a)


def _attn_b_kernel(lam_ref, q_ref, k_ref, vt_ref, bias_ref, subw_ref, o_ref, acc_ref, *, lam_init):
    i = pl.program_id(2)
    nkb = i + 2
    lp = lam_ref[...]
    lam = (jnp.exp(jnp.sum(lp[0:1] * lp[1:2], axis=-1, keepdims=True))
           - jnp.exp(jnp.sum(lp[2:3] * lp[3:4], axis=-1, keepdims=True)) + lam_init)

    q = q_ref[...] * (B_QK_DIM ** -0.5)
    lane = lax.broadcasted_iota(jnp.int32, (BLK, BLK), 1)
    zq = jnp.zeros_like(q)
    qbd = jnp.concatenate([jnp.where(lane < B_QK_DIM, q, zq), jnp.where(lane >= B_QK_DIM, q, zq)], axis=0)

    acc_ref[...] = jnp.zeros_like(acc_ref)

    def body(kb, carry):
        m, l = carry
        off = pl.multiple_of(kb * BLK, BLK)
        s = lax.dot_general(k_ref[pl.ds(off, BLK), :], qbd, NT_DIMS, preferred_element_type=F32)
        b = bias_ref[_tile_index(kb, i)]
        s = s + jnp.concatenate([b, b], axis=1)
        m_new = jnp.maximum(m, jnp.max(s, axis=0, keepdims=True))
        alpha = jnp.exp(m - m_new)
        p = jnp.exp(s - m_new)
        l_new = alpha * l + jnp.sum(p, axis=0, keepdims=True)
        pv = jnp.dot(vt_ref[kb], p.astype(BF16), preferred_element_type=F32)
        acc_ref[...] = acc_ref[...] * alpha + pv
        return m_new, l_new

    m0 = jnp.full((1, 2 * BLK), NEG, F32)
    l0 = jnp.zeros((1, 2 * BLK), F32)
    _, l = lax.fori_loop(0, nkb, body, (m0, l0))

    a = acc_ref[...] / l
    o = a[:, :BLK] - lam * a[:, BLK:]
    ms = jnp.mean(o * o, axis=0, keepdims=True)
    y = o * lax.rsqrt(ms + EPS) * subw_ref[...] * (1.0 - lam_init)
    o_ref[...] = y.T


def _attn_b(proj_a, vt, bias_b, lam_p, subw, lam_init):
    qrow = lambda b, h, i: b * NKB + 1 + i
    qcol0 = (A_WIDTH + IDX_HEADS * IDX_DIM) // BLK
    kcol0 = qcol0 + 2 * B_HEADS * B_QK_DIM // BLK
    return pl.pallas_call(
        functools.partial(_attn_b_kernel, lam_init=lam_init),
        grid=(BATCH, B_HEADS, NQB),
        in_specs=[
            pl.BlockSpec((4, B_QK_DIM), lambda b, h, i: (0, 0)),
            pl.BlockSpec((BLK, BLK), lambda b, h, i: (qrow(b, h, i), qcol0 + h)),
            pl.BlockSpec((None, TP, BLK), lambda b, h, i: (b, 0, kcol0 + h)),
            pl.BlockSpec((None, None, NKB, B_V_DIM, BLK), lambda b, h, i: (b, h, 0, 0, 0)),
            pl.BlockSpec((5, None, BLK, BLK), lambda b, h, i: (0, h, 0, 0)),
            pl.BlockSpec((B_V_DIM, BLK), lambda b, h, i: (0, 0)),
        ],
        out_specs=pl.BlockSpec((BLK, B_V_DIM), lambda b, h, i: (b * NQB + i, h)),
        out_shape=jax.ShapeDtypeStruct((BATCH * SEQ, B_WIDTH), F32),
        scratch_shapes=[pltpu.VMEM((B_V_DIM, 2 * BLK), F32)],
        compiler_params=pltpu.CompilerParams(
            dimension_semantics=("arbitrary", "arbitrary", "arbitrary"), vmem_limit_bytes=VMEM_LIMIT),
        name="attn_b",
    )(lam_p, proj_a, proj_a.reshape(BATCH, TP, -1), vt, bias_b, subw)


def _out_kernel(oa_ref, za_ref, ob_ref, zb_ref, ga_ref, gb_ref, x_ref, woa_ref, wob_ref, wout_ref, pw_ref, o_ref):
    a = (oa_ref[...] * jax.nn.silu(za_ref[...])).astype(BF16)
    ya = jnp.dot(a, woa_ref[...], preferred_element_type=F32)
    b = (ob_ref[...] * jax.nn.silu(zb_ref[...])).astype(BF16)
    yb = jnp.dot(b, wob_ref[...], preferred_element_type=F32)
    mix = jax.nn.sigmoid(ga_ref[...]) * ya + jax.nn.sigmoid(gb_ref[...]) * yb
    out = jnp.dot(mix.astype(BF16), wout_ref[...], preferred_element_type=F32)
    ms = jnp.mean(out * out, axis=-1, keepdims=True)
    o_ref[...] = x_ref[...] + out * lax.rsqrt(ms + EPS) * pw_ref[...]


def _out_stage(o_a, o_b, proj_b, x2, woa, wob, wout, pw):
    tm = BLK
    prow = lambda g: g + g // NQB + 1
    const = lambda g: (0, 0)
    return pl.pallas_call(
        _out_kernel,
        grid=(BATCH * NQB,),
        in_specs=[
            pl.BlockSpec((tm, A_WIDTH), lambda g: (g, 0)),
            pl.BlockSpec((tm, A_WIDTH), lambda g: (prow(g), 0)),
            pl.BlockSpec((tm, B_WIDTH), lambda g: (g, 0)),
            pl.BlockSpec((tm, B_WIDTH), lambda g: (prow(g), 1)),
            pl.BlockSpec((tm, D_MODEL), lambda g: (prow(g), 1)),
            pl.BlockSpec((tm, D_MODEL), lambda g: (prow(g), 2)),
            pl.BlockSpec((tm, D_MODEL), lambda g: (g, 0)),
            pl.BlockSpec((A_WIDTH, D_MODEL), const, pipeline_mode=pl.Buffered(1)),
            pl.BlockSpec((B_WIDTH, D_MODEL), const, pipeline_mode=pl.Buffered(1)),
            pl.BlockSpec((D_MODEL, D_MODEL), const, pipeline_mode=pl.Buffered(1)),
            pl.BlockSpec((1, D_MODEL), const),
        ],
        out_specs=pl.BlockSpec((tm, D_MODEL), lambda g: (g, 0)),
        out_shape=jax.ShapeDtypeStruct((BATCH * SEQ, D_MODEL), F32),
        compiler_params=pltpu.CompilerParams(
            dimension_semantics=("arbitrary",), vmem_limit_bytes=VMEM_LIMIT),
        name="out_stage",
    )(o_a, proj_b, o_b, proj_b, proj_b, proj_b, x2, woa, wob, wout, pw)


def kernel(x, meta_tokens, rel_bias, pre_norm_w, w_in, kv_norm_w, w_uk, w_uv, idx_k_norm_w, idx_k_norm_b,
           diff_lambda, diff_subln_w, w_o_a, w_o_b, w_out, post_norm_w):
    assert x.shape == (BATCH, SEQ, D_MODEL) and w_in.shape[0] == 1
    layer = 0
    lam_init = 0.8 - 0.6 * math.exp(-0.3 * layer)

    head = jnp.concatenate([meta_tokens.astype(F32), jnp.zeros((BLK - N_META, D_MODEL), F32)], axis=0)
    hp = jnp.concatenate([jnp.broadcast_to(head[None], (BATCH, BLK, D_MODEL)), x], axis=1).reshape(ROWS, D_MODEL)

    offs = np.concatenate([[0], np.cumsum(IN_SIZES)])
    w = w_in[0]
    seg = lambda k: w[:, offs[k]:offs[k + 1]]
    zcols = lambda n: jnp.zeros((D_MODEL, n), w.dtype)
    w_a = jnp.concatenate([seg(0), seg(3), seg(6), seg(7), seg(8)], axis=1).astype(BF16)
    w_b = jnp.concatenate([seg(2), seg(9), seg(10), seg(11)], axis=1).astype(BF16)
    w_c = jnp.concatenate([seg(1), seg(4), zcols(BLK - IDX_DIM), seg(5), zcols(BLK - IDX_HEADS)],
                          axis=1).astype(BF16)

    u = _prenorm(hp, pre_norm_w[0][None].astype(F32))
    proj_a = _matmul(u, w_a, BF16, 768, 512, "proj_a")
    proj_b = _matmul(u, w_b, F32, 768, 512, "proj_b")
    proj_c = _matmul(u, w_c, F32, 768, 512, "proj_c")

    ckv, ckvt, ik, iwt = _kvprep(proj_c, kv_norm_w[0][None].astype(F32),
                                 idx_k_norm_w[0][None].astype(F32), idx_k_norm_b[0][None].astype(F32))

    bias = _bias_tiles(rel_bias)
    bias_a = jnp.transpose(bias[:, :A_HEADS], (0, 2, 1, 3)).reshape(5, BLK, A_HEADS * BLK)
    bias_b = bias[:, A_HEADS:]

    wuk = jnp.transpose(w_uk[0], (1, 0, 2)).astype(BF16)
    wuvt = jnp.transpose(w_uv[0], (1, 2, 0)).astype(BF16)
    o_a = _attn_a(proj_a, iwt, ckv, ckvt, ik, wuk, wuvt, bias_a)

    vcol = A_WIDTH + IDX_HEADS * IDX_DIM + 4 * B_HEADS * B_QK_DIM
    vt = jnp.transpose(proj_a[:, vcol:vcol + B_WIDTH].reshape(BATCH, NKB, BLK, B_HEADS, B_V_DIM), (0, 3, 1, 4, 2))
    subw = jnp.broadcast_to(diff_subln_w[0].astype(F32)[:, None], (B_V_DIM, BLK))
    o_b = _attn_b(proj_a, vt, bias_b, diff_lambda[0].astype(F32), subw, lam_init)

    out = _out_stage(o_a, o_b, proj_b, x.reshape(BATCH * SEQ, D_MODEL),
                     w_o_a[0].astype(BF16), w_o_b[0].astype(BF16), w_out[0].astype(BF16),
                     post_norm_w[0][None].astype(F32))
    return out.reshape(BATCH, SEQ, D_MODEL)
```

```python
import functools
import math

import numpy as np
import jax
import jax.numpy as jnp
from jax import lax
from jax.experimental import pallas as pl
from jax.experimental.pallas import tpu as pltpu

D_MODEL = 2048
BATCH = 2
SEQ = 4096
CHUNK = 64
N_META = 16
N_BUCKETS = 32
MAX_DISTANCE = 128
A_HEADS = 8
A_HEAD_DIM = 128
KV_RANK = 256
IDX_HEADS = 16
IDX_DIM = 64
TOPK = 256
B_HEADS = 8
B_QK_DIM = 64
B_V_DIM = 128
A_WIDTH = A_HEADS * A_HEAD_DIM
B_WIDTH = B_HEADS * B_V_DIM
IN_SIZES = (A_WIDTH, KV_RANK, A_WIDTH, IDX_HEADS * IDX_DIM, IDX_DIM, IDX_HEADS,
            2 * B_HEADS * B_QK_DIM, 2 * B_HEADS * B_QK_DIM, B_WIDTH, B_WIDTH,
            D_MODEL, D_MODEL)
EPS = 1e-6

BLK = 128
NQB = SEQ // BLK
NKB = NQB + 1
TP = NKB * BLK
ROWS = BATCH * TP
FAR = 4
HPS = 4
NEG = -1e30
INT_MIN = -2 ** 31
VMEM_LIMIT = 56 * 1024 * 1024

F32 = jnp.float32
BF16 = jnp.bfloat16
NT_DIMS = (((1,), (1,)), ((), ()))


def _t5_bucket_np(rel):
    nb = N_BUCKETS // 2
    max_exact = nb // 2
    ret = np.where(rel > 0, nb, 0)
    n = np.abs(rel)
    nf = np.maximum(n, 1).astype(np.float32)
    large = max_exact + (np.log(nf / np.float32(max_exact))
                         / np.float32(math.log(MAX_DISTANCE / max_exact))
                         * np.float32(nb - max_exact)).astype(np.int32)
    large = np.minimum(large, nb - 1)
    return ret + np.where(n < max_exact, n, large)


T_DIAG, T_PREV, T_META0, T_METAFAR, T_NONE = range(5)


def _bias_tiles(rel_bias):
    a = np.arange(BLK)[:, None]
    b = np.arange(BLK)[None, :]
    far = np.full((BLK, BLK), -4 * BLK)
    nowhere = np.zeros((BLK, BLK), bool)
    pad_rows = (a >= N_META) | nowhere
    rels = np.stack([a - b, a - b - BLK, a - N_META - b, far, far])
    dis = np.stack([(a >= CHUNK) & (b < CHUNK), nowhere, pad_rows, pad_rows, ~nowhere])
    idx = _t5_bucket_np(rels)
    far_bucket = N_BUCKETS // 2 - 1
    assert _t5_bucket_np(np.array([-BLK - 1]))[0] == far_bucket == idx[T_METAFAR, 0, 0]
    rb = rel_bias.astype(F32)
    tiles = jnp.zeros((5, A_HEADS + B_HEADS, BLK, BLK), F32)
    for k in range(N_BUCKETS):
        tiles = jnp.where((idx == k)[:, None], rb[k][None, :, None, None], tiles)
    tiles = tiles - rb[far_bucket][None, :, None, None]
    return jnp.where(dis[:, None], NEG, tiles)


def _far_split(i):
    n_far = jnp.maximum(i - 1, 0)
    return n_far, lax.shift_right_logical(n_far, 2)


def _visit_key_blocks(i, group_fn, special_fn, carry):
    n_far, n_chunks = _far_split(i)
    carry = lax.fori_loop(0, n_chunks, lambda c, cr: group_fn(1 + FAR * c, FAR, cr), carry)
    carry = lax.fori_loop(1 + FAR * n_chunks, 1 + n_far, lambda kb, cr: group_fn(kb, 1, cr), carry)
    return special_fn(carry)


def _prenorm_kernel(x_ref, w_ref, o_ref):
    x = x_ref[...]
    ms = jnp.mean(x * x, axis=-1, keepdims=True)
    o_ref[...] = (x * lax.rsqrt(ms + EPS) * w_ref[...]).astype(o_ref.dtype)


def _prenorm(hp, w):
    tm = 256
    return pl.pallas_call(
        _prenorm_kernel,
        grid=(ROWS // tm,),
        in_specs=[pl.BlockSpec((tm, D_MODEL), lambda i: (i, 0)),
                  pl.BlockSpec((1, D_MODEL), lambda i: (0, 0))],
        out_specs=pl.BlockSpec((tm, D_MODEL), lambda i: (i, 0)),
        out_shape=jax.ShapeDtypeStruct((ROWS, D_MODEL), BF16),
        name="prenorm",
    )(hp, w)


def _mm_kernel(a_ref, w_ref, o_ref):
    o_ref[...] = jnp.dot(a_ref[...], w_ref[...], preferred_element_type=F32).astype(o_ref.dtype)


def _matmul(a, w, out_dtype, tm, tn, name):
    m, k = a.shape
    n = w.shape[1]
    return pl.pallas_call(
        _mm_kernel,
        grid=(m // tm, n // tn),
        in_specs=[pl.BlockSpec((tm, k), lambda i, j: (i, 0)),
                  pl.BlockSpec((k, tn), lambda i, j: (0, j))],
        out_specs=pl.BlockSpec((tm, tn), lambda i, j: (i, j)),
        out_shape=jax.ShapeDtypeStruct((m, n), out_dtype),
        compiler_params=pltpu.CompilerParams(
            dimension_semantics=("arbitrary", "arbitrary"), vmem_limit_bytes=VMEM_LIMIT),
        name=name,
    )(a, w)


def _kvprep_kernel(c_ref, kvw_ref, ikw_ref, ikb_ref, ckv_ref, ckvt_ref, ik_ref, iwt_ref):
    c = c_ref[...]
    ckv = c[:, :KV_RANK]
    ms = jnp.mean(ckv * ckv, axis=-1, keepdims=True)
    ckvn = ckv * lax.rsqrt(ms + EPS) * kvw_ref[...]
    ckv_ref[...] = ckvn.astype(BF16)
    ckvt_ref[0] = ckvn[:BLK].T.astype(BF16)
    ckvt_ref[1] = ckvn[BLK:].T.astype(BF16)
    ik = c[:, KV_RANK:KV_RANK + IDX_DIM]
    mu = jnp.mean(ik, axis=-1, keepdims=True)
    var = jnp.mean(jnp.square(ik - mu), axis=-1, keepdims=True)
    ikn = (ik - mu) * lax.rsqrt(var + EPS) * ikw_ref[...] + ikb_ref[...]
    ik_ref[...] = ikn.astype(BF16)
    iw = c[:, 3 * BLK:4 * BLK] * (IDX_HEADS ** -0.5 * IDX_DIM ** -0.5)
    iwt_ref[...] = iw.T[:IDX_HEADS, :]


def _kvprep(c, kvw, ikw, ikb):
    tm = 2 * BLK
    return pl.pallas_call(
        _kvprep_kernel,
        grid=(ROWS // tm,),
        in_specs=[pl.BlockSpec((tm, 4 * BLK), lambda i: (i, 0)),
                  pl.BlockSpec((1, KV_RANK), lambda i: (0, 0)),
                  pl.BlockSpec((1, IDX_DIM), lambda i: (0, 0)),
                  pl.BlockSpec((1, IDX_DIM), lambda i: (0, 0))],
        out_specs=[pl.BlockSpec((tm, KV_RANK), lambda i: (i, 0)),
                   pl.BlockSpec((2, KV_RANK, BLK), lambda i: (i, 0, 0)),
                   pl.BlockSpec((tm, IDX_DIM), lambda i: (i, 0)),
                   pl.BlockSpec((IDX_HEADS, tm), lambda i: (0, i))],
        out_shape=[jax.ShapeDtypeStruct((ROWS, KV_RANK), BF16),
                   jax.ShapeDtypeStruct((ROWS // BLK, KV_RANK, BLK), BF16),
                   jax.ShapeDtypeStruct((ROWS, IDX_DIM), BF16),
                   jax.ShapeDtypeStruct((IDX_HEADS, ROWS), F32)],
        name="kvprep",
    )(c, kvw, ikw, ikb)


def _online_softmax_step(s, m, l):
    m_new = jnp.maximum(m, jnp.max(s, axis=0, keepdims=True))
    alpha = jnp.exp(m - m_new)
    p = jnp.exp(s - m_new)
    return m_new, alpha * l + jnp.sum(p, axis=0, keepdims=True), alpha, p


def _attn_a_kernel(qa_ref, iq_ref, iwt_ref, ckv_ref, ckvt_ref, ik_ref, wuk_ref, wuvt_ref, bias_ref,
                   o_ref, keys_ref, qlat_ref, acc_ref, iqt_ref):
    i = pl.program_id(1)
    nkb = i + 2
    lanes = A_HEADS * BLK
    t_meta = jnp.where(i == 0, T_META0, T_METAFAR)
    t_prev = jnp.where(i == 0, T_NONE, T_PREV)
    special_blocks = (0, i, i + 1)

    for h in range(A_HEADS):
        qh = qa_ref[:, h * BLK:(h + 1) * BLK]
        ql = lax.dot_general(wuk_ref[h], qh, NT_DIMS, preferred_element_type=F32)
        qlat_ref[:, h * BLK:(h + 1) * BLK] = ql.astype(BF16)

    for pr in range(IDX_HEADS // 2):
        t = iq_ref[:, pr * BLK:(pr + 1) * BLK].astype(F32).T
        iqt_ref[:, (2 * pr) * BLK:(2 * pr + 1) * BLK] = t[:IDX_DIM].astype(BF16)
        iqt_ref[:, (2 * pr + 1) * BLK:(2 * pr + 2) * BLK] = t[IDX_DIM:].astype(BF16)

    iwt = iwt_ref[...]
    row = lax.broadcasted_iota(jnp.int32, (BLK, BLK), 0)
    lane = lax.broadcasted_iota(jnp.int32, (BLK, BLK), 1)

    def idx_keys(ikrows):
        sc = jnp.zeros((ikrows.shape[0], BLK), F32)
        for pr in range(IDX_HEADS // 2):
            s2 = jnp.dot(ikrows, iqt_ref[:, pr * 2 * BLK:(pr + 1) * 2 * BLK], preferred_element_type=F32)
            sc = sc + jnp.maximum(s2[:, :BLK], 0.0) * iwt[2 * pr:2 * pr + 1, :]
            sc = sc + jnp.maximum(s2[:, BLK:], 0.0) * iwt[2 * pr + 1:2 * pr + 2, :]
        bits = lax.bitcast_convert_type(sc, jnp.int32)
        return bits ^ ((bits >> 31) & 0x7FFFFFFF)

    def idx_group(kb, n, carry):
        off = pl.multiple_of(kb * BLK, BLK)
        keys_ref[pl.ds(off, n * BLK), :] = idx_keys(ik_ref[pl.ds(off, n * BLK), :])
        return carry

    def idx_special(carry):
        masks = (None, None, (row >= CHUNK) & (lane < CHUNK))
        for kb, dis in zip((i, i + 1), masks[1:]):
            off = pl.multiple_of(kb * BLK, BLK)
            key = idx_keys(ik_ref[pl.ds(off, BLK), :])
            keys_ref[pl.ds(off, BLK), :] = key if dis is None else jnp.where(dis, INT_MIN, key)
        key = idx_keys(ik_ref[pl.ds(0, BLK), :])
        keys_ref[pl.ds(0, BLK), :] = jnp.where(row >= N_META, INT_MIN, key)
        return carry

    _visit_key_blocks(i, idx_group, idx_special, 0)
    keys_ref[pl.ds(pl.multiple_of(nkb * BLK, BLK), (FAR - 1) * BLK), :] = jnp.full(
        ((FAR - 1) * BLK, BLK), INT_MIN, jnp.int32)
    n_search = lax.shift_right_logical(nkb + FAR - 1, 2)
    crow = lax.broadcasted_iota(jnp.int32, (FAR * BLK, BLK), 0)

    def count(pred_fn):
        def body(c, acc8):
            off = pl.multiple_of(c * FAR * BLK, FAR * BLK)
            k = keys_ref[pl.ds(off, FAR * BLK), :]
            hit = pred_fn(k, off).astype(jnp.int32)
            return acc8 + jnp.sum(hit.reshape(FAR * BLK // 8, 8, BLK), axis=0)
        acc8 = lax.fori_loop(0, n_search, body, jnp.zeros((8, BLK), jnp.int32))
        return jnp.sum(acc8, axis=0, keepdims=True)

    zero = jnp.zeros((1, BLK), jnp.int32)
    c0 = count(lambda k, off: k >= zero)
    prefix = jnp.where(c0 >= TOPK, 0, INT_MIN).astype(jnp.int32)

    def bit_body(t, prefix):
        cand = prefix | jnp.left_shift(jnp.int32(1), 30 - t)
        c = count(lambda k, off: k >= cand)
        return jnp.where(c >= TOPK, cand, prefix)

    thr = lax.fori_loop(0, 31, bit_body, prefix)

    full = thr == INT_MIN
    c_gt = count(lambda k, off: k > thr)
    c_eq = count(lambda k, off: k == thr)
    need = TOPK - c_gt
    tied = jnp.logical_and(jnp.logical_not(full), c_eq > need)
    j_default = jnp.where(full, -1, TP).astype(jnp.int32)

    def tie_search():
        def jbit(t, j):
            cand = j | jnp.left_shift(jnp.int32(1), 12 - t)
            c = count(lambda k, off: (k == thr) & ((off + crow) < cand))
            return jnp.where(c < need, cand, j)
        j = lax.fori_loop(0, 13, jbit, jnp.zeros((1, BLK), jnp.int32))
        return jnp.where(tied, j, j_default)

    any_tied = jnp.max(tied.astype(jnp.int32)) > 0
    jmax = lax.cond(any_tied, tie_search, lambda: j_default)

    acc_ref[...] = jnp.zeros_like(acc_ref)

    def att_update(rows, keys, rowidx, vt, bias, carry):
        m, l = carry
        s = jnp.dot(rows, qlat_ref[...], preferred_element_type=F32) * (A_HEAD_DIM ** -0.5)
        sel = (keys > thr) | ((keys == thr) & (rowidx <= jmax))
        selb = jnp.where(sel, 0.0, NEG)
        add = jnp.concatenate([selb] * A_HEADS, axis=1)
        if bias is not None:
            add = add + bias
        m_new, l_new, alpha, p = _online_softmax_step(s + add, m, l)
        pv = jnp.dot(vt, p.astype(BF16), preferred_element_type=F32)
        acc_ref[...] = acc_ref[...] * alpha + pv
        return m_new, l_new

    def att_group(kb, n, carry):
        off = pl.multiple_of(kb * BLK, BLK)
        rowidx = off + lax.broadcasted_iota(jnp.int32, (n * BLK, BLK), 0)
        vt = jnp.concatenate([ckvt_ref[kb + u] for u in range(n)], axis=1) if n > 1 else ckvt_ref[kb]
        return att_update(ckv_ref[pl.ds(off, n * BLK), :], keys_ref[pl.ds(off, n * BLK), :], rowidx, vt, None, carry)

    def att_special(carry):
        offs = [pl.multiple_of(kb * BLK, BLK) for kb in special_blocks]
        rows = jnp.concatenate([ckv_ref[pl.ds(o, BLK), :] for o in offs], axis=0)
        keys = jnp.concatenate([keys_ref[pl.ds(o, BLK), :] for o in offs], axis=0)
        rowidx = jnp.concatenate([o + row for o in offs], axis=0)
        vt = jnp.concatenate([ckvt_ref[kb] for kb in special_blocks], axis=1)
        bias = jnp.concatenate([bias_ref[t_meta], bias_ref[t_prev], bias_ref[T_DIAG]], axis=0)
        return att_update(rows, keys, rowidx, vt, bias, carry)

    m0 = jnp.full((1, lanes), NEG, F32)
    l0 = jnp.zeros((1, lanes), F32)
    _, l = _visit_key_blocks(i, att_group, att_special, (m0, l0))

    olat = (acc_ref[...] / l).astype(BF16)
    for h in range(A_HEADS):
        ot = jnp.dot(wuvt_ref[h], olat[:, h * BLK:(h + 1) * BLK], preferred_element_type=F32)
        o_ref[:, h * BLK:(h + 1) * BLK] = ot.T


def _attn_a(proj_a, iwt, ckv, ckvt, ik, wuk, wuvt, bias_a):
    qrow = lambda b, i: b * NKB + 1 + i
    return pl.pallas_call(
        _attn_a_kernel,
        grid=(BATCH, NQB),
        in_specs=[
            pl.BlockSpec((BLK, A_WIDTH), lambda b, i: (qrow(b, i), 0)),
            pl.BlockSpec((BLK, IDX_HEADS * IDX_DIM), lambda b, i: (qrow(b, i), 1)),
            pl.BlockSpec((IDX_HEADS, BLK), lambda b, i: (0, qrow(b, i))),
            pl.BlockSpec((None, TP, KV_RANK), lambda b, i: (b, 0, 0)),
            pl.BlockSpec((None, NKB, KV_RANK, BLK), lambda b, i: (b, 0, 0, 0)),
            pl.BlockSpec((None, TP, IDX_DIM), lambda b, i: (b, 0, 0)),
            pl.BlockSpec((A_HEADS, KV_RANK, A_HEAD_DIM), lambda b, i: (0, 0, 0)),
            pl.BlockSpec((A_HEADS, A_HEAD_DIM, KV_RANK), lambda b, i: (0, 0, 0)),
            pl.BlockSpec((5, BLK, A_HEADS * BLK), lambda b, i: (0, 0, 0)),
        ],
        out_specs=pl.BlockSpec((BLK, A_WIDTH), lambda b, i: (b * NQB + i, 0)),
        out_shape=jax.ShapeDtypeStruct((BATCH * SEQ, A_WIDTH), F32),
        scratch_shapes=[pltpu.VMEM(((NKB + FAR - 1) * BLK, BLK), jnp.int32),
                        pltpu.VMEM((KV_RANK, A_HEADS * BLK), BF16),
                        pltpu.VMEM((KV_RANK, A_HEADS * BLK), F32),
                        pltpu.VMEM((IDX_DIM, IDX_HEADS * BLK), BF16)],
        compiler_params=pltpu.CompilerParams(
            dimension_semantics=("arbitrary", "arbitrary"), vmem_limit_bytes=VMEM_LIMIT),
        name="attn_a",
    )(proj_a, proj_a, iwt, ckv.reshape(BATCH, TP, KV_RANK), ckvt.reshape(BATCH, NKB, KV_RANK, BLK),
      ik.reshape(BATCH, TP, IDX_DIM), wuk, wuvt, bias_a)


def _attn_b_kernel(lam_ref, q_ref, k_ref, vt_ref, bias_ref, subw_ref, o_ref, acc_ref, *, lam_init):
    i = pl.program_id(2)
    t_meta = jnp.where(i == 0, T_META0, T_METAFAR)
    t_prev = jnp.where(i == 0, T_NONE, T_PREV)
    special_blocks = (0, i, i + 1)
    lp = lam_ref[...]
    lam = (jnp.exp(jnp.sum(lp[0:1] * lp[1:2], axis=-1, keepdims=True))
           - jnp.exp(jnp.sum(lp[2:3] * lp[3:4], axis=-1, keepdims=True)) + lam_init)

    lane = lax.broadcasted_iota(jnp.int32, (BLK, BLK), 1)
    qbd = []
    for hh in range(HPS):
        q = q_ref[:, hh * BLK:(hh + 1) * BLK] * (B_QK_DIM ** -0.5)
        zq = jnp.zeros_like(q)
        qbd.append(jnp.concatenate([jnp.where(lane < B_QK_DIM, q, zq), jnp.where(lane >= B_QK_DIM, q, zq)], axis=0))

    acc_ref[...] = jnp.zeros_like(acc_ref)

    def update(hh, rows, vt, bias, carry):
        m, l = carry
        s = lax.dot_general(rows, qbd[hh], NT_DIMS, preferred_element_type=F32)
        if bias is not None:
            s = s + jnp.concatenate([bias, bias], axis=1)
        m_new, l_new, alpha, p = _online_softmax_step(s, m, l)
        pv = jnp.dot(vt, p.astype(BF16), preferred_element_type=F32)
        acc_ref[hh] = acc_ref[hh] * alpha + pv
        return m_new, l_new

    def group(kb, n, carry):
        off = pl.multiple_of(kb * BLK, BLK)
        out = []
        for hh in range(HPS):
            vt = jnp.concatenate([vt_ref[hh, kb + u] for u in range(n)], axis=1) if n > 1 else vt_ref[hh, kb]
            out.append(update(hh, k_ref[pl.ds(off, n * BLK), hh * BLK:(hh + 1) * BLK], vt, None, carry[hh]))
        return tuple(out)

    def special(carry):
        offs = [pl.multiple_of(kb * BLK, BLK) for kb in special_blocks]
        out = []
        for hh in range(HPS):
            rows = jnp.concatenate([k_ref[pl.ds(o, BLK), hh * BLK:(hh + 1) * BLK] for o in offs], axis=0)
            vt = jnp.concatenate([vt_ref[hh, kb] for kb in special_blocks], axis=1)
            bias = jnp.concatenate([bias_ref[t_meta, hh], bias_ref[t_prev, hh], bias_ref[T_DIAG, hh]], axis=0)
            out.append(update(hh, rows, vt, bias, carry[hh]))
        return tuple(out)

    m0 = jnp.full((1, 2 * BLK), NEG, F32)
    l0 = jnp.zeros((1, 2 * BLK), F32)
    stats = _visit_key_blocks(i, group, special, tuple((m0, l0) for _ in range(HPS)))

    for hh in range(HPS):
        a = acc_ref[hh] / stats[hh][1]
        o = a[:, :BLK] - lam * a[:, BLK:]
        ms = jnp.mean(o * o, axis=0, keepdims=True)
        y = o * lax.rsqrt(ms + EPS) * subw_ref[...] * (1.0 - lam_init)
        o_ref[:, hh * BLK:(hh + 1) * BLK] = y.T


def _attn_b(proj_a, vt, bias_b, lam_p, subw, lam_init):
    qrow = lambda b, g, i: b * NKB + 1 + i
    wide = HPS * BLK
    qcol0 = (A_WIDTH + IDX_HEADS * IDX_DIM) // wide
    kcol0 = qcol0 + 2 * B_HEADS * B_QK_DIM // wide
    return pl.pallas_call(
        functools.partial(_attn_b_kernel, lam_init=lam_init),
        grid=(BATCH, B_HEADS // HPS, NQB),
        in_specs=[
            pl.BlockSpec((4, B_QK_DIM), lambda b, g, i: (0, 0)),
            pl.BlockSpec((BLK, wide), lambda b, g, i: (qrow(b, g, i), qcol0 + g)),
            pl.BlockSpec((None, TP, wide), lambda b, g, i: (b, 0, kcol0 + g)),
            pl.BlockSpec((None, HPS, NKB, B_V_DIM, BLK), lambda b, g, i: (b, g, 0, 0, 0)),
            pl.BlockSpec((5, HPS, BLK, BLK), lambda b, g, i: (0, g, 0, 0)),
            pl.BlockSpec((B_V_DIM, BLK), lambda b, g, i: (0, 0)),
        ],
        out_specs=pl.BlockSpec((BLK, wide), lambda b, g, i: (b * NQB + i, g)),
        out_shape=jax.ShapeDtypeStruct((BATCH * SEQ, B_WIDTH), F32),
        scratch_shapes=[pltpu.VMEM((HPS, B_V_DIM, 2 * BLK), F32)],
        compiler_params=pltpu.CompilerParams(
            dimension_semantics=("arbitrary", "arbitrary", "arbitrary"), vmem_limit_bytes=VMEM_LIMIT),
        name="attn_b",
    )(lam_p, proj_a, proj_a.reshape(BATCH, TP, -1), vt, bias_b, subw)


def _out_kernel(oa_ref, za_ref, ob_ref, zb_ref, ga_ref, gb_ref, x_ref, woa_ref, wob_ref, wout_ref, pw_ref, o_ref):
    a = (oa_ref[...] * jax.nn.silu(za_ref[...])).astype(BF16)
    ya = jnp.dot(a, woa_ref[...], preferred_element_type=F32)
    b = (ob_ref[...] * jax.nn.silu(zb_ref[...])).astype(BF16)
    yb = jnp.dot(b, wob_ref[...], preferred_element_type=F32)
    mix = jax.nn.sigmoid(ga_ref[...]) * ya + jax.nn.sigmoid(gb_ref[...]) * yb
    out = jnp.dot(mix.astype(BF16), wout_ref[...], preferred_element_type=F32)
    ms = jnp.mean(out * out, axis=-1, keepdims=True)
    o_ref[...] = x_ref[...] + out * lax.rsqrt(ms + EPS) * pw_ref[...]


def _out_stage(o_a, o_b, proj_b, x2, woa, wob, wout, pw):
    tm = BLK
    prow = lambda g: g + g // NQB + 1
    const = lambda g: (0, 0)
    return pl.pallas_call(
        _out_kernel,
        grid=(BATCH * NQB,),
        in_specs=[
            pl.BlockSpec((tm, A_WIDTH), lambda g: (g, 0)),
            pl.BlockSpec((tm, A_WIDTH), lambda g: (prow(g), 0)),
            pl.BlockSpec((tm, B_WIDTH), lambda g: (g, 0)),
            pl.BlockSpec((tm, B_WIDTH), lambda g: (prow(g), 1)),
            pl.BlockSpec((tm, D_MODEL), lambda g: (prow(g), 1)),
            pl.BlockSpec((tm, D_MODEL), lambda g: (prow(g), 2)),
            pl.BlockSpec((tm, D_MODEL), lambda g: (g, 0)),
            pl.BlockSpec((A_WIDTH, D_MODEL), const, pipeline_mode=pl.Buffered(1)),
            pl.BlockSpec((B_WIDTH, D_MODEL), const, pipeline_mode=pl.Buffered(1)),
            pl.BlockSpec((D_MODEL, D_MODEL), const, pipeline_mode=pl.Buffered(1)),
            pl.BlockSpec((1, D_MODEL), const),
        ],
        out_specs=pl.BlockSpec((tm, D_MODEL), lambda g: (g, 0)),
        out_shape=jax.ShapeDtypeStruct((BATCH * SEQ, D_MODEL), F32),
        compiler_params=pltpu.CompilerParams(
            dimension_semantics=("arbitrary",), vmem_limit_bytes=VMEM_LIMIT),
        name="out_stage",
    )(o_a, proj_b, o_b, proj_b, proj_b, proj_b, x2, woa, wob, wout, pw)


def kernel(x, meta_tokens, rel_bias, pre_norm_w, w_in, kv_norm_w, w_uk, w_uv, idx_k_norm_w, idx_k_norm_b,
           diff_lambda, diff_subln_w, w_o_a, w_o_b, w_out, post_norm_w):
    assert x.shape == (BATCH, SEQ, D_MODEL) and w_in.shape[0] == 1
    layer = 0
    lam_init = 0.8 - 0.6 * math.exp(-0.3 * layer)

    head = jnp.concatenate([meta_tokens.astype(F32), jnp.zeros((BLK - N_META, D_MODEL), F32)], axis=0)
    hp = jnp.concatenate([jnp.broadcast_to(head[None], (BATCH, BLK, D_MODEL)), x], axis=1).reshape(ROWS, D_MODEL)

    offs = np.concatenate([[0], np.cumsum(IN_SIZES)])
    w = w_in[0]
    seg = lambda k: w[:, offs[k]:offs[k + 1]]
    zcols = lambda n: jnp.zeros((D_MODEL, n), w.dtype)
    w_a = jnp.concatenate([seg(0), seg(3), seg(6), seg(7), seg(8)], axis=1).astype(BF16)
    w_b = jnp.concatenate([seg(2), seg(9), seg(10), seg(11)], axis=1).astype(BF16)
    w_c = jnp.concatenate([seg(1), seg(4), zcols(BLK - IDX_DIM), seg(5), zcols(BLK - IDX_HEADS)],
                          axis=1).astype(BF16)

    u = _prenorm(hp, pre_norm_w[0][None].astype(F32))
    proj_a = _matmul(u, w_a, BF16, 768, 512, "proj_a")
    proj_b = _matmul(u, w_b, F32, 768, 512, "proj_b")
    proj_c = _matmul(u, w_c, F32, 768, 512, "proj_c")

    ckv, ckvt, ik, iwt = _kvprep(proj_c, kv_norm_w[0][None].astype(F32),
                                 idx_k_norm_w[0][None].astype(F32), idx_k_norm_b[0][None].astype(F32))

    bias = _bias_tiles(rel_bias)
    bias_a = jnp.transpose(bias[:, :A_HEADS], (0, 2, 1, 3)).reshape(5, BLK, A_HEADS * BLK)
    bias_b = bias[:, A_HEADS:]

    wuk = jnp.transpose(w_uk[0], (1, 0, 2)).astype(BF16)
    wuvt = jnp.transpose(w_uv[0], (1, 2, 0)).astype(BF16)
    o_a = _attn_a(proj_a, iwt, ckv, ckvt, ik, wuk, wuvt, bias_a)

    vcol = A_WIDTH + IDX_HEADS * IDX_DIM + 4 * B_HEADS * B_QK_DIM
    vt = jnp.transpose(proj_a[:, vcol:vcol + B_WIDTH].reshape(BATCH, NKB, BLK, B_HEADS, B_V_DIM), (0, 3, 1, 4, 2))
    subw = jnp.broadcast_to(diff_subln_w[0].astype(F32)[:, None], (B_V_DIM, BLK))
    o_b = _attn_b(proj_a, vt, bias_b, diff_lambda[0].astype(F32), subw, lam_init)

    out = _out_stage(o_a, o_b, proj_b, x.reshape(BATCH * SEQ, D_MODEL),
                     w_o_a[0].astype(BF16), w_o_b[0].astype(BF16), w_out[0].astype(BF16),
                     post_norm_w[0][None].astype(F32))
    return out.reshape(BATCH, SEQ, D_MODEL)
```

```python
import functools
import math

import numpy as np
import jax
import jax.numpy as jnp
from jax import lax
from jax.experimental import pallas as pl
from jax.experimental.pallas import tpu as pltpu

D_MODEL = 2048
BATCH = 2
SEQ = 4096
CHUNK = 64
N_META = 16
N_BUCKETS = 32
MAX_DISTANCE = 128
A_HEADS = 8
A_HEAD_DIM = 128
KV_RANK = 256
IDX_HEADS = 16
IDX_DIM = 64
TOPK = 256
B_HEADS = 8
B_QK_DIM = 64
B_V_DIM = 128
A_WIDTH = A_HEADS * A_HEAD_DIM
B_WIDTH = B_HEADS * B_V_DIM
IN_SIZES = (A_WIDTH, KV_RANK, A_WIDTH, IDX_HEADS * IDX_DIM, IDX_DIM, IDX_HEADS,
            2 * B_HEADS * B_QK_DIM, 2 * B_HEADS * B_QK_DIM, B_WIDTH, B_WIDTH,
            D_MODEL, D_MODEL)
EPS = 1e-6

BLK = 128
NQB = SEQ // BLK
NKB = NQB + 1
TP = NKB * BLK
ROWS = BATCH * TP
FAR = 4
HPS = 4
NEG = -1e30
INT_MIN = -2 ** 31
VMEM_LIMIT = 56 * 1024 * 1024

F32 = jnp.float32
BF16 = jnp.bfloat16
NT_DIMS = (((1,), (1,)), ((), ()))


def _t5_bucket_np(rel):
    nb = N_BUCKETS // 2
    max_exact = nb // 2
    ret = np.where(rel > 0, nb, 0)
    n = np.abs(rel)
    nf = np.maximum(n, 1).astype(np.float32)
    large = max_exact + (np.log(nf / np.float32(max_exact))
                         / np.float32(math.log(MAX_DISTANCE / max_exact))
                         * np.float32(nb - max_exact)).astype(np.int32)
    large = np.minimum(large, nb - 1)
    return ret + np.where(n < max_exact, n, large)


T_DIAG, T_PREV, T_META0, T_METAFAR, T_NONE = range(5)


def _bias_tiles(rel_bias):
    a = np.arange(BLK)[:, None]
    b = np.arange(BLK)[None, :]
    far = np.full((BLK, BLK), -4 * BLK)
    nowhere = np.zeros((BLK, BLK), bool)
    pad_rows = (a >= N_META) | nowhere
    rels = np.stack([a - b, a - b - BLK, a - N_META - b, far, far])
    dis = np.stack([(a >= CHUNK) & (b < CHUNK), nowhere, pad_rows, pad_rows, ~nowhere])
    idx = _t5_bucket_np(rels)
    far_bucket = N_BUCKETS // 2 - 1
    assert _t5_bucket_np(np.array([-BLK - 1]))[0] == far_bucket == idx[T_METAFAR, 0, 0]
    rb = rel_bias.astype(F32)
    tiles = jnp.zeros((5, A_HEADS + B_HEADS, BLK, BLK), F32)
    for k in range(N_BUCKETS):
        tiles = jnp.where((idx == k)[:, None], rb[k][None, :, None, None], tiles)
    tiles = tiles - rb[far_bucket][None, :, None, None]
    return jnp.where(dis[:, None], NEG, tiles)


def _far_split(i):
    n_far = jnp.maximum(i - 1, 0)
    return n_far, lax.shift_right_logical(n_far, 2)


def _visit_key_blocks(i, group_fn, special_fn, carry):
    n_far, n_chunks = _far_split(i)
    carry = lax.fori_loop(0, n_chunks, lambda c, cr: group_fn(1 + FAR * c, FAR, cr), carry)
    carry = lax.fori_loop(1 + FAR * n_chunks, 1 + n_far, lambda kb, cr: group_fn(kb, 1, cr), carry)
    return special_fn(carry)


def _prenorm_kernel(x_ref, w_ref, o_ref):
    x = x_ref[...]
    ms = jnp.mean(x * x, axis=-1, keepdims=True)
    o_ref[...] = (x * lax.rsqrt(ms + EPS) * w_ref[...]).astype(o_ref.dtype)


def _prenorm(hp, w):
    tm = 256
    return pl.pallas_call(
        _prenorm_kernel,
        grid=(ROWS // tm,),
        in_specs=[pl.BlockSpec((tm, D_MODEL), lambda i: (i, 0)),
                  pl.BlockSpec((1, D_MODEL), lambda i: (0, 0))],
        out_specs=pl.BlockSpec((tm, D_MODEL), lambda i: (i, 0)),
        out_shape=jax.ShapeDtypeStruct((ROWS, D_MODEL), BF16),
        name="prenorm",
    )(hp, w)


def _mm_kernel(a_ref, w_ref, o_ref):
    o_ref[...] = jnp.dot(a_ref[...], w_ref[...], preferred_element_type=F32).astype(o_ref.dtype)


def _matmul(a, w, out_dtype, tm, tn, name):
    m, k = a.shape
    n = w.shape[1]
    return pl.pallas_call(
        _mm_kernel,
        grid=(m // tm, n // tn),
        in_specs=[pl.BlockSpec((tm, k), lambda i, j: (i, 0)),
                  pl.BlockSpec((k, tn), lambda i, j: (0, j))],
        out_specs=pl.BlockSpec((tm, tn), lambda i, j: (i, j)),
        out_shape=jax.ShapeDtypeStruct((m, n), out_dtype),
        compiler_params=pltpu.CompilerParams(
            dimension_semantics=("arbitrary", "arbitrary"), vmem_limit_bytes=VMEM_LIMIT),
        name=name,
    )(a, w)


def _kvprep_kernel(c_ref, kvw_ref, ikw_ref, ikb_ref, ckv_ref, ckvt_ref, ik_ref, iwt_ref):
    c = c_ref[...]
    ckv = c[:, :KV_RANK]
    ms = jnp.mean(ckv * ckv, axis=-1, keepdims=True)
    ckvn = ckv * lax.rsqrt(ms + EPS) * kvw_ref[...]
    ckv_ref[...] = ckvn.astype(BF16)
    ckvt_ref[0] = ckvn[:BLK].T.astype(BF16)
    ckvt_ref[1] = ckvn[BLK:].T.astype(BF16)
    ik = c[:, KV_RANK:KV_RANK + IDX_DIM]
    mu = jnp.mean(ik, axis=-1, keepdims=True)
    var = jnp.mean(jnp.square(ik - mu), axis=-1, keepdims=True)
    ikn = (ik - mu) * lax.rsqrt(var + EPS) * ikw_ref[...] + ikb_ref[...]
    ik_ref[...] = ikn.astype(BF16)
    iw = c[:, 3 * BLK:4 * BLK] * (IDX_HEADS ** -0.5 * IDX_DIM ** -0.5)
    iwt_ref[...] = iw.T[:IDX_HEADS, :]


def _kvprep(c, kvw, ikw, ikb):
    tm = 2 * BLK
    return pl.pallas_call(
        _kvprep_kernel,
        grid=(ROWS // tm,),
        in_specs=[pl.BlockSpec((tm, 4 * BLK), lambda i: (i, 0)),
                  pl.BlockSpec((1, KV_RANK), lambda i: (0, 0)),
                  pl.BlockSpec((1, IDX_DIM), lambda i: (0, 0)),
                  pl.BlockSpec((1, IDX_DIM), lambda i: (0, 0))],
        out_specs=[pl.BlockSpec((tm, KV_RANK), lambda i: (i, 0)),
                   pl.BlockSpec((2, KV_RANK, BLK), lambda i: (i, 0, 0)),
                   pl.BlockSpec((tm, IDX_DIM), lambda i: (i, 0)),
                   pl.BlockSpec((IDX_HEADS, tm), lambda i: (0, i))],
        out_shape=[jax.ShapeDtypeStruct((ROWS, KV_RANK), BF16),
                   jax.ShapeDtypeStruct((ROWS // BLK, KV_RANK, BLK), BF16),
                   jax.ShapeDtypeStruct((ROWS, IDX_DIM), BF16),
                   jax.ShapeDtypeStruct((IDX_HEADS, ROWS), F32)],
        name="kvprep",
    )(c, kvw, ikw, ikb)


def _online_softmax_step(s, m, l):
    m_new = jnp.maximum(m, jnp.max(s, axis=0, keepdims=True))
    alpha = jnp.exp(m - m_new)
    p = jnp.exp(s - m_new)
    return m_new, alpha * l + jnp.sum(p, axis=0, keepdims=True), alpha, p


def _attn_a_kernel(qa_ref, iq_ref, iwt_ref, ckv_ref, ckvt_ref, ik_ref, wuk_ref, wuvt_ref, bias_ref,
                   o_ref, keys_ref, qlat_ref, acc_ref, iqt_ref):
    i = pl.program_id(1)
    nkb = i + 2
    NG = A_HEADS // 2
    GW = 2 * BLK
    t_meta = jnp.where(i == 0, T_META0, T_METAFAR)
    t_prev = jnp.where(i == 0, T_NONE, T_PREV)
    special_blocks = (0, i, i + 1)

    for h in range(A_HEADS):
        qh = qa_ref[:, h * BLK:(h + 1) * BLK]
        ql = lax.dot_general(wuk_ref[h], qh, NT_DIMS, preferred_element_type=F32)
        qlat_ref[:, h * BLK:(h + 1) * BLK] = ql.astype(BF16)

    for pr in range(IDX_HEADS // 2):
        t = iq_ref[:, pr * BLK:(pr + 1) * BLK].astype(F32).T
        iqt_ref[:, (2 * pr) * BLK:(2 * pr + 1) * BLK] = t[:IDX_DIM].astype(BF16)
        iqt_ref[:, (2 * pr + 1) * BLK:(2 * pr + 2) * BLK] = t[IDX_DIM:].astype(BF16)

    iwt = iwt_ref[...]
    row = lax.broadcasted_iota(jnp.int32, (BLK, BLK), 0)
    lane = lax.broadcasted_iota(jnp.int32, (BLK, BLK), 1)

    def idx_keys(ikrows):
        sc = jnp.zeros((ikrows.shape[0], BLK), F32)
        for pr in range(IDX_HEADS // 2):
            s2 = jnp.dot(ikrows, iqt_ref[:, pr * 2 * BLK:(pr + 1) * 2 * BLK], preferred_element_type=F32)
            sc = sc + jnp.maximum(s2[:, :BLK], 0.0) * iwt[2 * pr:2 * pr + 1, :]
            sc = sc + jnp.maximum(s2[:, BLK:], 0.0) * iwt[2 * pr + 1:2 * pr + 2, :]
        bits = lax.bitcast_convert_type(sc, jnp.int32)
        return bits ^ ((bits >> 31) & 0x7FFFFFFF)

    def idx_group(kb, n, carry):
        off = pl.multiple_of(kb * BLK, BLK)
        keys_ref[pl.ds(off, n * BLK), :] = idx_keys(ik_ref[pl.ds(off, n * BLK), :])
        return carry

    def idx_special(carry):
        masks = (None, None, (row >= CHUNK) & (lane < CHUNK))
        for kb, dis in zip((i, i + 1), masks[1:]):
            off = pl.multiple_of(kb * BLK, BLK)
            key = idx_keys(ik_ref[pl.ds(off, BLK), :])
            keys_ref[pl.ds(off, BLK), :] = key if dis is None else jnp.where(dis, INT_MIN, key)
        key = idx_keys(ik_ref[pl.ds(0, BLK), :])
        keys_ref[pl.ds(0, BLK), :] = jnp.where(row >= N_META, INT_MIN, key)
        return carry

    _visit_key_blocks(i, idx_group, idx_special, 0)
    keys_ref[pl.ds(pl.multiple_of(nkb * BLK, BLK), (FAR - 1) * BLK), :] = jnp.full(
        ((FAR - 1) * BLK, BLK), INT_MIN, jnp.int32)
    n_search = lax.shift_right_logical(nkb + FAR - 1, 2)
    crow = lax.broadcasted_iota(jnp.int32, (FAR * BLK, BLK), 0)

    def count(pred_fn):
        def body(c, acc8):
            off = pl.multiple_of(c * FAR * BLK, FAR * BLK)
            k = keys_ref[pl.ds(off, FAR * BLK), :]
            hit = pred_fn(k, off).astype(jnp.int32)
            return acc8 + jnp.sum(hit.reshape(FAR * BLK // 8, 8, BLK), axis=0)
        acc8 = lax.fori_loop(0, n_search, body, jnp.zeros((8, BLK), jnp.int32))
        return jnp.sum(acc8, axis=0, keepdims=True)

    zero = jnp.zeros((1, BLK), jnp.int32)
    c0 = count(lambda k, off: k >= zero)
    prefix = jnp.where(c0 >= TOPK, 0, INT_MIN).astype(jnp.int32)

    def bit_body(t, prefix):
        cand = prefix | jnp.left_shift(jnp.int32(1), 30 - t)
        c = count(lambda k, off: k >= cand)
        return jnp.where(c >= TOPK, cand, prefix)

    thr = lax.fori_loop(0, 31, bit_body, prefix)

    full = thr == INT_MIN
    c_gt = count(lambda k, off: k > thr)
    c_eq = count(lambda k, off: k == thr)
    need = TOPK - c_gt
    tied = jnp.logical_and(jnp.logical_not(full), c_eq > need)
    j_default = jnp.where(full, -1, TP).astype(jnp.int32)

    def tie_search():
        def jbit(t, j):
            cand = j | jnp.left_shift(jnp.int32(1), 12 - t)
            c = count(lambda k, off: (k == thr) & ((off + crow) < cand))
            return jnp.where(c < need, cand, j)
        j = lax.fori_loop(0, 13, jbit, jnp.zeros((1, BLK), jnp.int32))
        return jnp.where(tied, j, j_default)

    any_tied = jnp.max(tied.astype(jnp.int32)) > 0
    jmax = lax.cond(any_tied, tie_search, lambda: j_default)

    acc_ref[...] = jnp.zeros_like(acc_ref)

    def att_update(rows, keys, rowidx, vt, bias, carry):
        sel = (keys > thr) | ((keys == thr) & (rowidx <= jmax))
        selb = jnp.where(sel, 0.0, NEG)
        add = jnp.concatenate([selb, selb], axis=1)
        gs = [slice(g * GW, (g + 1) * GW) for g in range(NG)]
        ss = [jnp.dot(rows, qlat_ref[:, gs[g]], preferred_element_type=F32) * (A_HEAD_DIM ** -0.5) for g in range(NG)]
        ss = [ss[g] + (add if bias is None else add + bias[:, gs[g]]) for g in range(NG)]
        steps = [_online_softmax_step(ss[g], *carry[g]) for g in range(NG)]
        pvs = [jnp.dot(vt, steps[g][3].astype(BF16), preferred_element_type=F32) for g in range(NG)]
        for g in range(NG):
            acc_ref[:, gs[g]] = acc_ref[:, gs[g]] * steps[g][2] + pvs[g]
        return tuple((steps[g][0], steps[g][1]) for g in range(NG))

    def att_group(kb, n, carry):
        off = pl.multiple_of(kb * BLK, BLK)
        rowidx = off + lax.broadcasted_iota(jnp.int32, (n * BLK, BLK), 0)
        vt = jnp.concatenate([ckvt_ref[kb + u] for u in range(n)], axis=1) if n > 1 else ckvt_ref[kb]
        return att_update(ckv_ref[pl.ds(off, n * BLK), :], keys_ref[pl.ds(off, n * BLK), :], rowidx, vt, None, carry)

    def att_special(carry):
        offs = [pl.multiple_of(kb * BLK, BLK) for kb in special_blocks]
        rows = jnp.concatenate([ckv_ref[pl.ds(o, BLK), :] for o in offs], axis=0)
        keys = jnp.concatenate([keys_ref[pl.ds(o, BLK), :] for o in offs], axis=0)
        rowidx = jnp.concatenate([o + row for o in offs], axis=0)
        vt = jnp.concatenate([ckvt_ref[kb] for kb in special_blocks], axis=1)
        bias = jnp.concatenate([bias_ref[t_meta], bias_ref[t_prev], bias_ref[T_DIAG]], axis=0)
        return att_update(rows, keys, rowidx, vt, bias, carry)

    m0 = jnp.full((1, GW), NEG, F32)
    l0 = jnp.zeros((1, GW), F32)
    stats = _visit_key_blocks(i, att_group, att_special, tuple((m0, l0) for _ in range(NG)))
    l = jnp.concatenate([stats[g][1] for g in range(NG)], axis=1)

    olat = (acc_ref[...] / l).astype(BF16)
    for h in range(A_HEADS):
        ot = jnp.dot(wuvt_ref[h], olat[:, h * BLK:(h + 1) * BLK], preferred_element_type=F32)
        o_ref[:, h * BLK:(h + 1) * BLK] = ot.T


def _attn_a(proj_a, iwt, ckv, ckvt, ik, wuk, wuvt, bias_a):
    qrow = lambda b, i: b * NKB + 1 + i
    return pl.pallas_call(
        _attn_a_kernel,
        grid=(BATCH, NQB),
        in_specs=[
            pl.BlockSpec((BLK, A_WIDTH), lambda b, i: (qrow(b, i), 0)),
            pl.BlockSpec((BLK, IDX_HEADS * IDX_DIM), lambda b, i: (qrow(b, i), 1)),
            pl.BlockSpec((IDX_HEADS, BLK), lambda b, i: (0, qrow(b, i))),
            pl.BlockSpec((None, TP, KV_RANK), lambda b, i: (b, 0, 0)),
            pl.BlockSpec((None, NKB, KV_RANK, BLK), lambda b, i: (b, 0, 0, 0)),
            pl.BlockSpec((None, TP, IDX_DIM), lambda b, i: (b, 0, 0)),
            pl.BlockSpec((A_HEADS, KV_RANK, A_HEAD_DIM), lambda b, i: (0, 0, 0)),
            pl.BlockSpec((A_HEADS, A_HEAD_DIM, KV_RANK), lambda b, i: (0, 0, 0)),
            pl.BlockSpec((5, BLK, A_HEADS * BLK), lambda b, i: (0, 0, 0)),
        ],
        out_specs=pl.BlockSpec((BLK, A_WIDTH), lambda b, i: (b * NQB + i, 0)),
        out_shape=jax.ShapeDtypeStruct((BATCH * SEQ, A_WIDTH), F32),
        scratch_shapes=[pltpu.VMEM(((NKB + FAR - 1) * BLK, BLK), jnp.int32),
                        pltpu.VMEM((KV_RANK, A_HEADS * BLK), BF16),
                        pltpu.VMEM((KV_RANK, A_HEADS * BLK), F32),
                        pltpu.VMEM((IDX_DIM, IDX_HEADS * BLK), BF16)],
        compiler_params=pltpu.CompilerParams(
            dimension_semantics=("arbitrary", "arbitrary"), vmem_limit_bytes=VMEM_LIMIT),
        name="attn_a",
    )(proj_a, proj_a, iwt, ckv.reshape(BATCH, TP, KV_RANK), ckvt.reshape(BATCH, NKB, KV_RANK, BLK),
      ik.reshape(BATCH, TP, IDX_DIM), wuk, wuvt, bias_a)


def _attn_b_kernel(lam_ref, q_ref, k_ref, vt_ref, bias_ref, subw_ref, o_ref, acc_ref, *, lam_init):
    i = pl.program_id(2)
    t_meta = jnp.where(i == 0, T_META0, T_METAFAR)
    t_prev = jnp.where(i == 0, T_NONE, T_PREV)
    special_blocks = (0, i, i + 1)
    lp = lam_ref[...]
    lam = (jnp.exp(jnp.sum(lp[0:1] * lp[1:2], axis=-1, keepdims=True))
           - jnp.exp(jnp.sum(lp[2:3] * lp[3:4], axis=-1, keepdims=True)) + lam_init)

    lane = lax.broadcasted_iota(jnp.int32, (BLK, BLK), 1)
    qbd = []
    for hh in range(HPS):
        q = q_ref[:, hh * BLK:(hh + 1) * BLK] * (B_QK_DIM ** -0.5)
        zq = jnp.zeros_like(q)
        qbd.append(jnp.concatenate([jnp.where(lane < B_QK_DIM, q, zq), jnp.where(lane >= B_QK_DIM, q, zq)], axis=0))

    acc_ref[...] = jnp.zeros_like(acc_ref)

    def update_all(rows, vts, biases, carry):
        ss = [lax.dot_general(rows[hh], qbd[hh], NT_DIMS, preferred_element_type=F32) for hh in range(HPS)]
        if biases is not None:
            ss = [ss[hh] + jnp.concatenate([biases[hh], biases[hh]], axis=1) for hh in range(HPS)]
        steps = [_online_softmax_step(ss[hh], *carry[hh]) for hh in range(HPS)]
        pvs = [jnp.dot(vts[hh], steps[hh][3].astype(BF16), preferred_element_type=F32) for hh in range(HPS)]
        for hh in range(HPS):
            acc_ref[hh] = acc_ref[hh] * steps[hh][2] + pvs[hh]
        return tuple((steps[hh][0], steps[hh][1]) for hh in range(HPS))

    def group(kb, n, carry):
        off = pl.multiple_of(kb * BLK, BLK)
        rows = [k_ref[pl.ds(off, n * BLK), hh * BLK:(hh + 1) * BLK] for hh in range(HPS)]
        vts = [jnp.concatenate([vt_ref[hh, kb + u] for u in range(n)], axis=1) if n > 1 else vt_ref[hh, kb]
               for hh in range(HPS)]
        return update_all(rows, vts, None, carry)

    def special(carry):
        offs = [pl.multiple_of(kb * BLK, BLK) for kb in special_blocks]
        rows = [jnp.concatenate([k_ref[pl.ds(o, BLK), hh * BLK:(hh + 1) * BLK] for o in offs], axis=0)
                for hh in range(HPS)]
        vts = [jnp.concatenate([vt_ref[hh, kb] for kb in special_blocks], axis=1) for hh in range(HPS)]
        biases = [jnp.concatenate([bias_ref[t_meta, hh], bias_ref[t_prev, hh], bias_ref[T_DIAG, hh]], axis=0)
                  for hh in range(HPS)]
        return update_all(rows, vts, biases, carry)

    m0 = jnp.full((1, 2 * BLK), NEG, F32)
    l0 = jnp.zeros((1, 2 * BLK), F32)
    stats = _visit_key_blocks(i, group, special, tuple((m0, l0) for _ in range(HPS)))

    for hh in range(HPS):
        a = acc_ref[hh] / stats[hh][1]
        o = a[:, :BLK] - lam * a[:, BLK:]
        ms = jnp.mean(o * o, axis=0, keepdims=True)
        y = o * lax.rsqrt(ms + EPS) * subw_ref[...] * (1.0 - lam_init)
        o_ref[:, hh * BLK:(hh + 1) * BLK] = y.T


def _attn_b(proj_a, vt, bias_b, lam_p, subw, lam_init):
    qrow = lambda b, g, i: b * NKB + 1 + i
    wide = HPS * BLK
    qcol0 = (A_WIDTH + IDX_HEADS * IDX_DIM) // wide
    kcol0 = qcol0 + 2 * B_HEADS * B_QK_DIM // wide
    return pl.pallas_call(
        functools.partial(_attn_b_kernel, lam_init=lam_init),
        grid=(BATCH, B_HEADS // HPS, NQB),
        in_specs=[
            pl.BlockSpec((4, B_QK_DIM), lambda b, g, i: (0, 0)),
            pl.BlockSpec((BLK, wide), lambda b, g, i: (qrow(b, g, i), qcol0 + g)),
            pl.BlockSpec((None, TP, wide), lambda b, g, i: (b, 0, kcol0 + g)),
            pl.BlockSpec((None, HPS, NKB, B_V_DIM, BLK), lambda b, g, i: (b, g, 0, 0, 0)),
            pl.BlockSpec((5, HPS, BLK, BLK), lambda b, g, i: (0, g, 0, 0)),
            pl.BlockSpec((B_V_DIM, BLK), lambda b, g, i: (0, 0)),
        ],
        out_specs=pl.BlockSpec((BLK, wide), lambda b, g, i: (b * NQB + i, g)),
        out_shape=jax.ShapeDtypeStruct((BATCH * SEQ, B_WIDTH), F32),
        scratch_shapes=[pltpu.VMEM((HPS, B_V_DIM, 2 * BLK), F32)],
        compiler_params=pltpu.CompilerParams(
            dimension_semantics=("arbitrary", "arbitrary", "arbitrary"), vmem_limit_bytes=VMEM_LIMIT),
        name="attn_b",
    )(lam_p, proj_a, proj_a.reshape(BATCH, TP, -1), vt, bias_b, subw)


def _out_kernel(oa_ref, za_ref, ob_ref, zb_ref, ga_ref, gb_ref, x_ref, woa_ref, wob_ref, wout_ref, pw_ref, o_ref):
    a = (oa_ref[...] * jax.nn.silu(za_ref[...])).astype(BF16)
    ya = jnp.dot(a, woa_ref[...], preferred_element_type=F32)
    b = (ob_ref[...] * jax.nn.silu(zb_ref[...])).astype(BF16)
    yb = jnp.dot(b, wob_ref[...], preferred_element_type=F32)
    mix = jax.nn.sigmoid(ga_ref[...]) * ya + jax.nn.sigmoid(gb_ref[...]) * yb
    out = jnp.dot(mix.astype(BF16), wout_ref[...], preferred_element_type=F32)
    ms = jnp.mean(out * out, axis=-1, keepdims=True)
    o_ref[...] = x_ref[...] + out * lax.rsqrt(ms + EPS) * pw_ref[...]


def _out_stage(o_a, o_b, proj_b, x2, woa, wob, wout, pw):
    tm = BLK
    prow = lambda g: g + g // NQB + 1
    const = lambda g: (0, 0)
    return pl.pallas_call(
        _out_kernel,
        grid=(BATCH * NQB,),
        in_specs=[
            pl.BlockSpec((tm, A_WIDTH), lambda g: (g, 0)),
            pl.BlockSpec((tm, A_WIDTH), lambda g: (prow(g), 0)),
            pl.BlockSpec((tm, B_WIDTH), lambda g: (g, 0)),
            pl.BlockSpec((tm, B_WIDTH), lambda g: (prow(g), 1)),
            pl.BlockSpec((tm, D_MODEL), lambda g: (prow(g), 1)),
            pl.BlockSpec((tm, D_MODEL), lambda g: (prow(g), 2)),
            pl.BlockSpec((tm, D_MODEL), lambda g: (g, 0)),
            pl.BlockSpec((A_WIDTH, D_MODEL), const, pipeline_mode=pl.Buffered(1)),
            pl.BlockSpec((B_WIDTH, D_MODEL), const, pipeline_mode=pl.Buffered(1)),
            pl.BlockSpec((D_MODEL, D_MODEL), const, pipeline_mode=pl.Buffered(1)),
            pl.BlockSpec((1, D_MODEL), const),
        ],
        out_specs=pl.BlockSpec((tm, D_MODEL), lambda g: (g, 0)),
        out_shape=jax.ShapeDtypeStruct((BATCH * SEQ, D_MODEL), F32),
        compiler_params=pltpu.CompilerParams(
            dimension_semantics=("arbitrary",), vmem_limit_bytes=VMEM_LIMIT),
        name="out_stage",
    )(o_a, proj_b, o_b, proj_b, proj_b, proj_b, x2, woa, wob, wout, pw)


def kernel(x, meta_tokens, rel_bias, pre_norm_w, w_in, kv_norm_w, w_uk, w_uv, idx_k_norm_w, idx_k_norm_b,
           diff_lambda, diff_subln_w, w_o_a, w_o_b, w_out, post_norm_w):
    assert x.shape == (BATCH, SEQ, D_MODEL) and w_in.shape[0] == 1
    layer = 0
    lam_init = 0.8 - 0.6 * math.exp(-0.3 * layer)

    head = jnp.concatenate([meta_tokens.astype(F32), jnp.zeros((BLK - N_META, D_MODEL), F32)], axis=0)
    hp = jnp.concatenate([jnp.broadcast_to(head[None], (BATCH, BLK, D_MODEL)), x], axis=1).reshape(ROWS, D_MODEL)

    offs = np.concatenate([[0], np.cumsum(IN_SIZES)])
    w = w_in[0]
    seg = lambda k: w[:, offs[k]:offs[k + 1]]
    zcols = lambda n: jnp.zeros((D_MODEL, n), w.dtype)
    w_a = jnp.concatenate([seg(0), seg(3), seg(6), seg(7), seg(8)], axis=1).astype(BF16)
    w_b = jnp.concatenate([seg(2), seg(9), seg(10), seg(11)], axis=1).astype(BF16)
    w_c = jnp.concatenate([seg(1), seg(4), zcols(BLK - IDX_DIM), seg(5), zcols(BLK - IDX_HEADS)],
                          axis=1).astype(BF16)

    u = _prenorm(hp, pre_norm_w[0][None].astype(F32))
    proj_a = _matmul(u, w_a, BF16, 768, 512, "proj_a")
    proj_b = _matmul(u, w_b, F32, 768, 512, "proj_b")
    proj_c = _matmul(u, w_c, F32, 768, 512, "proj_c")

    ckv, ckvt, ik, iwt = _kvprep(proj_c, kv_norm_w[0][None].astype(F32),
                                 idx_k_norm_w[0][None].astype(F32), idx_k_norm_b[0][None].astype(F32))

    bias = _bias_tiles(rel_bias)
    bias_a = jnp.transpose(bias[:, :A_HEADS], (0, 2, 1, 3)).reshape(5, BLK, A_HEADS * BLK)
    bias_b = bias[:, A_HEADS:]

    wuk = jnp.transpose(w_uk[0], (1, 0, 2)).astype(BF16)
    wuvt = jnp.transpose(w_uv[0], (1, 2, 0)).astype(BF16)
    o_a = _attn_a(proj_a, iwt, ckv, ckvt, ik, wuk, wuvt, bias_a)

    vcol = A_WIDTH + IDX_HEADS * IDX_DIM + 4 * B_HEADS * B_QK_DIM
    vt = jnp.transpose(proj_a[:, vcol:vcol + B_WIDTH].reshape(BATCH, NKB, BLK, B_HEADS, B_V_DIM), (0, 3, 1, 4, 2))
    subw = jnp.broadcast_to(diff_subln_w[0].astype(F32)[:, None], (B_V_DIM, BLK))
    o_b = _attn_b(proj_a, vt, bias_b, diff_lambda[0].astype(F32), subw, lam_init)

    out = _out_stage(o_a, o_b, proj_b, x.reshape(BATCH * SEQ, D_MODEL),
                     w_o_a[0].astype(BF16), w_o_b[0].astype(BF16), w_out[0].astype(BF16),
                     post_norm_w[0][None].astype(F32))
    return out.reshape(BATCH, SEQ, D_MODEL)
```

```python
import functools
import math

import numpy as np
import jax
import jax.numpy as jnp
from jax import lax
from jax.experimental import pallas as pl
from jax.experimental.pallas import tpu as pltpu

D_MODEL = 2048
BATCH = 2
SEQ = 4096
CHUNK = 64
N_META = 16
N_BUCKETS = 32
MAX_DISTANCE = 128
A_HEADS = 8
A_HEAD_DIM = 128
KV_RANK = 256
IDX_HEADS = 16
IDX_DIM = 64
TOPK = 256
B_HEADS = 8
B_QK_DIM = 64
B_V_DIM = 128
A_WIDTH = A_HEADS * A_HEAD_DIM
B_WIDTH = B_HEADS * B_V_DIM
IN_SIZES = (A_WIDTH, KV_RANK, A_WIDTH, IDX_HEADS * IDX_DIM, IDX_DIM, IDX_HEADS,
            2 * B_HEADS * B_QK_DIM, 2 * B_HEADS * B_QK_DIM, B_WIDTH, B_WIDTH,
            D_MODEL, D_MODEL)
EPS = 1e-6

BLK = 128
NQB = SEQ // BLK
NKB = NQB + 1
TP = NKB * BLK
ROWS = BATCH * TP
FAR = 4
HPS = 8
NEG = -1e30
INT_MIN = -2 ** 31
HALF = 2 ** 15
LOG2E = math.log2(math.e)
VMEM_LIMIT = 56 * 1024 * 1024

F32 = jnp.float32
BF16 = jnp.bfloat16
NT_DIMS = (((1,), (1,)), ((), ()))


def _t5_bucket_np(rel):
    nb = N_BUCKETS // 2
    max_exact = nb // 2
    ret = np.where(rel > 0, nb, 0)
    n = np.abs(rel)
    nf = np.maximum(n, 1).astype(np.float32)
    large = max_exact + (np.log(nf / np.float32(max_exact))
                         / np.float32(math.log(MAX_DISTANCE / max_exact))
                         * np.float32(nb - max_exact)).astype(np.int32)
    large = np.minimum(large, nb - 1)
    return ret + np.where(n < max_exact, n, large)


T_DIAG, T_PREV, T_META0, T_METAFAR, T_NONE = range(5)


def _bias_tiles(rel_bias):
    a = np.arange(BLK)[:, None]
    b = np.arange(BLK)[None, :]
    far = np.full((BLK, BLK), -4 * BLK)
    nowhere = np.zeros((BLK, BLK), bool)
    pad_rows = (a >= N_META) | nowhere
    rels = np.stack([a - b, a - b - BLK, a - N_META - b, far, far])
    dis = np.stack([(a >= CHUNK) & (b < CHUNK), nowhere, pad_rows, pad_rows, ~nowhere])
    idx = _t5_bucket_np(rels)
    far_bucket = N_BUCKETS // 2 - 1
    assert _t5_bucket_np(np.array([-BLK - 1]))[0] == far_bucket == idx[T_METAFAR, 0, 0]
    rb = rel_bias.astype(F32)
    tiles = jnp.zeros((5, A_HEADS + B_HEADS, BLK, BLK), F32)
    for k in range(N_BUCKETS):
        tiles = jnp.where((idx == k)[:, None], rb[k][None, :, None, None], tiles)
    tiles = tiles - rb[far_bucket][None, :, None, None]
    return jnp.where(dis[:, None], NEG, tiles)


def _far_split(i):
    n_far = jnp.maximum(i - 1, 0)
    return n_far, lax.shift_right_logical(n_far, 2)


def _visit_key_blocks(i, group_fn, special_fn, carry):
    n_far, n_chunks = _far_split(i)
    carry = lax.fori_loop(0, n_chunks, lambda c, cr: group_fn(1 + FAR * c, FAR, cr), carry)
    carry = lax.fori_loop(1 + FAR * n_chunks, 1 + n_far, lambda kb, cr: group_fn(kb, 1, cr), carry)
    return special_fn(carry)


def _prenorm_kernel(x_ref, meta_ref, w_ref, o_ref):
    def norm(x):
        ms = jnp.mean(x * x, axis=-1, keepdims=True)
        return (x * lax.rsqrt(ms + EPS) * w_ref[...]).astype(o_ref.dtype)

    is_meta = lax.rem(pl.program_id(0), NKB) == 0

    @pl.when(is_meta)
    def _():
        o_ref[...] = norm(meta_ref[...])

    @pl.when(jnp.logical_not(is_meta))
    def _():
        o_ref[...] = norm(x_ref[...])


def _prenorm(x2, meta_block, w):
    frame_block = lambda r: (jnp.maximum(r - r // NKB - 1, 0), 0)
    return pl.pallas_call(
        _prenorm_kernel,
        grid=(ROWS // BLK,),
        in_specs=[pl.BlockSpec((BLK, D_MODEL), frame_block),
                  pl.BlockSpec((BLK, D_MODEL), lambda r: (0, 0)),
                  pl.BlockSpec((1, D_MODEL), lambda r: (0, 0))],
        out_specs=pl.BlockSpec((BLK, D_MODEL), lambda r: (r, 0)),
        out_shape=jax.ShapeDtypeStruct((ROWS, D_MODEL), BF16),
        compiler_params=pltpu.CompilerParams(dimension_semantics=("arbitrary",)),
        name="prenorm",
    )(x2, meta_block, w)


def _mm_kernel(a_ref, w_ref, o_ref):
    o_ref[...] = jnp.dot(a_ref[...], w_ref[...], preferred_element_type=F32).astype(o_ref.dtype)


def _mm_scaled_kernel(a_ref, w_ref, cs_ref, o_ref):
    acc = jnp.dot(a_ref[...], w_ref[...], preferred_element_type=F32)
    o_ref[...] = (acc * cs_ref[...]).astype(o_ref.dtype)


def _matmul(a, w, col0, n, out_dtype, tm, tn, name, col_scale=None):
    m, k = a.shape
    assert col0 % tn == 0 and n % tn == 0 and m % tm == 0
    c0 = col0 // tn
    in_specs = [pl.BlockSpec((tm, k), lambda i, j: (i, 0)),
                pl.BlockSpec((k, tn), lambda i, j: (0, c0 + j))]
    args = (a, w)
    if col_scale is not None:
        in_specs.append(pl.BlockSpec((1, tn), lambda i, j: (0, j)))
        args += (col_scale,)
    return pl.pallas_call(
        _mm_kernel if col_scale is None else _mm_scaled_kernel,
        grid=(m // tm, n // tn),
        in_specs=in_specs,
        out_specs=pl.BlockSpec((tm, tn), lambda i, j: (i, j)),
        out_shape=jax.ShapeDtypeStruct((m, n), out_dtype),
        compiler_params=pltpu.CompilerParams(
            dimension_semantics=("arbitrary", "arbitrary"), vmem_limit_bytes=VMEM_LIMIT),
        name=name,
    )(*args)


def _kvprep_kernel(c_ref, v_ref, kvw_ref, ikw_ref, ikb_ref, ckv_ref, ckvt_ref, ik_ref, iwt_ref, vt_ref):
    for blk in range(2):
        for h in range(B_HEADS):
            vh = v_ref[blk * BLK:(blk + 1) * BLK, h * B_V_DIM:(h + 1) * B_V_DIM]
            vt_ref[blk, h] = vh.astype(F32).T.astype(BF16)
    c = c_ref[...]
    ckv = c[:, :KV_RANK]
    ms = jnp.mean(ckv * ckv, axis=-1, keepdims=True)
    ckvn = ckv * lax.rsqrt(ms + EPS) * kvw_ref[...]
    ckv_ref[...] = ckvn.astype(BF16)
    ckvt_ref[0] = ckvn[:BLK].T.astype(BF16)
    ckvt_ref[1] = ckvn[BLK:].T.astype(BF16)
    ik = c[:, KV_RANK:KV_RANK + IDX_DIM]
    mu = jnp.mean(ik, axis=-1, keepdims=True)
    var = jnp.mean(jnp.square(ik - mu), axis=-1, keepdims=True)
    ikn = (ik - mu) * lax.rsqrt(var + EPS) * ikw_ref[...] + ikb_ref[...]
    ik_ref[...] = ikn.astype(BF16)
    iw = c[:, 3 * BLK:4 * BLK] * (IDX_HEADS ** -0.5 * IDX_DIM ** -0.5)
    iwt_ref[...] = iw.T[:IDX_HEADS, :]


def _kvprep(c, proj_a, vcol, kvw, ikw, ikb):
    tm = 2 * BLK
    assert vcol % B_WIDTH == 0
    return pl.pallas_call(
        _kvprep_kernel,
        grid=(ROWS // tm,),
        in_specs=[pl.BlockSpec((tm, 4 * BLK), lambda i: (i, 0)),
                  pl.BlockSpec((tm, B_WIDTH), lambda i: (i, vcol // B_WIDTH)),
                  pl.BlockSpec((1, KV_RANK), lambda i: (0, 0)),
                  pl.BlockSpec((1, IDX_DIM), lambda i: (0, 0)),
                  pl.BlockSpec((1, IDX_DIM), lambda i: (0, 0))],
        out_specs=[pl.BlockSpec((tm, KV_RANK), lambda i: (i, 0)),
                   pl.BlockSpec((2, KV_RANK, BLK), lambda i: (i, 0, 0)),
                   pl.BlockSpec((tm, IDX_DIM), lambda i: (i, 0)),
                   pl.BlockSpec((IDX_HEADS, tm), lambda i: (0, i)),
                   pl.BlockSpec((2, B_HEADS, B_V_DIM, BLK), lambda i: (i, 0, 0, 0))],
        out_shape=[jax.ShapeDtypeStruct((ROWS, KV_RANK), BF16),
                   jax.ShapeDtypeStruct((ROWS // BLK, KV_RANK, BLK), BF16),
                   jax.ShapeDtypeStruct((ROWS, IDX_DIM), BF16),
                   jax.ShapeDtypeStruct((IDX_HEADS, ROWS), F32),
                   jax.ShapeDtypeStruct((ROWS // BLK, B_HEADS, B_V_DIM, BLK), BF16)],
        name="kvprep",
    )(c, proj_a, kvw, ikw, ikb)


def _online_softmax_step(s, m, l):
    m_new = jnp.maximum(m, jnp.max(s, axis=0, keepdims=True))
    alpha = jnp.exp2(m - m_new)
    p = jnp.exp2(s - m_new)
    return m_new, alpha * l + jnp.sum(p, axis=0, keepdims=True), alpha, p


def _attn_a_kernel(qa_ref, iq_ref, iwt_ref, ckv_ref, ckvt_ref, ik_ref, wuk_ref, wuvt_ref, bias_ref,
                   o_ref, keys_ref, qlat_ref, acc_ref, iqt_ref, khi_ref, klo_ref, lom_ref):
    i = pl.program_id(1)
    nkb = i + 2
    NG = A_HEADS // 2
    GW = 2 * BLK
    t_meta = jnp.where(i == 0, T_META0, T_METAFAR)
    t_prev = jnp.where(i == 0, T_NONE, T_PREV)
    special_blocks = (0, i, i + 1)

    for h in range(A_HEADS):
        qh = qa_ref[:, h * BLK:(h + 1) * BLK]
        ql = lax.dot_general(wuk_ref[h], qh, NT_DIMS, preferred_element_type=F32)
        qlat_ref[:, h * BLK:(h + 1) * BLK] = (ql * (A_HEAD_DIM ** -0.5 * LOG2E)).astype(BF16)

    for pr in range(IDX_HEADS // 2):
        t = iq_ref[:, pr * BLK:(pr + 1) * BLK].astype(F32).T
        iqt_ref[:, (2 * pr) * BLK:(2 * pr + 1) * BLK] = t[:IDX_DIM].astype(BF16)
        iqt_ref[:, (2 * pr + 1) * BLK:(2 * pr + 2) * BLK] = t[IDX_DIM:].astype(BF16)

    iwt = iwt_ref[...]
    row = lax.broadcasted_iota(jnp.int32, (BLK, BLK), 0)
    lane = lax.broadcasted_iota(jnp.int32, (BLK, BLK), 1)

    def idx_keys(ikrows):
        sc = jnp.zeros((ikrows.shape[0], BLK), F32)
        for pr in range(IDX_HEADS // 2):
            s2 = jnp.dot(ikrows, iqt_ref[:, pr * 2 * BLK:(pr + 1) * 2 * BLK], preferred_element_type=F32)
            sc = sc + jnp.maximum(s2[:, :BLK], 0.0) * iwt[2 * pr:2 * pr + 1, :]
            sc = sc + jnp.maximum(s2[:, BLK:], 0.0) * iwt[2 * pr + 1:2 * pr + 2, :]
        bits = lax.bitcast_convert_type(sc, jnp.int32)
        return bits ^ ((bits >> 31) & 0x7FFFFFFF)

    def put_keys(off, n, key):
        keys_ref[pl.ds(off, n * BLK), :] = key
        khi_ref[pl.ds(off, n * BLK), :] = (key >> 16).astype(jnp.int16)
        klo_ref[pl.ds(off, n * BLK), :] = ((key & 0xFFFF) - HALF).astype(jnp.int16)

    def idx_group(kb, n, carry):
        off = pl.multiple_of(kb * BLK, BLK)
        put_keys(off, n, idx_keys(ik_ref[pl.ds(off, n * BLK), :]))
        return carry

    def idx_special(carry):
        offs = [pl.multiple_of(kb * BLK, BLK) for kb in special_blocks]
        key = idx_keys(jnp.concatenate([ik_ref[pl.ds(o, BLK), :] for o in offs], axis=0))
        put_keys(offs[1], 1, key[BLK:2 * BLK])
        put_keys(offs[2], 1, jnp.where((row >= CHUNK) & (lane < CHUNK), INT_MIN, key[2 * BLK:]))
        put_keys(offs[0], 1, jnp.where(row >= N_META, INT_MIN, key[:BLK]))
        return carry

    _visit_key_blocks(i, idx_group, idx_special, 0)
    put_keys(pl.multiple_of(nkb * BLK, BLK), FAR - 1, jnp.full(((FAR - 1) * BLK, BLK), INT_MIN, jnp.int32))
    n_search = lax.shift_right_logical(nkb + FAR - 1, 2)
    crow = lax.broadcasted_iota(jnp.int32, (FAR * BLK, BLK), 0)
    chunk_off = lambda c: pl.multiple_of(c * FAR * BLK, FAR * BLK)

    def count(pred_fn):
        def body(c, acc8):
            k = keys_ref[pl.ds(chunk_off(c), FAR * BLK), :]
            hit = pred_fn(k, chunk_off(c)).astype(jnp.int32)
            return acc8 + jnp.sum(hit.reshape(FAR * BLK // 8, 8, BLK), axis=0)
        acc8 = lax.fori_loop(0, n_search, body, jnp.zeros((8, BLK), jnp.int32))
        return jnp.sum(acc8, axis=0, keepdims=True)

    PACK = 16
    one16 = jnp.ones((PACK, BLK), jnp.int16)
    zero16 = jnp.zeros((PACK, BLK), jnp.int16)

    def count16(tab_ref, cand, strict=False):
        c16 = jnp.broadcast_to(cand, (PACK, BLK)).astype(jnp.int16)

        def body(c, acc):
            t = tab_ref[pl.ds(chunk_off(c), FAR * BLK), :]
            hits = []
            for r in range(FAR * BLK // PACK):
                tr = t[r * PACK:(r + 1) * PACK]
                hits.append(jnp.where((tr > c16) if strict else (tr >= c16), one16, zero16))
            while len(hits) > 1:
                hits = [hits[a] + hits[a + 1] for a in range(0, len(hits), 2)]
            return acc + hits[0]
        acc = lax.fori_loop(0, n_search, body, zero16)
        return jnp.sum(acc.astype(jnp.int32), axis=0, keepdims=True)

    def kth_largest16(tab_ref, k):
        c0 = count16(tab_ref, jnp.zeros((1, BLK), jnp.int32))
        prefix = jnp.where(c0 >= k, 0, -HALF).astype(jnp.int32)

        def bit_body(t, prefix):
            cand = prefix | jnp.left_shift(jnp.int32(1), 14 - t)
            return jnp.where(count16(tab_ref, cand) >= k, cand, prefix)
        return lax.fori_loop(0, 15, bit_body, prefix)

    hi_k = kth_largest16(khi_ref, TOPK)
    full = hi_k == -HALF
    need_lo = TOPK - count16(khi_ref, hi_k, strict=True)
    hi_k16 = jnp.broadcast_to(hi_k, (PACK, BLK)).astype(jnp.int16)
    none16 = jnp.full((PACK, BLK), -HALF, jnp.int16)

    def lom_body(c, carry):
        hi = khi_ref[pl.ds(chunk_off(c), FAR * BLK), :]
        lo = klo_ref[pl.ds(chunk_off(c), FAR * BLK), :]
        sls = [slice(r * PACK, (r + 1) * PACK) for r in range(FAR * BLK // PACK)]
        lom_ref[pl.ds(chunk_off(c), FAR * BLK), :] = jnp.concatenate(
            [jnp.where(hi[sl] == hi_k16, lo[sl], none16) for sl in sls], axis=0)
        return carry

    lax.fori_loop(0, n_search, lom_body, 0)
    lo_k = kth_largest16(lom_ref, need_lo)
    thr = jnp.where(full, INT_MIN, (hi_k << 16) | (lo_k + HALF)).astype(jnp.int32)

    c_gt = count(lambda k, off: k > thr)
    c_eq = count(lambda k, off: k == thr)
    need = TOPK - c_gt
    tied = jnp.logical_and(jnp.logical_not(full), c_eq > need)
    j_default = jnp.where(full, -1, TP).astype(jnp.int32)

    def tie_search():
        def jbit(t, j):
            cand = j | jnp.left_shift(jnp.int32(1), 12 - t)
            c = count(lambda k, off: (k == thr) & ((off + crow) < cand))
            return jnp.where(c < need, cand, j)
        j = lax.fori_loop(0, 13, jbit, jnp.zeros((1, BLK), jnp.int32))
        return jnp.where(tied, j, j_default)

    any_tied = jnp.max(tied.astype(jnp.int32)) > 0
    jmax = lax.cond(any_tied, tie_search, lambda: j_default)

    acc_ref[...] = jnp.zeros_like(acc_ref)

    def att_update(rows, keys, rowidx, vt, bias, carry):
        sel = (keys > thr) | ((keys == thr) & (rowidx <= jmax))
        selb = jnp.where(sel, 0.0, NEG)
        add = jnp.concatenate([selb, selb], axis=1)
        gs = [slice(g * GW, (g + 1) * GW) for g in range(NG)]
        ss = [jnp.dot(rows, qlat_ref[:, gs[g]], preferred_element_type=F32) for g in range(NG)]
        ss = [ss[g] + (add if bias is None else add + bias[:, gs[g]]) for g in range(NG)]
        steps = [_online_softmax_step(ss[g], *carry[g]) for g in range(NG)]
        pvs = [jnp.dot(vt, steps[g][3].astype(BF16), preferred_element_type=F32) for g in range(NG)]
        for g in range(NG):
            acc_ref[:, gs[g]] = acc_ref[:, gs[g]] * steps[g][2] + pvs[g]
        return tuple((steps[g][0], steps[g][1]) for g in range(NG))

    def att_group(kb, n, carry):
        off = pl.multiple_of(kb * BLK, BLK)
        rowidx = off + lax.broadcasted_iota(jnp.int32, (n * BLK, BLK), 0)
        vt = jnp.concatenate([ckvt_ref[kb + u] for u in range(n)], axis=1) if n > 1 else ckvt_ref[kb]
        return att_update(ckv_ref[pl.ds(off, n * BLK), :], keys_ref[pl.ds(off, n * BLK), :], rowidx, vt, None, carry)

    def att_special(carry):
        offs = [pl.multiple_of(kb * BLK, BLK) for kb in special_blocks]
        rows = jnp.concatenate([ckv_ref[pl.ds(o, BLK), :] for o in offs], axis=0)
        keys = jnp.concatenate([keys_ref[pl.ds(o, BLK), :] for o in offs], axis=0)
        rowidx = jnp.concatenate([o + row for o in offs], axis=0)
        vt = jnp.concatenate([ckvt_ref[kb] for kb in special_blocks], axis=1)
        bias = jnp.concatenate([bias_ref[t_meta], bias_ref[t_prev], bias_ref[T_DIAG]], axis=0)
        return att_update(rows, keys, rowidx, vt, bias, carry)

    m0 = jnp.full((1, GW), NEG, F32)
    l0 = jnp.zeros((1, GW), F32)
    stats = _visit_key_blocks(i, att_group, att_special, tuple((m0, l0) for _ in range(NG)))
    l = jnp.concatenate([stats[g][1] for g in range(NG)], axis=1)

    olat = (acc_ref[...] / l).astype(BF16)
    for h in range(A_HEADS):
        ot = jnp.dot(wuvt_ref[h], olat[:, h * BLK:(h + 1) * BLK], preferred_element_type=F32)
        o_ref[:, h * BLK:(h + 1) * BLK] = ot.T


def _attn_a(proj_a, iwt, ckv, ckvt, ik, wuk, wuvt, bias_a):
    qrow = lambda b, i: b * NKB + 1 + i
    return pl.pallas_call(
        _attn_a_kernel,
        grid=(BATCH, NQB),
        in_specs=[
            pl.BlockSpec((BLK, A_WIDTH), lambda b, i: (qrow(b, i), 0)),
            pl.BlockSpec((BLK, IDX_HEADS * IDX_DIM), lambda b, i: (qrow(b, i), 1)),
            pl.BlockSpec((IDX_HEADS, BLK), lambda b, i: (0, qrow(b, i))),
            pl.BlockSpec((None, TP, KV_RANK), lambda b, i: (b, 0, 0)),
            pl.BlockSpec((None, NKB, KV_RANK, BLK), lambda b, i: (b, 0, 0, 0)),
            pl.BlockSpec((None, TP, IDX_DIM), lambda b, i: (b, 0, 0)),
            pl.BlockSpec((A_HEADS, KV_RANK, A_HEAD_DIM), lambda b, i: (0, 0, 0)),
            pl.BlockSpec((A_HEADS, A_HEAD_DIM, KV_RANK), lambda b, i: (0, 0, 0)),
            pl.BlockSpec((5, BLK, A_HEADS * BLK), lambda b, i: (0, 0, 0)),
        ],
        out_specs=pl.BlockSpec((BLK, A_WIDTH), lambda b, i: (b * NQB + i, 0)),
        out_shape=jax.ShapeDtypeStruct((BATCH * SEQ, A_WIDTH), F32),
        scratch_shapes=[pltpu.VMEM(((NKB + FAR - 1) * BLK, BLK), jnp.int32),
                        pltpu.VMEM((KV_RANK, A_HEADS * BLK), BF16),
                        pltpu.VMEM((KV_RANK, A_HEADS * BLK), F32),
                        pltpu.VMEM((IDX_DIM, IDX_HEADS * BLK), BF16)]
                       + [pltpu.VMEM(((NKB + FAR - 1) * BLK, BLK), jnp.int16)] * 3,
        compiler_params=pltpu.CompilerParams(
            dimension_semantics=("arbitrary", "arbitrary"), vmem_limit_bytes=VMEM_LIMIT),
        name="attn_a",
    )(proj_a, proj_a, iwt, ckv.reshape(BATCH, TP, KV_RANK), ckvt.reshape(BATCH, NKB, KV_RANK, BLK),
      ik.reshape(BATCH, TP, IDX_DIM), wuk, wuvt, bias_a)


def _attn_b_kernel(lam_ref, q_ref, k_ref, vt_ref, bias_ref, subw_ref, o_ref, acc_ref, *, lam_init):
    i = pl.program_id(2)
    t_meta = jnp.where(i == 0, T_META0, T_METAFAR)
    t_prev = jnp.where(i == 0, T_NONE, T_PREV)
    special_blocks = (0, i, i + 1)
    lp = lam_ref[...]
    lam = (jnp.exp(jnp.sum(lp[0:1] * lp[1:2], axis=-1, keepdims=True))
           - jnp.exp(jnp.sum(lp[2:3] * lp[3:4], axis=-1, keepdims=True)) + lam_init)

    lane = lax.broadcasted_iota(jnp.int32, (BLK, BLK), 1)
    qbd = []
    for hh in range(HPS):
        q = q_ref[:, hh * BLK:(hh + 1) * BLK]
        zq = jnp.zeros_like(q)
        qbd.append(jnp.concatenate([jnp.where(lane < B_QK_DIM, q, zq), jnp.where(lane >= B_QK_DIM, q, zq)], axis=0))

    acc_ref[...] = jnp.zeros_like(acc_ref)

    def update_all(rows, vts, biases, carry):
        ss = [lax.dot_general(rows[hh], qbd[hh], NT_DIMS, preferred_element_type=F32) for hh in range(HPS)]
        if biases is not None:
            ss = [ss[hh] + jnp.concatenate([biases[hh], biases[hh]], axis=1) for hh in range(HPS)]
        steps = [_online_softmax_step(ss[hh], *carry[hh]) for hh in range(HPS)]
        pvs = [jnp.dot(vts[hh], steps[hh][3].astype(BF16), preferred_element_type=F32) for hh in range(HPS)]
        for hh in range(HPS):
            acc_ref[hh] = acc_ref[hh] * steps[hh][2] + pvs[hh]
        return tuple((steps[hh][0], steps[hh][1]) for hh in range(HPS))

    def group(kb, n, carry):
        off = pl.multiple_of(kb * BLK, BLK)
        rows = [k_ref[pl.ds(off, n * BLK), hh * BLK:(hh + 1) * BLK] for hh in range(HPS)]
        vts = [jnp.concatenate([vt_ref[kb + u, hh] for u in range(n)], axis=1) if n > 1 else vt_ref[kb, hh]
               for hh in range(HPS)]
        return update_all(rows, vts, None, carry)

    def special(carry):
        offs = [pl.multiple_of(kb * BLK, BLK) for kb in special_blocks]
        rows = [jnp.concatenate([k_ref[pl.ds(o, BLK), hh * BLK:(hh + 1) * BLK] for o in offs], axis=0)
                for hh in range(HPS)]
        vts = [jnp.concatenate([vt_ref[kb, hh] for kb in special_blocks], axis=1) for hh in range(HPS)]
        biases = [jnp.concatenate([bias_ref[t_meta, hh], bias_ref[t_prev, hh], bias_ref[T_DIAG, hh]], axis=0)
                  for hh in range(HPS)]
        return update_all(rows, vts, biases, carry)

    m0 = jnp.full((1, 2 * BLK), NEG, F32)
    l0 = jnp.zeros((1, 2 * BLK), F32)
    stats = _visit_key_blocks(i, group, special, tuple((m0, l0) for _ in range(HPS)))

    for hh in range(HPS):
        a = acc_ref[hh] / stats[hh][1]
        o = a[:, :BLK] - lam * a[:, BLK:]
        ms = jnp.mean(o * o, axis=0, keepdims=True)
        y = o * lax.rsqrt(ms + EPS) * subw_ref[...] * (1.0 - lam_init)
        o_ref[:, hh * BLK:(hh + 1) * BLK] = y.T


def _attn_b(proj_a, vt, bias_b, lam_p, subw, lam_init):
    qrow = lambda b, g, i: b * NKB + 1 + i
    wide = HPS * BLK
    qcol0 = (A_WIDTH + IDX_HEADS * IDX_DIM) // wide
    kcol0 = qcol0 + 2 * B_HEADS * B_QK_DIM // wide
    return pl.pallas_call(
        functools.partial(_attn_b_kernel, lam_init=lam_init),
        grid=(BATCH, B_HEADS // HPS, NQB),
        in_specs=[
            pl.BlockSpec((4, B_QK_DIM), lambda b, g, i: (0, 0)),
            pl.BlockSpec((BLK, wide), lambda b, g, i: (qrow(b, g, i), qcol0 + g)),
            pl.BlockSpec((None, TP, wide), lambda b, g, i: (b, 0, kcol0 + g)),
            pl.BlockSpec((None, NKB, HPS, B_V_DIM, BLK), lambda b, g, i: (b, 0, g, 0, 0)),
            pl.BlockSpec((5, HPS, BLK, BLK), lambda b, g, i: (0, g, 0, 0)),
            pl.BlockSpec((B_V_DIM, BLK), lambda b, g, i: (0, 0)),
        ],
        out_specs=pl.BlockSpec((BLK, wide), lambda b, g, i: (b * NQB + i, g)),
        out_shape=jax.ShapeDtypeStruct((BATCH * SEQ, B_WIDTH), F32),
        scratch_shapes=[pltpu.VMEM((HPS, B_V_DIM, 2 * BLK), F32)],
        compiler_params=pltpu.CompilerParams(
            dimension_semantics=("arbitrary", "arbitrary", "arbitrary"), vmem_limit_bytes=VMEM_LIMIT),
        name="attn_b",
    )(lam_p, proj_a, proj_a.reshape(BATCH, TP, -1), vt, bias_b, subw)


def _out_kernel(oa_ref, za_ref, ob_ref, zb_ref, ga_ref, gb_ref, x_ref, woa_ref, wob_ref, wout_ref, pw_ref, o_ref):
    a = (oa_ref[...] * jax.nn.silu(za_ref[...])).astype(BF16)
    ya = jnp.dot(a, woa_ref[...], preferred_element_type=F32)
    b = (ob_ref[...] * jax.nn.silu(zb_ref[...])).astype(BF16)
    yb = jnp.dot(b, wob_ref[...], preferred_element_type=F32)
    mix = jax.nn.sigmoid(ga_ref[...]) * ya + jax.nn.sigmoid(gb_ref[...]) * yb
    out = jnp.dot(mix.astype(BF16), wout_ref[...], preferred_element_type=F32)
    ms = jnp.mean(out * out, axis=-1, keepdims=True)
    o_ref[...] = x_ref[...] + out * lax.rsqrt(ms + EPS) * pw_ref[...]


def _out_stage(o_a, o_b, proj_b, x2, woa, wob, wout, pw):
    tm = BLK
    prow = lambda g: g + g // NQB + 1
    const = lambda g: (0, 0)
    return pl.pallas_call(
        _out_kernel,
        grid=(BATCH * NQB,),
        in_specs=[
            pl.BlockSpec((tm, A_WIDTH), lambda g: (g, 0)),
            pl.BlockSpec((tm, A_WIDTH), lambda g: (prow(g), 0)),
            pl.BlockSpec((tm, B_WIDTH), lambda g: (g, 0)),
            pl.BlockSpec((tm, B_WIDTH), lambda g: (prow(g), 1)),
            pl.BlockSpec((tm, D_MODEL), lambda g: (prow(g), 1)),
            pl.BlockSpec((tm, D_MODEL), lambda g: (prow(g), 2)),
            pl.BlockSpec((tm, D_MODEL), lambda g: (g, 0)),
            pl.BlockSpec((A_WIDTH, D_MODEL), const, pipeline_mode=pl.Buffered(1)),
            pl.BlockSpec((B_WIDTH, D_MODEL), const, pipeline_mode=pl.Buffered(1)),
            pl.BlockSpec((D_MODEL, D_MODEL), const, pipeline_mode=pl.Buffered(1)),
            pl.BlockSpec((1, D_MODEL), const),
        ],
        out_specs=pl.BlockSpec((tm, D_MODEL), lambda g: (g, 0)),
        out_shape=jax.ShapeDtypeStruct((BATCH * SEQ, D_MODEL), F32),
        compiler_params=pltpu.CompilerParams(
            dimension_semantics=("arbitrary",), vmem_limit_bytes=VMEM_LIMIT),
        name="out_stage",
    )(o_a, proj_b, o_b, proj_b, proj_b, proj_b, x2, woa, wob, wout, pw)


def kernel(x, meta_tokens, rel_bias, pre_norm_w, w_in, kv_norm_w, w_uk, w_uv, idx_k_norm_w, idx_k_norm_b,
           diff_lambda, diff_subln_w, w_o_a, w_o_b, w_out, post_norm_w):
    assert x.shape == (BATCH, SEQ, D_MODEL) and w_in.shape[0] == 1
    layer = 0
    lam_init = 0.8 - 0.6 * math.exp(-0.3 * layer)

    meta_block = jnp.concatenate([meta_tokens.astype(F32), jnp.zeros((BLK - N_META, D_MODEL), F32)], axis=0)

    offs = np.concatenate([[0], np.cumsum(IN_SIZES)])
    w = w_in[0]
    seg = lambda k: w[:, offs[k]:offs[k + 1]]
    zcols = lambda n: jnp.zeros((D_MODEL, n), w.dtype)
    group_a, group_b = (0, 3, 6, 7, 8), (2, 9, 10, 11)
    w_all = jnp.concatenate([seg(k) for k in group_a + group_b]
                            + [seg(1), seg(4), zcols(BLK - IDX_DIM), seg(5), zcols(BLK - IDX_HEADS)],
                            axis=1).astype(BF16)
    n_a = sum(IN_SIZES[k] for k in group_a)
    n_b = sum(IN_SIZES[k] for k in group_b)
    qb0 = IN_SIZES[0] + IN_SIZES[3]
    scale_a = jnp.ones((1, n_a), F32).at[:, qb0:qb0 + IN_SIZES[6]].set(B_QK_DIM ** -0.5 * LOG2E)

    u = _prenorm(x.reshape(BATCH * SEQ, D_MODEL), meta_block, pre_norm_w[0][None].astype(F32))
    proj_a = _matmul(u, w_all, 0, n_a, BF16, 1056, 1024, "proj_a", col_scale=scale_a)
    proj_b = _matmul(u, w_all, n_a, n_b, F32, 1056, 1024, "proj_b")
    proj_c = _matmul(u, w_all, n_a + n_b, 4 * BLK, F32, 1056, 4 * BLK, "proj_c")

    vcol = qb0 + IN_SIZES[6] + IN_SIZES[7]
    ckv, ckvt, ik, iwt, vt = _kvprep(proj_c, proj_a, vcol, kv_norm_w[0][None].astype(F32),
                                     idx_k_norm_w[0][None].astype(F32), idx_k_norm_b[0][None].astype(F32))

    bias = _bias_tiles(rel_bias) * LOG2E
    bias_a = jnp.transpose(bias[:, :A_HEADS], (0, 2, 1, 3)).reshape(5, BLK, A_HEADS * BLK)
    bias_b = bias[:, A_HEADS:]

    wuk = jnp.transpose(w_uk[0], (1, 0, 2)).astype(BF16)
    wuvt = jnp.transpose(w_uv[0], (1, 2, 0)).astype(BF16)
    o_a = _attn_a(proj_a, iwt, ckv, ckvt, ik, wuk, wuvt, bias_a)

    vt = vt.reshape(BATCH, NKB, B_HEADS, B_V_DIM, BLK)
    subw = jnp.broadcast_to(diff_subln_w[0].astype(F32)[:, None], (B_V_DIM, BLK))
    o_b = _attn_b(proj_a, vt, bias_b, diff_lambda[0].astype(F32), subw, lam_init)

    out = _out_stage(o_a, o_b, proj_b, x.reshape(BATCH * SEQ, D_MODEL),
                     w_o_a[0].astype(BF16), w_o_b[0].astype(BF16), w_out[0].astype(BF16),
                     post_norm_w[0][None].astype(F32))
    return out.reshape(BATCH, SEQ, D_MODEL)
```

```python
import functools
import math

import numpy as np
import jax
import jax.numpy as jnp
from jax import lax
from jax.experimental import pallas as pl
from jax.experimental.pallas import tpu as pltpu

D_MODEL = 2048
BATCH = 2
SEQ = 4096
CHUNK = 64
N_META = 16
N_BUCKETS = 32
MAX_DISTANCE = 128
A_HEADS = 8
A_HEAD_DIM = 128
KV_RANK = 256
IDX_HEADS = 16
IDX_DIM = 64
TOPK = 256
B_HEADS = 8
B_QK_DIM = 64
B_V_DIM = 128
A_WIDTH = A_HEADS * A_HEAD_DIM
B_WIDTH = B_HEADS * B_V_DIM
IN_SIZES = (A_WIDTH, KV_RANK, A_WIDTH, IDX_HEADS * IDX_DIM, IDX_DIM, IDX_HEADS,
            2 * B_HEADS * B_QK_DIM, 2 * B_HEADS * B_QK_DIM, B_WIDTH, B_WIDTH,
            D_MODEL, D_MODEL)
EPS = 1e-6

BLK = 128
NQB = SEQ // BLK
NKB = NQB + 1
TP = NKB * BLK
ROWS = BATCH * TP
FAR = 4
HPS = 8
NEG = -1e30
INT_MIN = -2 ** 31
HALF = 2 ** 15
LOG2E = math.log2(math.e)
VMEM_LIMIT = 56 * 1024 * 1024

F32 = jnp.float32
BF16 = jnp.bfloat16
NT_DIMS = (((1,), (1,)), ((), ()))


def _t5_bucket_np(rel):
    nb = N_BUCKETS // 2
    max_exact = nb // 2
    ret = np.where(rel > 0, nb, 0)
    n = np.abs(rel)
    nf = np.maximum(n, 1).astype(np.float32)
    large = max_exact + (np.log(nf / np.float32(max_exact))
                         / np.float32(math.log(MAX_DISTANCE / max_exact))
                         * np.float32(nb - max_exact)).astype(np.int32)
    large = np.minimum(large, nb - 1)
    return ret + np.where(n < max_exact, n, large)


T_DIAG, T_PREV, T_META0, T_METAFAR, T_NONE = range(5)


def _bias_tiles(rel_bias):
    a = np.arange(BLK)[:, None]
    b = np.arange(BLK)[None, :]
    far = np.full((BLK, BLK), -4 * BLK)
    nowhere = np.zeros((BLK, BLK), bool)
    pad_rows = (a >= N_META) | nowhere
    rels = np.stack([a - b, a - b - BLK, a - N_META - b, far, far])
    dis = np.stack([(a >= CHUNK) & (b < CHUNK), nowhere, pad_rows, pad_rows, ~nowhere])
    idx = _t5_bucket_np(rels)
    far_bucket = N_BUCKETS // 2 - 1
    assert _t5_bucket_np(np.array([-BLK - 1]))[0] == far_bucket == idx[T_METAFAR, 0, 0]
    rb = rel_bias.astype(F32)
    tiles = jnp.zeros((5, A_HEADS + B_HEADS, BLK, BLK), F32)
    for k in range(N_BUCKETS):
        tiles = jnp.where((idx == k)[:, None], rb[k][None, :, None, None], tiles)
    tiles = tiles - rb[far_bucket][None, :, None, None]
    return jnp.where(dis[:, None], NEG, tiles)


def _far_split(i):
    n_far = jnp.maximum(i - 1, 0)
    return n_far, lax.shift_right_logical(n_far, 2)


def _visit_key_blocks(i, group_fn, special_fn, carry):
    n_far, n_chunks = _far_split(i)
    carry = lax.fori_loop(0, n_chunks, lambda c, cr: group_fn(1 + FAR * c, FAR, cr), carry)
    carry = lax.fori_loop(1 + FAR * n_chunks, 1 + n_far, lambda kb, cr: group_fn(kb, 1, cr), carry)
    return special_fn(carry)


def _prenorm_kernel(x_ref, meta_ref, w_ref, o_ref):
    def norm(x):
        ms = jnp.mean(x * x, axis=-1, keepdims=True)
        return (x * lax.rsqrt(ms + EPS) * w_ref[...]).astype(o_ref.dtype)

    is_meta = lax.rem(pl.program_id(0), NKB) == 0

    @pl.when(is_meta)
    def _():
        o_ref[...] = norm(meta_ref[...])

    @pl.when(jnp.logical_not(is_meta))
    def _():
        o_ref[...] = norm(x_ref[...])


def _prenorm(x2, meta_block, w):
    frame_block = lambda r: (jnp.maximum(r - r // NKB - 1, 0), 0)
    return pl.pallas_call(
        _prenorm_kernel,
        grid=(ROWS // BLK,),
        in_specs=[pl.BlockSpec((BLK, D_MODEL), frame_block),
                  pl.BlockSpec((BLK, D_MODEL), lambda r: (0, 0)),
                  pl.BlockSpec((1, D_MODEL), lambda r: (0, 0))],
        out_specs=pl.BlockSpec((BLK, D_MODEL), lambda r: (r, 0)),
        out_shape=jax.ShapeDtypeStruct((ROWS, D_MODEL), BF16),
        compiler_params=pltpu.CompilerParams(dimension_semantics=("arbitrary",)),
        name="prenorm",
    )(x2, meta_block, w)


def _mm_kernel(a_ref, w_ref, o_ref):
    o_ref[...] = jnp.dot(a_ref[...], w_ref[...], preferred_element_type=F32).astype(o_ref.dtype)


def _mm_scaled_kernel(a_ref, w_ref, cs_ref, o_ref):
    acc = jnp.dot(a_ref[...], w_ref[...], preferred_element_type=F32)
    o_ref[...] = (acc * cs_ref[...]).astype(o_ref.dtype)


def _matmul(a, w, col0, n, out_dtype, tm, tn, name, col_scale=None):
    m, k = a.shape
    assert col0 % tn == 0 and n % tn == 0 and m % tm == 0
    c0 = col0 // tn
    in_specs = [pl.BlockSpec((tm, k), lambda i, j: (i, 0)),
                pl.BlockSpec((k, tn), lambda i, j: (0, c0 + j))]
    args = (a, w)
    if col_scale is not None:
        in_specs.append(pl.BlockSpec((1, tn), lambda i, j: (0, j)))
        args += (col_scale,)
    return pl.pallas_call(
        _mm_kernel if col_scale is None else _mm_scaled_kernel,
        grid=(m // tm, n // tn),
        in_specs=in_specs,
        out_specs=pl.BlockSpec((tm, tn), lambda i, j: (i, j)),
        out_shape=jax.ShapeDtypeStruct((m, n), out_dtype),
        compiler_params=pltpu.CompilerParams(
            dimension_semantics=("arbitrary", "arbitrary"), vmem_limit_bytes=VMEM_LIMIT),
        name=name,
    )(*args)


def _kvprep_kernel(c_ref, v_ref, kvw_ref, ikw_ref, ikb_ref, ckv_ref, ckvt_ref, ik_ref, iwt_ref, vt_ref):
    for blk in range(2):
        for h in range(B_HEADS):
            vh = v_ref[blk * BLK:(blk + 1) * BLK, h * B_V_DIM:(h + 1) * B_V_DIM]
            vt_ref[blk, h] = vh.astype(F32).T.astype(BF16)
    c = c_ref[...]
    ckv = c[:, :KV_RANK]
    ms = jnp.mean(ckv * ckv, axis=-1, keepdims=True)
    ckvn = ckv * lax.rsqrt(ms + EPS) * kvw_ref[...]
    ckv_ref[...] = ckvn.astype(BF16)
    ckvt_ref[0] = ckvn[:BLK].T.astype(BF16)
    ckvt_ref[1] = ckvn[BLK:].T.astype(BF16)
    ik = c[:, KV_RANK:KV_RANK + IDX_DIM]
    mu = jnp.mean(ik, axis=-1, keepdims=True)
    var = jnp.mean(jnp.square(ik - mu), axis=-1, keepdims=True)
    ikn = (ik - mu) * lax.rsqrt(var + EPS) * ikw_ref[...] + ikb_ref[...]
    ik_ref[...] = ikn.astype(BF16)
    iw = c[:, 3 * BLK:4 * BLK] * (IDX_HEADS ** -0.5 * IDX_DIM ** -0.5)
    iwt_ref[...] = iw.T[:IDX_HEADS, :]


def _kvprep(c, proj_a, vcol, kvw, ikw, ikb):
    tm = 2 * BLK
    assert vcol % B_WIDTH == 0
    return pl.pallas_call(
        _kvprep_kernel,
        grid=(ROWS // tm,),
        in_specs=[pl.BlockSpec((tm, 4 * BLK), lambda i: (i, 0)),
                  pl.BlockSpec((tm, B_WIDTH), lambda i: (i, vcol // B_WIDTH)),
                  pl.BlockSpec((1, KV_RANK), lambda i: (0, 0)),
                  pl.BlockSpec((1, IDX_DIM), lambda i: (0, 0)),
                  pl.BlockSpec((1, IDX_DIM), lambda i: (0, 0))],
        out_specs=[pl.BlockSpec((tm, KV_RANK), lambda i: (i, 0)),
                   pl.BlockSpec((2, KV_RANK, BLK), lambda i: (i, 0, 0)),
                   pl.BlockSpec((tm, IDX_DIM), lambda i: (i, 0)),
                   pl.BlockSpec((IDX_HEADS, tm), lambda i: (0, i)),
                   pl.BlockSpec((2, B_HEADS, B_V_DIM, BLK), lambda i: (i, 0, 0, 0))],
        out_shape=[jax.ShapeDtypeStruct((ROWS, KV_RANK), BF16),
                   jax.ShapeDtypeStruct((ROWS // BLK, KV_RANK, BLK), BF16),
                   jax.ShapeDtypeStruct((ROWS, IDX_DIM), BF16),
                   jax.ShapeDtypeStruct((IDX_HEADS, ROWS), F32),
                   jax.ShapeDtypeStruct((ROWS // BLK, B_HEADS, B_V_DIM, BLK), BF16)],
        name="kvprep",
    )(c, proj_a, kvw, ikw, ikb)


def _online_softmax_step(s, m, l):
    m_new = jnp.maximum(m, jnp.max(s, axis=0, keepdims=True))
    alpha = jnp.exp2(m - m_new)
    p = jnp.exp2(s - m_new)
    return m_new, alpha * l + jnp.sum(p, axis=0, keepdims=True), alpha, p


def _attn_a_kernel(qa_ref, iq_ref, iwt_ref, ckv_ref, ckvt_ref, ik_ref, wuk_ref, wuvt_ref, bias_ref,
                   o_ref, keys_ref, qlat_ref, acc_ref, iqt_ref, khi_ref, klo_ref, lom_ref):
    i = pl.program_id(1)
    nkb = i + 2
    NG = A_HEADS // 2
    GW = 2 * BLK
    t_meta = jnp.where(i == 0, T_META0, T_METAFAR)
    t_prev = jnp.where(i == 0, T_NONE, T_PREV)
    special_blocks = (0, i, i + 1)

    for h in range(A_HEADS):
        qh = qa_ref[:, h * BLK:(h + 1) * BLK]
        ql = lax.dot_general(wuk_ref[h], qh, NT_DIMS, preferred_element_type=F32)
        qlat_ref[:, h * BLK:(h + 1) * BLK] = (ql * (A_HEAD_DIM ** -0.5 * LOG2E)).astype(BF16)

    for pr in range(IDX_HEADS // 2):
        t = iq_ref[:, pr * BLK:(pr + 1) * BLK].astype(F32).T
        iqt_ref[:, (2 * pr) * BLK:(2 * pr + 1) * BLK] = t[:IDX_DIM].astype(BF16)
        iqt_ref[:, (2 * pr + 1) * BLK:(2 * pr + 2) * BLK] = t[IDX_DIM:].astype(BF16)

    iwt = iwt_ref[...]
    row = lax.broadcasted_iota(jnp.int32, (BLK, BLK), 0)
    lane = lax.broadcasted_iota(jnp.int32, (BLK, BLK), 1)

    def idx_keys(ikrows):
        sc = jnp.zeros((ikrows.shape[0], BLK), F32)
        for pr in range(IDX_HEADS // 2):
            s2 = jnp.dot(ikrows, iqt_ref[:, pr * 2 * BLK:(pr + 1) * 2 * BLK], preferred_element_type=F32)
            sc = sc + jnp.maximum(s2[:, :BLK], 0.0) * iwt[2 * pr:2 * pr + 1, :]
            sc = sc + jnp.maximum(s2[:, BLK:], 0.0) * iwt[2 * pr + 1:2 * pr + 2, :]
        bits = lax.bitcast_convert_type(sc, jnp.int32)
        return bits ^ ((bits >> 31) & 0x7FFFFFFF)

    def put_keys(off, n, key):
        keys_ref[pl.ds(off, n * BLK), :] = key

    def idx_group(kb, n, carry):
        off = pl.multiple_of(kb * BLK, BLK)
        put_keys(off, n, idx_keys(ik_ref[pl.ds(off, n * BLK), :]))
        return carry

    def idx_special(carry):
        offs = [pl.multiple_of(kb * BLK, BLK) for kb in special_blocks]
        key = idx_keys(jnp.concatenate([ik_ref[pl.ds(o, BLK), :] for o in offs], axis=0))
        put_keys(offs[1], 1, key[BLK:2 * BLK])
        put_keys(offs[2], 1, jnp.where((row >= CHUNK) & (lane < CHUNK), INT_MIN, key[2 * BLK:]))
        put_keys(offs[0], 1, jnp.where(row >= N_META, INT_MIN, key[:BLK]))
        return carry

    _visit_key_blocks(i, idx_group, idx_special, 0)
    put_keys(pl.multiple_of(nkb * BLK, BLK), FAR - 1, jnp.full(((FAR - 1) * BLK, BLK), INT_MIN, jnp.int32))
    n_search = lax.shift_right_logical(nkb + FAR - 1, 2)
    crow = lax.broadcasted_iota(jnp.int32, (FAR * BLK, BLK), 0)
    chunk_off = lambda c: pl.multiple_of(c * FAR * BLK, FAR * BLK)

    def count(pred_fn):
        def body(c, acc8):
            k = keys_ref[pl.ds(chunk_off(c), FAR * BLK), :]
            hit = pred_fn(k, chunk_off(c)).astype(jnp.int32)
            return acc8 + jnp.sum(hit.reshape(FAR * BLK // 8, 8, BLK), axis=0)
        acc8 = lax.fori_loop(0, n_search, body, jnp.zeros((8, BLK), jnp.int32))
        return jnp.sum(acc8, axis=0, keepdims=True)

    PACK = 16
    one16 = jnp.ones((PACK, BLK), jnp.int16)
    zero16 = jnp.zeros((PACK, BLK), jnp.int16)

    def count16(tab_ref, cand, strict=False):
        c16 = jnp.broadcast_to(cand, (PACK, BLK)).astype(jnp.int16)

        def body(c, acc):
            t = tab_ref[pl.ds(chunk_off(c), FAR * BLK), :]
            hits = []
            for r in range(FAR * BLK // PACK):
                tr = t[r * PACK:(r + 1) * PACK]
                hits.append(jnp.where((tr > c16) if strict else (tr >= c16), one16, zero16))
            while len(hits) > 1:
                hits = [hits[a] + hits[a + 1] for a in range(0, len(hits), 2)]
            return acc + hits[0]
        acc = lax.fori_loop(0, n_search, body, zero16)
        return jnp.sum(acc.astype(jnp.int32), axis=0, keepdims=True)

    def kth_largest16(tab_ref, k):
        c0 = count16(tab_ref, jnp.zeros((1, BLK), jnp.int32))
        prefix = jnp.where(c0 >= k, 0, -HALF).astype(jnp.int32)

        def bit_body(t, prefix):
            cand = prefix | jnp.left_shift(jnp.int32(1), 14 - t)
            return jnp.where(count16(tab_ref, cand) >= k, cand, prefix)
        return lax.fori_loop(0, 15, bit_body, prefix)

    zero = jnp.zeros((1, BLK), jnp.int32)
    c0 = count(lambda k, off: k >= zero)
    prefix = jnp.where(c0 >= TOPK, 0, INT_MIN).astype(jnp.int32)

    def bit_body(t, prefix):
        cand = prefix | jnp.left_shift(jnp.int32(1), 30 - t)
        c = count(lambda k, off: k >= cand)
        return jnp.where(c >= TOPK, cand, prefix)

    thr = lax.fori_loop(0, 31, bit_body, prefix)
    full = thr == INT_MIN

    c_gt = count(lambda k, off: k > thr)
    c_eq = count(lambda k, off: k == thr)
    need = TOPK - c_gt
    tied = jnp.logical_and(jnp.logical_not(full), c_eq > need)
    j_default = jnp.where(full, -1, TP).astype(jnp.int32)

    def tie_search():
        def jbit(t, j):
            cand = j | jnp.left_shift(jnp.int32(1), 12 - t)
            c = count(lambda k, off: (k == thr) & ((off + crow) < cand))
            return jnp.where(c < need, cand, j)
        j = lax.fori_loop(0, 13, jbit, jnp.zeros((1, BLK), jnp.int32))
        return jnp.where(tied, j, j_default)

    any_tied = jnp.max(tied.astype(jnp.int32)) > 0
    jmax = lax.cond(any_tied, tie_search, lambda: j_default)

    acc_ref[...] = jnp.zeros_like(acc_ref)

    def att_update(rows, keys, rowidx, vt, bias, carry):
        sel = (keys > thr) | ((keys == thr) & (rowidx <= jmax))
        selb = jnp.where(sel, 0.0, NEG)
        add = jnp.concatenate([selb, selb], axis=1)
        gs = [slice(g * GW, (g + 1) * GW) for g in range(NG)]
        ss = [jnp.dot(rows, qlat_ref[:, gs[g]], preferred_element_type=F32) for g in range(NG)]
        ss = [ss[g] + (add if bias is None else add + bias[:, gs[g]]) for g in range(NG)]
        steps = [_online_softmax_step(ss[g], *carry[g]) for g in range(NG)]
        pvs = [jnp.dot(vt, steps[g][3].astype(BF16), preferred_element_type=F32) for g in range(NG)]
        for g in range(NG):
            acc_ref[:, gs[g]] = acc_ref[:, gs[g]] * steps[g][2] + pvs[g]
        return tuple((steps[g][0], steps[g][1]) for g in range(NG))

    def att_group(kb, n, carry):
        off = pl.multiple_of(kb * BLK, BLK)
        rowidx = off + lax.broadcasted_iota(jnp.int32, (n * BLK, BLK), 0)
        vt = jnp.concatenate([ckvt_ref[kb + u] for u in range(n)], axis=1) if n > 1 else ckvt_ref[kb]
        return att_update(ckv_ref[pl.ds(off, n * BLK), :], keys_ref[pl.ds(off, n * BLK), :], rowidx, vt, None, carry)

    def att_special(carry):
        offs = [pl.multiple_of(kb * BLK, BLK) for kb in special_blocks]
        rows = jnp.concatenate([ckv_ref[pl.ds(o, BLK), :] for o in offs], axis=0)
        keys = jnp.concatenate([keys_ref[pl.ds(o, BLK), :] for o in offs], axis=0)
        rowidx = jnp.concatenate([o + row for o in offs], axis=0)
        vt = jnp.concatenate([ckvt_ref[kb] for kb in special_blocks], axis=1)
        bias = jnp.concatenate([bias_ref[t_meta], bias_ref[t_prev], bias_ref[T_DIAG]], axis=0)
        return att_update(rows, keys, rowidx, vt, bias, carry)

    m0 = jnp.full((1, GW), NEG, F32)
    l0 = jnp.zeros((1, GW), F32)
    stats = _visit_key_blocks(i, att_group, att_special, tuple((m0, l0) for _ in range(NG)))
    l = jnp.concatenate([stats[g][1] for g in range(NG)], axis=1)

    olat = (acc_ref[...] / l).astype(BF16)
    for h in range(A_HEADS):
        ot = jnp.dot(wuvt_ref[h], olat[:, h * BLK:(h + 1) * BLK], preferred_element_type=F32)
        o_ref[:, h * BLK:(h + 1) * BLK] = ot.T


def _attn_a(proj_a, iwt, ckv, ckvt, ik, wuk, wuvt, bias_a):
    qrow = lambda b, i: b * NKB + 1 + i
    return pl.pallas_call(
        _attn_a_kernel,
        grid=(BATCH, NQB),
        in_specs=[
            pl.BlockSpec((BLK, A_WIDTH), lambda b, i: (qrow(b, i), 0)),
            pl.BlockSpec((BLK, IDX_HEADS * IDX_DIM), lambda b, i: (qrow(b, i), 1)),
            pl.BlockSpec((IDX_HEADS, BLK), lambda b, i: (0, qrow(b, i))),
            pl.BlockSpec((None, TP, KV_RANK), lambda b, i: (b, 0, 0)),
            pl.BlockSpec((None, NKB, KV_RANK, BLK), lambda b, i: (b, 0, 0, 0)),
            pl.BlockSpec((None, TP, IDX_DIM), lambda b, i: (b, 0, 0)),
            pl.BlockSpec((A_HEADS, KV_RANK, A_HEAD_DIM), lambda b, i: (0, 0, 0)),
            pl.BlockSpec((A_HEADS, A_HEAD_DIM, KV_RANK), lambda b, i: (0, 0, 0)),
            pl.BlockSpec((5, BLK, A_HEADS * BLK), lambda b, i: (0, 0, 0)),
        ],
        out_specs=pl.BlockSpec((BLK, A_WIDTH), lambda b, i: (b * NQB + i, 0)),
        out_shape=jax.ShapeDtypeStruct((BATCH * SEQ, A_WIDTH), F32),
        scratch_shapes=[pltpu.VMEM(((NKB + FAR - 1) * BLK, BLK), jnp.int32),
                        pltpu.VMEM((KV_RANK, A_HEADS * BLK), BF16),
                        pltpu.VMEM((KV_RANK, A_HEADS * BLK), F32),
                        pltpu.VMEM((IDX_DIM, IDX_HEADS * BLK), BF16)]
                       + [pltpu.VMEM(((NKB + FAR - 1) * BLK, BLK), jnp.int16)] * 3,
        compiler_params=pltpu.CompilerParams(
            dimension_semantics=("arbitrary", "arbitrary"), vmem_limit_bytes=VMEM_LIMIT),
        name="attn_a",
    )(proj_a, proj_a, iwt, ckv.reshape(BATCH, TP, KV_RANK), ckvt.reshape(BATCH, NKB, KV_RANK, BLK),
      ik.reshape(BATCH, TP, IDX_DIM), wuk, wuvt, bias_a)


def _attn_b_kernel(lam_ref, q_ref, k_ref, vt_ref, bias_ref, subw_ref, o_ref, acc_ref, *, lam_init):
    i = pl.program_id(2)
    t_meta = jnp.where(i == 0, T_META0, T_METAFAR)
    t_prev = jnp.where(i == 0, T_NONE, T_PREV)
    special_blocks = (0, i, i + 1)
    lp = lam_ref[...]
    lam = (jnp.exp(jnp.sum(lp[0:1] * lp[1:2], axis=-1, keepdims=True))
           - jnp.exp(jnp.sum(lp[2:3] * lp[3:4], axis=-1, keepdims=True)) + lam_init)

    lane = lax.broadcasted_iota(jnp.int32, (BLK, BLK), 1)
    qbd = []
    for hh in range(HPS):
        q = q_ref[:, hh * BLK:(hh + 1) * BLK]
        zq = jnp.zeros_like(q)
        qbd.append(jnp.concatenate([jnp.where(lane < B_QK_DIM, q, zq), jnp.where(lane >= B_QK_DIM, q, zq)], axis=0))

    acc_ref[...] = jnp.zeros_like(acc_ref)

    def update_all(rows, vts, biases, carry):
        ss = [lax.dot_general(rows[hh], qbd[hh], NT_DIMS, preferred_element_type=F32) for hh in range(HPS)]
        if biases is not None:
            ss = [ss[hh] + jnp.concatenate([biases[hh], biases[hh]], axis=1) for hh in range(HPS)]
        steps = [_online_softmax_step(ss[hh], *carry[hh]) for hh in range(HPS)]
        pvs = [jnp.dot(vts[hh], steps[hh][3].astype(BF16), preferred_element_type=F32) for hh in range(HPS)]
        for hh in range(HPS):
            acc_ref[hh] = acc_ref[hh] * steps[hh][2] + pvs[hh]
        return tuple((steps[hh][0], steps[hh][1]) for hh in range(HPS))

    def group(kb, n, carry):
        off = pl.multiple_of(kb * BLK, BLK)
        rows = [k_ref[pl.ds(off, n * BLK), hh * BLK:(hh + 1) * BLK] for hh in range(HPS)]
        vts = [jnp.concatenate([vt_ref[kb + u, hh] for u in range(n)], axis=1) if n > 1 else vt_ref[kb, hh]
               for hh in range(HPS)]
        return update_all(rows, vts, None, carry)

    def special(carry):
        offs = [pl.multiple_of(kb * BLK, BLK) for kb in special_blocks]
        rows = [jnp.concatenate([k_ref[pl.ds(o, BLK), hh * BLK:(hh + 1) * BLK] for o in offs], axis=0)
                for hh in range(HPS)]
        vts = [jnp.concatenate([vt_ref[kb, hh] for kb in special_blocks], axis=1) for hh in range(HPS)]
        biases = [jnp.concatenate([bias_ref[t_meta, hh], bias_ref[t_prev, hh], bias_ref[T_DIAG, hh]], axis=0)
                  for hh in range(HPS)]
        return update_all(rows, vts, biases, carry)

    m0 = jnp.full((1, 2 * BLK), NEG, F32)
    l0 = jnp.zeros((1, 2 * BLK), F32)
    stats = _visit_key_blocks(i, group, special, tuple((m0, l0) for _ in range(HPS)))

    for hh in range(HPS):
        a = acc_ref[hh] / stats[hh][1]
        o = a[:, :BLK] - lam * a[:, BLK:]
        ms = jnp.mean(o * o, axis=0, keepdims=True)
        y = o * lax.rsqrt(ms + EPS) * subw_ref[...] * (1.0 - lam_init)
        o_ref[:, hh * BLK:(hh + 1) * BLK] = y.T


def _attn_b(proj_a, vt, bias_b, lam_p, subw, lam_init):
    qrow = lambda b, g, i: b * NKB + 1 + i
    wide = HPS * BLK
    qcol0 = (A_WIDTH + IDX_HEADS * IDX_DIM) // wide
    kcol0 = qcol0 + 2 * B_HEADS * B_QK_DIM // wide
    return pl.pallas_call(
        functools.partial(_attn_b_kernel, lam_init=lam_init),
        grid=(BATCH, B_HEADS // HPS, NQB),
        in_specs=[
            pl.BlockSpec((4, B_QK_DIM), lambda b, g, i: (0, 0)),
            pl.BlockSpec((BLK, wide), lambda b, g, i: (qrow(b, g, i), qcol0 + g)),
            pl.BlockSpec((None, TP, wide), lambda b, g, i: (b, 0, kcol0 + g)),
            pl.BlockSpec((None, NKB, HPS, B_V_DIM, BLK), lambda b, g, i: (b, 0, g, 0, 0)),
            pl.BlockSpec((5, HPS, BLK, BLK), lambda b, g, i: (0, g, 0, 0)),
            pl.BlockSpec((B_V_DIM, BLK), lambda b, g, i: (0, 0)),
        ],
        out_specs=pl.BlockSpec((BLK, wide), lambda b, g, i: (b * NQB + i, g)),
        out_shape=jax.ShapeDtypeStruct((BATCH * SEQ, B_WIDTH), F32),
        scratch_shapes=[pltpu.VMEM((HPS, B_V_DIM, 2 * BLK), F32)],
        compiler_params=pltpu.CompilerParams(
            dimension_semantics=("arbitrary", "arbitrary", "arbitrary"), vmem_limit_bytes=VMEM_LIMIT),
        name="attn_b",
    )(lam_p, proj_a, proj_a.reshape(BATCH, TP, -1), vt, bias_b, subw)


def _out_kernel(oa_ref, za_ref, ob_ref, zb_ref, ga_ref, gb_ref, x_ref, woa_ref, wob_ref, wout_ref, pw_ref, o_ref):
    a = (oa_ref[...] * jax.nn.silu(za_ref[...])).astype(BF16)
    ya = jnp.dot(a, woa_ref[...], preferred_element_type=F32)
    b = (ob_ref[...] * jax.nn.silu(zb_ref[...])).astype(BF16)
    yb = jnp.dot(b, wob_ref[...], preferred_element_type=F32)
    mix = jax.nn.sigmoid(ga_ref[...]) * ya + jax.nn.sigmoid(gb_ref[...]) * yb
    out = jnp.dot(mix.astype(BF16), wout_ref[...], preferred_element_type=F32)
    ms = jnp.mean(out * out, axis=-1, keepdims=True)
    o_ref[...] = x_ref[...] + out * lax.rsqrt(ms + EPS) * pw_ref[...]


def _out_stage(o_a, o_b, proj_b, x2, woa, wob, wout, pw):
    tm = BLK
    prow = lambda g: g + g // NQB + 1
    const = lambda g: (0, 0)
    return pl.pallas_call(
        _out_kernel,
        grid=(BATCH * NQB,),
        in_specs=[
            pl.BlockSpec((tm, A_WIDTH), lambda g: (g, 0)),
            pl.BlockSpec((tm, A_WIDTH), lambda g: (prow(g), 0)),
            pl.BlockSpec((tm, B_WIDTH), lambda g: (g, 0)),
            pl.BlockSpec((tm, B_WIDTH), lambda g: (prow(g), 1)),
            pl.BlockSpec((tm, D_MODEL), lambda g: (prow(g), 1)),
            pl.BlockSpec((tm, D_MODEL), lambda g: (prow(g), 2)),
            pl.BlockSpec((tm, D_MODEL), lambda g: (g, 0)),
            pl.BlockSpec((A_WIDTH, D_MODEL), const, pipeline_mode=pl.Buffered(1)),
            pl.BlockSpec((B_WIDTH, D_MODEL), const, pipeline_mode=pl.Buffered(1)),
            pl.BlockSpec((D_MODEL, D_MODEL), const, pipeline_mode=pl.Buffered(1)),
            pl.BlockSpec((1, D_MODEL), const),
        ],
        out_specs=pl.BlockSpec((tm, D_MODEL), lambda g: (g, 0)),
        out_shape=jax.ShapeDtypeStruct((BATCH * SEQ, D_MODEL), F32),
        compiler_params=pltpu.CompilerParams(
            dimension_semantics=("arbitrary",), vmem_limit_bytes=VMEM_LIMIT),
        name="out_stage",
    )(o_a, proj_b, o_b, proj_b, proj_b, proj_b, x2, woa, wob, wout, pw)


def kernel(x, meta_tokens, rel_bias, pre_norm_w, w_in, kv_norm_w, w_uk, w_uv, idx_k_norm_w, idx_k_norm_b,
           diff_lambda, diff_subln_w, w_o_a, w_o_b, w_out, post_norm_w):
    assert x.shape == (BATCH, SEQ, D_MODEL) and w_in.shape[0] == 1
    layer = 0
    lam_init = 0.8 - 0.6 * math.exp(-0.3 * layer)

    meta_block = jnp.concatenate([meta_tokens.astype(F32), jnp.zeros((BLK - N_META, D_MODEL), F32)], axis=0)

    offs = np.concatenate([[0], np.cumsum(IN_SIZES)])
    w = w_in[0]
    seg = lambda k: w[:, offs[k]:offs[k + 1]]
    zcols = lambda n: jnp.zeros((D_MODEL, n), w.dtype)
    group_a, group_b = (0, 3, 6, 7, 8), (2, 9, 10, 11)
    w_all = jnp.concatenate([seg(k) for k in group_a + group_b]
                            + [seg(1), seg(4), zcols(BLK - IDX_DIM), seg(5), zcols(BLK - IDX_HEADS)],
                            axis=1).astype(BF16)
    n_a = sum(IN_SIZES[k] for k in group_a)
    n_b = sum(IN_SIZES[k] for k in group_b)
    qb0 = IN_SIZES[0] + IN_SIZES[3]
    scale_a = jnp.ones((1, n_a), F32).at[:, qb0:qb0 + IN_SIZES[6]].set(B_QK_DIM ** -0.5 * LOG2E)

    u = _prenorm(x.reshape(BATCH * SEQ, D_MODEL), meta_block, pre_norm_w[0][None].astype(F32))
    proj_a = _matmul(u, w_all, 0, n_a, BF16, 1056, 1024, "proj_a", col_scale=scale_a)
    proj_b = _matmul(u, w_all, n_a, n_b, F32, 1056, 1024, "proj_b")
    proj_c = _matmul(u, w_all, n_a + n_b, 4 * BLK, F32, 1056, 4 * BLK, "proj_c")

    vcol = qb0 + IN_SIZES[6] + IN_SIZES[7]
    ckv, ckvt, ik, iwt, vt = _kvprep(proj_c, proj_a, vcol, kv_norm_w[0][None].astype(F32),
                                     idx_k_norm_w[0][None].astype(F32), idx_k_norm_b[0][None].astype(F32))

    bias = _bias_tiles(rel_bias) * LOG2E
    bias_a = jnp.transpose(bias[:, :A_HEADS], (0, 2, 1, 3)).reshape(5, BLK, A_HEADS * BLK)
    bias_b = bias[:, A_HEADS:]

    wuk = jnp.transpose(w_uk[0], (1, 0, 2)).astype(BF16)
    wuvt = jnp.transpose(w_uv[0], (1, 2, 0)).astype(BF16)
    o_a = _attn_a(proj_a, iwt, ckv, ckvt, ik, wuk, wuvt, bias_a)

    vt = vt.reshape(BATCH, NKB, B_HEADS, B_V_DIM, BLK)
    subw = jnp.broadcast_to(diff_subln_w[0].astype(F32)[:, None], (B_V_DIM, BLK))
    o_b = _attn_b(proj_a, vt, bias_b, diff_lambda[0].astype(F32), subw, lam_init)

    out = _out_stage(o_a, o_b, proj_b, x.reshape(BATCH * SEQ, D_MODEL),
                     w_o_a[0].astype(BF16), w_o_b[0].astype(BF16), w_out[0].astype(BF16),
                     post_norm_w[0][None].astype(F32))
    return out.reshape(BATCH, SEQ, D_MODEL)
```

```python
import functools
import math

import numpy as np
import jax
import jax.numpy as jnp
from jax import lax
from jax.experimental import pallas as pl
from jax.experimental.pallas import tpu as pltpu

D_MODEL = 2048
BATCH = 2
SEQ = 4096
CHUNK = 64
N_META = 16
N_BUCKETS = 32
MAX_DISTANCE = 128
A_HEADS = 8
A_HEAD_DIM = 128
KV_RANK = 256
IDX_HEADS = 16
IDX_DIM = 64
TOPK = 256
B_HEADS = 8
B_QK_DIM = 64
B_V_DIM = 128
A_WIDTH = A_HEADS * A_HEAD_DIM
B_WIDTH = B_HEADS * B_V_DIM
IN_SIZES = (A_WIDTH, KV_RANK, A_WIDTH, IDX_HEADS * IDX_DIM, IDX_DIM, IDX_HEADS,
            2 * B_HEADS * B_QK_DIM, 2 * B_HEADS * B_QK_DIM, B_WIDTH, B_WIDTH,
            D_MODEL, D_MODEL)
EPS = 1e-6

BLK = 128
NQB = SEQ // BLK
NKB = NQB + 1
TP = NKB * BLK
ROWS = BATCH * TP
FAR = 4
HPS = 8
NEG = -1e30
INT_MIN = -2 ** 31
LOG2E = math.log2(math.e)
VMEM_LIMIT = 56 * 1024 * 1024

F32 = jnp.float32
BF16 = jnp.bfloat16
NT_DIMS = (((1,), (1,)), ((), ()))


def _t5_bucket_np(rel):
    nb = N_BUCKETS // 2
    max_exact = nb // 2
    ret = np.where(rel > 0, nb, 0)
    n = np.abs(rel)
    nf = np.maximum(n, 1).astype(np.float32)
    large = max_exact + (np.log(nf / np.float32(max_exact))
                         / np.float32(math.log(MAX_DISTANCE / max_exact))
                         * np.float32(nb - max_exact)).astype(np.int32)
    large = np.minimum(large, nb - 1)
    return ret + np.where(n < max_exact, n, large)


T_DIAG, T_PREV, T_META0, T_METAFAR, T_NONE = range(5)


def _bias_tiles(rel_bias):
    a = np.arange(BLK)[:, None]
    b = np.arange(BLK)[None, :]
    far = np.full((BLK, BLK), -4 * BLK)
    nowhere = np.zeros((BLK, BLK), bool)
    pad_rows = (a >= N_META) | nowhere
    rels = np.stack([a - b, a - b - BLK, a - N_META - b, far, far])
    dis = np.stack([(a >= CHUNK) & (b < CHUNK), nowhere, pad_rows, pad_rows, ~nowhere])
    idx = _t5_bucket_np(rels)
    far_bucket = N_BUCKETS // 2 - 1
    assert _t5_bucket_np(np.array([-BLK - 1]))[0] == far_bucket == idx[T_METAFAR, 0, 0]
    rb = rel_bias.astype(F32)
    tiles = jnp.zeros((5, A_HEADS + B_HEADS, BLK, BLK), F32)
    for k in range(N_BUCKETS):
        tiles = jnp.where((idx == k)[:, None], rb[k][None, :, None, None], tiles)
    tiles = tiles - rb[far_bucket][None, :, None, None]
    return jnp.where(dis[:, None], NEG, tiles)


def _far_split(i):
    n_far = jnp.maximum(i - 1, 0)
    return n_far, lax.shift_right_logical(n_far, 2)


def _visit_key_blocks(i, group_fn, special_fn, carry):
    n_far, n_chunks = _far_split(i)
    carry = lax.fori_loop(0, n_chunks, lambda c, cr: group_fn(1 + FAR * c, FAR, cr), carry)
    carry = lax.fori_loop(1 + FAR * n_chunks, 1 + n_far, lambda kb, cr: group_fn(kb, 1, cr), carry)
    return special_fn(carry)


def _prenorm_kernel(x_ref, meta_ref, w_ref, o_ref):
    def norm(x):
        ms = jnp.mean(x * x, axis=-1, keepdims=True)
        return (x * lax.rsqrt(ms + EPS) * w_ref[...]).astype(o_ref.dtype)

    is_meta = lax.rem(pl.program_id(0), NKB) == 0

    @pl.when(is_meta)
    def _():
        o_ref[...] = norm(meta_ref[...])

    @pl.when(jnp.logical_not(is_meta))
    def _():
        o_ref[...] = norm(x_ref[...])


def _prenorm(x2, meta_block, w):
    frame_block = lambda r: (jnp.maximum(r - r // NKB - 1, 0), 0)
    return pl.pallas_call(
        _prenorm_kernel,
        grid=(ROWS // BLK,),
        in_specs=[pl.BlockSpec((BLK, D_MODEL), frame_block),
                  pl.BlockSpec((BLK, D_MODEL), lambda r: (0, 0)),
                  pl.BlockSpec((1, D_MODEL), lambda r: (0, 0))],
        out_specs=pl.BlockSpec((BLK, D_MODEL), lambda r: (r, 0)),
        out_shape=jax.ShapeDtypeStruct((ROWS, D_MODEL), BF16),
        compiler_params=pltpu.CompilerParams(dimension_semantics=("arbitrary",)),
        name="prenorm",
    )(x2, meta_block, w)


def _mm_kernel(a_ref, w_ref, o_ref):
    o_ref[...] = jnp.dot(a_ref[...], w_ref[...], preferred_element_type=F32).astype(o_ref.dtype)


def _mm_scaled_kernel(a_ref, w_ref, cs_ref, o_ref):
    acc = jnp.dot(a_ref[...], w_ref[...], preferred_element_type=F32)
    o_ref[...] = (acc * cs_ref[...]).astype(o_ref.dtype)


def _matmul(a, w, col0, n, out_dtype, tm, tn, name, col_scale=None):
    m, k = a.shape
    assert col0 % tn == 0 and n % tn == 0 and m % tm == 0
    c0 = col0 // tn
    in_specs = [pl.BlockSpec((tm, k), lambda i, j: (i, 0)),
                pl.BlockSpec((k, tn), lambda i, j: (0, c0 + j))]
    args = (a, w)
    if col_scale is not None:
        in_specs.append(pl.BlockSpec((1, tn), lambda i, j: (0, j)))
        args += (col_scale,)
    return pl.pallas_call(
        _mm_kernel if col_scale is None else _mm_scaled_kernel,
        grid=(m // tm, n // tn),
        in_specs=in_specs,
        out_specs=pl.BlockSpec((tm, tn), lambda i, j: (i, j)),
        out_shape=jax.ShapeDtypeStruct((m, n), out_dtype),
        compiler_params=pltpu.CompilerParams(
            dimension_semantics=("arbitrary", "arbitrary"), vmem_limit_bytes=VMEM_LIMIT),
        name=name,
    )(*args)


def _kvprep_kernel(c_ref, v_ref, kvw_ref, ikw_ref, ikb_ref, ckv_ref, ckvt_ref, ik_ref, iwt_ref, vt_ref):
    for blk in range(2):
        for h in range(B_HEADS):
            vh = v_ref[blk * BLK:(blk + 1) * BLK, h * B_V_DIM:(h + 1) * B_V_DIM]
            vt_ref[blk, h] = vh.astype(F32).T.astype(BF16)
    c = c_ref[...]
    ckv = c[:, :KV_RANK]
    ms = jnp.mean(ckv * ckv, axis=-1, keepdims=True)
    ckvn = ckv * lax.rsqrt(ms + EPS) * kvw_ref[...]
    ckv_ref[...] = ckvn.astype(BF16)
    ckvt_ref[0] = ckvn[:BLK].T.astype(BF16)
    ckvt_ref[1] = ckvn[BLK:].T.astype(BF16)
    ik = c[:, KV_RANK:KV_RANK + IDX_DIM]
    mu = jnp.mean(ik, axis=-1, keepdims=True)
    var = jnp.mean(jnp.square(ik - mu), axis=-1, keepdims=True)
    ikn = (ik - mu) * lax.rsqrt(var + EPS) * ikw_ref[...] + ikb_ref[...]
    ik_ref[...] = ikn.astype(BF16)
    tail = c[:, KV_RANK:KV_RANK + BLK] * (IDX_HEADS ** -0.5 * IDX_DIM ** -0.5)
    iwt_ref[...] = tail.T[IDX_DIM:IDX_DIM + IDX_HEADS, :]


def _kvprep(c, kv_b, vcol, kvw, ikw, ikb):
    tm = 2 * BLK
    assert vcol % B_WIDTH == 0 and c.shape[1] == KV_RANK + BLK
    return pl.pallas_call(
        _kvprep_kernel,
        grid=(ROWS // tm,),
        in_specs=[pl.BlockSpec((tm, KV_RANK + BLK), lambda i: (i, 0)),
                  pl.BlockSpec((tm, B_WIDTH), lambda i: (i, vcol // B_WIDTH)),
                  pl.BlockSpec((1, KV_RANK), lambda i: (0, 0)),
                  pl.BlockSpec((1, IDX_DIM), lambda i: (0, 0)),
                  pl.BlockSpec((1, IDX_DIM), lambda i: (0, 0))],
        out_specs=[pl.BlockSpec((tm, KV_RANK), lambda i: (i, 0)),
                   pl.BlockSpec((2, KV_RANK, BLK), lambda i: (i, 0, 0)),
                   pl.BlockSpec((tm, IDX_DIM), lambda i: (i, 0)),
                   pl.BlockSpec((IDX_HEADS, tm), lambda i: (0, i)),
                   pl.BlockSpec((2, B_HEADS, B_V_DIM, BLK), lambda i: (i, 0, 0, 0))],
        out_shape=[jax.ShapeDtypeStruct((ROWS, KV_RANK), BF16),
                   jax.ShapeDtypeStruct((ROWS // BLK, KV_RANK, BLK), BF16),
                   jax.ShapeDtypeStruct((ROWS, IDX_DIM), BF16),
                   jax.ShapeDtypeStruct((IDX_HEADS, ROWS), F32),
                   jax.ShapeDtypeStruct((ROWS // BLK, B_HEADS, B_V_DIM, BLK), BF16)],
        name="kvprep",
    )(c, kv_b, kvw, ikw, ikb)


def _online_softmax_step(s, m, l):
    m_new = jnp.maximum(m, jnp.max(s, axis=0, keepdims=True))
    alpha = jnp.exp2(m - m_new)
    p = jnp.exp2(s - m_new)
    return m_new, alpha * l + jnp.sum(p, axis=0, keepdims=True), alpha, p


def _attn_a_kernel(qa_ref, iq_ref, iwt_ref, ckv_ref, ckvt_ref, ik_ref, wuk_ref, wuvt_ref, bias_ref,
                   o_ref, keys_ref, qlat_ref, acc_ref, iqt_ref):
    i = pl.program_id(1)
    nkb = i + 2
    NG = A_HEADS // 2
    GW = 2 * BLK
    t_meta = jnp.where(i == 0, T_META0, T_METAFAR)
    t_prev = jnp.where(i == 0, T_NONE, T_PREV)
    special_blocks = (0, i, i + 1)

    for h in range(A_HEADS):
        qh = qa_ref[:, h * BLK:(h + 1) * BLK]
        ql = lax.dot_general(wuk_ref[h], qh, NT_DIMS, preferred_element_type=F32)
        qlat_ref[:, h * BLK:(h + 1) * BLK] = (ql * (A_HEAD_DIM ** -0.5 * LOG2E)).astype(BF16)

    for pr in range(IDX_HEADS // 2):
        t = iq_ref[:, pr * BLK:(pr + 1) * BLK].astype(F32).T
        iqt_ref[:, (2 * pr) * BLK:(2 * pr + 1) * BLK] = t[:IDX_DIM].astype(BF16)
        iqt_ref[:, (2 * pr + 1) * BLK:(2 * pr + 2) * BLK] = t[IDX_DIM:].astype(BF16)

    iwt = iwt_ref[...]
    row = lax.broadcasted_iota(jnp.int32, (BLK, BLK), 0)
    lane = lax.broadcasted_iota(jnp.int32, (BLK, BLK), 1)

    def idx_keys(ikrows):
        sc = jnp.zeros((ikrows.shape[0], BLK), F32)
        for pr in range(IDX_HEADS // 2):
            s2 = jnp.dot(ikrows, iqt_ref[:, pr * 2 * BLK:(pr + 1) * 2 * BLK], preferred_element_type=F32)
            sc = sc + jnp.maximum(s2[:, :BLK], 0.0) * iwt[2 * pr:2 * pr + 1, :]
            sc = sc + jnp.maximum(s2[:, BLK:], 0.0) * iwt[2 * pr + 1:2 * pr + 2, :]
        bits = lax.bitcast_convert_type(sc, jnp.int32)
        return bits ^ ((bits >> 31) & 0x7FFFFFFF)

    def put_keys(off, n, key):
        keys_ref[pl.ds(off, n * BLK), :] = key

    def idx_group(kb, n, carry):
        off = pl.multiple_of(kb * BLK, BLK)
        put_keys(off, n, idx_keys(ik_ref[pl.ds(off, n * BLK), :]))
        return carry

    def idx_special(carry):
        offs = [pl.multiple_of(kb * BLK, BLK) for kb in special_blocks]
        key = idx_keys(jnp.concatenate([ik_ref[pl.ds(o, BLK), :] for o in offs], axis=0))
        put_keys(offs[1], 1, key[BLK:2 * BLK])
        put_keys(offs[2], 1, jnp.where((row >= CHUNK) & (lane < CHUNK), INT_MIN, key[2 * BLK:]))
        put_keys(offs[0], 1, jnp.where(row >= N_META, INT_MIN, key[:BLK]))
        return carry

    _visit_key_blocks(i, idx_group, idx_special, 0)
    put_keys(pl.multiple_of(nkb * BLK, BLK), FAR - 1, jnp.full(((FAR - 1) * BLK, BLK), INT_MIN, jnp.int32))
    n_search = lax.shift_right_logical(nkb + FAR - 1, 2)
    crow = lax.broadcasted_iota(jnp.int32, (FAR * BLK, BLK), 0)
    chunk_off = lambda c: pl.multiple_of(c * FAR * BLK, FAR * BLK)

    def count(pred_fn):
        def body(c, acc8):
            k = keys_ref[pl.ds(chunk_off(c), FAR * BLK), :]
            hit = pred_fn(k, chunk_off(c)).astype(jnp.int32)
            return acc8 + jnp.sum(hit.reshape(FAR * BLK // 8, 8, BLK), axis=0)
        acc8 = lax.fori_loop(0, n_search, body, jnp.zeros((8, BLK), jnp.int32))
        return jnp.sum(acc8, axis=0, keepdims=True)

    zero = jnp.zeros((1, BLK), jnp.int32)
    c0 = count(lambda k, off: k >= zero)
    prefix = jnp.where(c0 >= TOPK, 0, INT_MIN).astype(jnp.int32)

    def bit_body(t, prefix):
        cand = prefix | jnp.left_shift(jnp.int32(1), 30 - t)
        c = count(lambda k, off: k >= cand)
        return jnp.where(c >= TOPK, cand, prefix)

    thr = lax.fori_loop(0, 31, bit_body, prefix)
    full = thr == INT_MIN

    c_gt = count(lambda k, off: k > thr)
    c_eq = count(lambda k, off: k == thr)
    need = TOPK - c_gt
    tied = jnp.logical_and(jnp.logical_not(full), c_eq > need)
    j_default = jnp.where(full, -1, TP).astype(jnp.int32)

    def tie_search():
        def jbit(t, j):
            cand = j | jnp.left_shift(jnp.int32(1), 12 - t)
            c = count(lambda k, off: (k == thr) & ((off + crow) < cand))
            return jnp.where(c < need, cand, j)
        j = lax.fori_loop(0, 13, jbit, jnp.zeros((1, BLK), jnp.int32))
        return jnp.where(tied, j, j_default)

    any_tied = jnp.max(tied.astype(jnp.int32)) > 0
    jmax = lax.cond(any_tied, tie_search, lambda: j_default)

    acc_ref[...] = jnp.zeros_like(acc_ref)

    def att_update(rows, keys, rowidx, vt, bias, carry):
        sel = (keys > thr) | ((keys == thr) & (rowidx <= jmax))
        selb = jnp.where(sel, 0.0, NEG)
        add = jnp.concatenate([selb, selb], axis=1)
        gs = [slice(g * GW, (g + 1) * GW) for g in range(NG)]
        ss = [jnp.dot(rows, qlat_ref[:, gs[g]], preferred_element_type=F32) for g in range(NG)]
        ss = [ss[g] + (add if bias is None else add + bias[:, gs[g]]) for g in range(NG)]
        steps = [_online_softmax_step(ss[g], *carry[g]) for g in range(NG)]
        pvs = [jnp.dot(vt, steps[g][3].astype(BF16), preferred_element_type=F32) for g in range(NG)]
        for g in range(NG):
            acc_ref[:, gs[g]] = acc_ref[:, gs[g]] * steps[g][2] + pvs[g]
        return tuple((steps[g][0], steps[g][1]) for g in range(NG))

    def att_group(kb, n, carry):
        off = pl.multiple_of(kb * BLK, BLK)
        rowidx = off + lax.broadcasted_iota(jnp.int32, (n * BLK, BLK), 0)
        vt = jnp.concatenate([ckvt_ref[kb + u] for u in range(n)], axis=1) if n > 1 else ckvt_ref[kb]
        return att_update(ckv_ref[pl.ds(off, n * BLK), :], keys_ref[pl.ds(off, n * BLK), :], rowidx, vt, None, carry)

    def att_special(carry):
        offs = [pl.multiple_of(kb * BLK, BLK) for kb in special_blocks]
        rows = jnp.concatenate([ckv_ref[pl.ds(o, BLK), :] for o in offs], axis=0)
        keys = jnp.concatenate([keys_ref[pl.ds(o, BLK), :] for o in offs], axis=0)
        rowidx = jnp.concatenate([o + row for o in offs], axis=0)
        vt = jnp.concatenate([ckvt_ref[kb] for kb in special_blocks], axis=1)
        bias = jnp.concatenate([bias_ref[t_meta], bias_ref[t_prev], bias_ref[T_DIAG]], axis=0)
        return att_update(rows, keys, rowidx, vt, bias, carry)

    m0 = jnp.full((1, GW), NEG, F32)
    l0 = jnp.zeros((1, GW), F32)
    stats = _visit_key_blocks(i, att_group, att_special, tuple((m0, l0) for _ in range(NG)))
    l = jnp.concatenate([stats[g][1] for g in range(NG)], axis=1)

    olat = (acc_ref[...] / l).astype(BF16)
    for h in range(A_HEADS):
        ot = jnp.dot(wuvt_ref[h], olat[:, h * BLK:(h + 1) * BLK], preferred_element_type=F32)
        o_ref[:, h * BLK:(h + 1) * BLK] = ot.T


def _attn_a(proj_a, iwt, ckv, ckvt, ik, wuk, wuvt, bias_a):
    qrow = lambda b, i: b * NKB + 1 + i
    return pl.pallas_call(
        _attn_a_kernel,
        grid=(BATCH, NQB),
        in_specs=[
            pl.BlockSpec((BLK, A_WIDTH), lambda b, i: (qrow(b, i), 0)),
            pl.BlockSpec((BLK, IDX_HEADS * IDX_DIM), lambda b, i: (qrow(b, i), 1)),
            pl.BlockSpec((IDX_HEADS, BLK), lambda b, i: (0, qrow(b, i))),
            pl.BlockSpec((None, TP, KV_RANK), lambda b, i: (b, 0, 0)),
            pl.BlockSpec((None, NKB, KV_RANK, BLK), lambda b, i: (b, 0, 0, 0)),
            pl.BlockSpec((None, TP, IDX_DIM), lambda b, i: (b, 0, 0)),
            pl.BlockSpec((A_HEADS, KV_RANK, A_HEAD_DIM), lambda b, i: (0, 0, 0)),
            pl.BlockSpec((A_HEADS, A_HEAD_DIM, KV_RANK), lambda b, i: (0, 0, 0)),
            pl.BlockSpec((5, BLK, A_HEADS * BLK), lambda b, i: (0, 0, 0)),
        ],
        out_specs=pl.BlockSpec((BLK, A_WIDTH), lambda b, i: (b * NQB + i, 0)),
        out_shape=jax.ShapeDtypeStruct((BATCH * SEQ, A_WIDTH), F32),
        scratch_shapes=[pltpu.VMEM(((NKB + FAR - 1) * BLK, BLK), jnp.int32),
                        pltpu.VMEM((KV_RANK, A_HEADS * BLK), BF16),
                        pltpu.VMEM((KV_RANK, A_HEADS * BLK), F32),
                        pltpu.VMEM((IDX_DIM, IDX_HEADS * BLK), BF16)],
        compiler_params=pltpu.CompilerParams(
            dimension_semantics=("arbitrary", "arbitrary"), vmem_limit_bytes=VMEM_LIMIT),
        name="attn_a",
    )(proj_a, proj_a, iwt, ckv.reshape(BATCH, TP, KV_RANK), ckvt.reshape(BATCH, NKB, KV_RANK, BLK),
      ik.reshape(BATCH, TP, IDX_DIM), wuk, wuvt, bias_a)


def _attn_b_kernel(lam_ref, q_ref, k_ref, vt_ref, bias_ref, subw_ref, o_ref, acc_ref, *, lam_init):
    i = pl.program_id(2)
    t_meta = jnp.where(i == 0, T_META0, T_METAFAR)
    t_prev = jnp.where(i == 0, T_NONE, T_PREV)
    special_blocks = (0, i, i + 1)
    lp = lam_ref[...]
    lam = (jnp.exp(jnp.sum(lp[0:1] * lp[1:2], axis=-1, keepdims=True))
           - jnp.exp(jnp.sum(lp[2:3] * lp[3:4], axis=-1, keepdims=True)) + lam_init)

    lane = lax.broadcasted_iota(jnp.int32, (BLK, BLK), 1)
    qbd = []
    for hh in range(HPS):
        q = q_ref[:, hh * BLK:(hh + 1) * BLK]
        zq = jnp.zeros_like(q)
        qbd.append(jnp.concatenate([jnp.where(lane < B_QK_DIM, q, zq), jnp.where(lane >= B_QK_DIM, q, zq)], axis=0))

    acc_ref[...] = jnp.zeros_like(acc_ref)

    def update_all(rows, vts, biases, carry):
        ss = [lax.dot_general(rows[hh], qbd[hh], NT_DIMS, preferred_element_type=F32) for hh in range(HPS)]
        if biases is not None:
            ss = [ss[hh] + jnp.concatenate([biases[hh], biases[hh]], axis=1) for hh in range(HPS)]
        steps = [_online_softmax_step(ss[hh], *carry[hh]) for hh in range(HPS)]
        pvs = [jnp.dot(vts[hh], steps[hh][3].astype(BF16), preferred_element_type=F32) for hh in range(HPS)]
        for hh in range(HPS):
            acc_ref[hh] = acc_ref[hh] * steps[hh][2] + pvs[hh]
        return tuple((steps[hh][0], steps[hh][1]) for hh in range(HPS))

    def group(kb, n, carry):
        off = pl.multiple_of(kb * BLK, BLK)
        rows = [k_ref[pl.ds(off, n * BLK), hh * BLK:(hh + 1) * BLK] for hh in range(HPS)]
        vts = [jnp.concatenate([vt_ref[kb + u, hh] for u in range(n)], axis=1) if n > 1 else vt_ref[kb, hh]
               for hh in range(HPS)]
        return update_all(rows, vts, None, carry)

    def special(carry):
        offs = [pl.multiple_of(kb * BLK, BLK) for kb in special_blocks]
        rows = [jnp.concatenate([k_ref[pl.ds(o, BLK), hh * BLK:(hh + 1) * BLK] for o in offs], axis=0)
                for hh in range(HPS)]
        vts = [jnp.concatenate([vt_ref[kb, hh] for kb in special_blocks], axis=1) for hh in range(HPS)]
        biases = [jnp.concatenate([bias_ref[t_meta, hh], bias_ref[t_prev, hh], bias_ref[T_DIAG, hh]], axis=0)
                  for hh in range(HPS)]
        return update_all(rows, vts, biases, carry)

    m0 = jnp.full((1, 2 * BLK), NEG, F32)
    l0 = jnp.zeros((1, 2 * BLK), F32)
    stats = _visit_key_blocks(i, group, special, tuple((m0, l0) for _ in range(HPS)))

    for hh in range(HPS):
        a = acc_ref[hh] / stats[hh][1]
        o = a[:, :BLK] - lam * a[:, BLK:]
        ms = jnp.mean(o * o, axis=0, keepdims=True)
        y = o * lax.rsqrt(ms + EPS) * subw_ref[...] * (1.0 - lam_init)
        o_ref[:, hh * BLK:(hh + 1) * BLK] = y.T


def _attn_b(qkv_b, vt, bias_b, lam_p, subw, lam_init):
    qrow = lambda b, g, i: b * NKB + 1 + i
    wide = HPS * BLK
    qcol0 = 0
    kcol0 = 2 * B_HEADS * B_QK_DIM // wide
    return pl.pallas_call(
        functools.partial(_attn_b_kernel, lam_init=lam_init),
        grid=(BATCH, B_HEADS // HPS, NQB),
        in_specs=[
            pl.BlockSpec((4, B_QK_DIM), lambda b, g, i: (0, 0)),
            pl.BlockSpec((BLK, wide), lambda b, g, i: (qrow(b, g, i), qcol0 + g)),
            pl.BlockSpec((None, TP, wide), lambda b, g, i: (b, 0, kcol0 + g)),
            pl.BlockSpec((None, NKB, HPS, B_V_DIM, BLK), lambda b, g, i: (b, 0, g, 0, 0)),
            pl.BlockSpec((5, HPS, BLK, BLK), lambda b, g, i: (0, g, 0, 0)),
            pl.BlockSpec((B_V_DIM, BLK), lambda b, g, i: (0, 0)),
        ],
        out_specs=pl.BlockSpec((BLK, wide), lambda b, g, i: (b * NQB + i, g)),
        out_shape=jax.ShapeDtypeStruct((BATCH * SEQ, B_WIDTH), F32),
        scratch_shapes=[pltpu.VMEM((HPS, B_V_DIM, 2 * BLK), F32)],
        compiler_params=pltpu.CompilerParams(
            dimension_semantics=("arbitrary", "arbitrary", "arbitrary"), vmem_limit_bytes=VMEM_LIMIT),
        name="attn_b",
    )(lam_p, qkv_b, qkv_b.reshape(BATCH, TP, -1), vt, bias_b, subw)


def _out_kernel(oa_ref, za_ref, ob_ref, zb_ref, ga_ref, gb_ref, x_ref, woa_ref, wob_ref, wout_ref, pw_ref, o_ref):
    a = (oa_ref[...] * jax.nn.silu(za_ref[...])).astype(BF16)
    ya = jnp.dot(a, woa_ref[...], preferred_element_type=F32)
    b = (ob_ref[...] * jax.nn.silu(zb_ref[...])).astype(BF16)
    yb = jnp.dot(b, wob_ref[...], preferred_element_type=F32)
    mix = jax.nn.sigmoid(ga_ref[...]) * ya + jax.nn.sigmoid(gb_ref[...]) * yb
    out = jnp.dot(mix.astype(BF16), wout_ref[...], preferred_element_type=F32)
    ms = jnp.mean(out * out, axis=-1, keepdims=True)
    o_ref[...] = x_ref[...] + out * lax.rsqrt(ms + EPS) * pw_ref[...]


def _out_stage(o_a, o_b, z_a, z_b, gates, x2, woa, wob, wout, pw):
    tm = BLK
    prow = lambda g: g + g // NQB + 1
    const = lambda g: (0, 0)
    return pl.pallas_call(
        _out_kernel,
        grid=(BATCH * NQB,),
        in_specs=[
            pl.BlockSpec((tm, A_WIDTH), lambda g: (g, 0)),
            pl.BlockSpec((tm, A_WIDTH), lambda g: (prow(g), 0)),
            pl.BlockSpec((tm, B_WIDTH), lambda g: (g, 0)),
            pl.BlockSpec((tm, B_WIDTH), lambda g: (prow(g), 0)),
            pl.BlockSpec((tm, D_MODEL), lambda g: (prow(g), 0)),
            pl.BlockSpec((tm, D_MODEL), lambda g: (prow(g), 1)),
            pl.BlockSpec((tm, D_MODEL), lambda g: (g, 0)),
            pl.BlockSpec((A_WIDTH, D_MODEL), const, pipeline_mode=pl.Buffered(1)),
            pl.BlockSpec((B_WIDTH, D_MODEL), const, pipeline_mode=pl.Buffered(1)),
            pl.BlockSpec((D_MODEL, D_MODEL), const, pipeline_mode=pl.Buffered(1)),
            pl.BlockSpec((1, D_MODEL), const),
        ],
        out_specs=pl.BlockSpec((tm, D_MODEL), lambda g: (g, 0)),
        out_shape=jax.ShapeDtypeStruct((BATCH * SEQ, D_MODEL), F32),
        compiler_params=pltpu.CompilerParams(
            dimension_semantics=("arbitrary",), vmem_limit_bytes=VMEM_LIMIT),
        name="out_stage",
    )(o_a, z_a, o_b, z_b, gates, gates, x2, woa, wob, wout, pw)


def kernel(x, meta_tokens, rel_bias, pre_norm_w, w_in, kv_norm_w, w_uk, w_uv, idx_k_norm_w, idx_k_norm_b,
           diff_lambda, diff_subln_w, w_o_a, w_o_b, w_out, post_norm_w):
    assert x.shape == (BATCH, SEQ, D_MODEL) and w_in.shape[0] == 1
    layer = 0
    lam_init = 0.8 - 0.6 * math.exp(-0.3 * layer)

    meta_block = jnp.concatenate([meta_tokens.astype(F32), jnp.zeros((BLK - N_META, D_MODEL), F32)], axis=0)

    offs = np.concatenate([[0], np.cumsum(IN_SIZES)])
    w = w_in[0]
    seg = lambda k: w[:, offs[k]:offs[k + 1]]
    tail = w[:, offs[4]:offs[4] + BLK]
    w_head = jnp.concatenate([seg(0), seg(3), seg(2), seg(1), tail], axis=1).astype(BF16)
    w_rest = w[:, offs[6]:].astype(BF16)
    wd = A_WIDTH
    assert all(IN_SIZES[k] == wd for k in (0, 2, 3, 6, 7, 8, 9)) and IN_SIZES[10] == IN_SIZES[11] == 2 * wd
    scale_qb = jnp.concatenate([jnp.full((1, wd), B_QK_DIM ** -0.5 * LOG2E, F32), jnp.ones((1, 2 * wd), F32)], axis=1)

    u = _prenorm(x.reshape(BATCH * SEQ, D_MODEL), meta_block, pre_norm_w[0][None].astype(F32))
    tm = ROWS // 8
    q_iq = _matmul(u, w_head, 0, 2 * wd, BF16, tm, wd, "proj_q_iq")
    z_a = _matmul(u, w_head, 2 * wd, wd, F32, tm, wd, "proj_z_a")
    lat = _matmul(u, w_head, 3 * wd, KV_RANK + BLK, F32, tm, KV_RANK + BLK, "proj_latent")
    qkv_b = _matmul(u, w_rest, 0, 3 * wd, BF16, tm, wd, "proj_qkv_b", col_scale=scale_qb)
    z_b = _matmul(u, w_rest, 3 * wd, wd, F32, tm, wd, "proj_z_b")
    gates = _matmul(u, w_rest, 4 * wd, 4 * wd, F32, tm, wd, "proj_gates")

    ckv, ckvt, ik, iwt, vt = _kvprep(lat, qkv_b, 2 * wd, kv_norm_w[0][None].astype(F32),
                                     idx_k_norm_w[0][None].astype(F32), idx_k_norm_b[0][None].astype(F32))

    bias = _bias_tiles(rel_bias) * LOG2E
    bias_a = jnp.transpose(bias[:, :A_HEADS], (0, 2, 1, 3)).reshape(5, BLK, A_HEADS * BLK)
    bias_b = bias[:, A_HEADS:]

    wuk = jnp.transpose(w_uk[0], (1, 0, 2)).astype(BF16)
    wuvt = jnp.transpose(w_uv[0], (1, 2, 0)).astype(BF16)
    o_a = _attn_a(q_iq, iwt, ckv, ckvt, ik, wuk, wuvt, bias_a)

    vt = vt.reshape(BATCH, NKB, B_HEADS, B_V_DIM, BLK)
    subw = jnp.broadcast_to(diff_subln_w[0].astype(F32)[:, None], (B_V_DIM, BLK))
    o_b = _attn_b(qkv_b, vt, bias_b, diff_lambda[0].astype(F32), subw, lam_init)

    out = _out_stage(o_a, o_b, z_a, z_b, gates, x.reshape(BATCH * SEQ, D_MODEL),
                     w_o_a[0].astype(BF16), w_o_b[0].astype(BF16), w_out[0].astype(BF16),
                     post_norm_w[0][None].astype(F32))
    return out.reshape(BATCH, SEQ, D_MODEL)
```

```python
import functools
import math

import numpy as np
import jax
import jax.numpy as jnp
from jax import lax
from jax.experimental import pallas as pl
from jax.experimental.pallas import tpu as pltpu

D_MODEL = 2048
BATCH = 2
SEQ = 4096
CHUNK = 64
N_META = 16
N_BUCKETS = 32
MAX_DISTANCE = 128
A_HEADS = 8
A_HEAD_DIM = 128
KV_RANK = 256
IDX_HEADS = 16
IDX_DIM = 64
TOPK = 256
B_HEADS = 8
B_QK_DIM = 64
B_V_DIM = 128
A_WIDTH = A_HEADS * A_HEAD_DIM
B_WIDTH = B_HEADS * B_V_DIM
IN_SIZES = (A_WIDTH, KV_RANK, A_WIDTH, IDX_HEADS * IDX_DIM, IDX_DIM, IDX_HEADS,
            2 * B_HEADS * B_QK_DIM, 2 * B_HEADS * B_QK_DIM, B_WIDTH, B_WIDTH,
            D_MODEL, D_MODEL)
EPS = 1e-6

BLK = 128
NQB = SEQ // BLK
NKB = NQB + 1
TP = NKB * BLK
ROWS = BATCH * TP
FAR = 4
HPS = 8
NEG = -1e30
INT_MIN = -2 ** 31
LOG2E = math.log2(math.e)
VMEM_LIMIT = 56 * 1024 * 1024

F32 = jnp.float32
BF16 = jnp.bfloat16
NT_DIMS = (((1,), (1,)), ((), ()))


def _t5_bucket_np(rel):
    nb = N_BUCKETS // 2
    max_exact = nb // 2
    ret = np.where(rel > 0, nb, 0)
    n = np.abs(rel)
    nf = np.maximum(n, 1).astype(np.float32)
    large = max_exact + (np.log(nf / np.float32(max_exact))
                         / np.float32(math.log(MAX_DISTANCE / max_exact))
                         * np.float32(nb - max_exact)).astype(np.int32)
    large = np.minimum(large, nb - 1)
    return ret + np.where(n < max_exact, n, large)


T_DIAG, T_PREV, T_META0, T_METAFAR, T_NONE = range(5)


def _bias_tiles(rel_bias):
    a = np.arange(BLK)[:, None]
    b = np.arange(BLK)[None, :]
    nowhere = np.zeros((BLK, BLK), bool)
    pad_rows = (a >= N_META) | nowhere
    rels = np.stack([a - b, a - b - BLK, a - N_META - b])
    dis = np.stack([(a >= CHUNK) & (b < CHUNK), nowhere, pad_rows, pad_rows, ~nowhere])
    idx = _t5_bucket_np(rels)
    far_bucket = N_BUCKETS // 2 - 1
    assert _t5_bucket_np(np.array([-BLK - 1]))[0] == far_bucket
    rb = rel_bias.astype(F32)
    heads = A_HEADS + B_HEADS
    tiles = jnp.zeros((3, heads, BLK, BLK), F32)
    for k in np.unique(idx):
        tiles = jnp.where((idx == k)[:, None], rb[k][None, :, None, None], tiles)
    tiles = tiles - rb[far_bucket][None, :, None, None]
    tiles = jnp.concatenate([tiles, jnp.zeros((2, heads, BLK, BLK), F32)], axis=0)
    return jnp.where(dis[:, None], NEG, tiles)


def _far_split(i):
    n_far = jnp.maximum(i - 1, 0)
    return n_far, lax.shift_right_logical(n_far, 2)


def _visit_key_blocks(i, group_fn, special_fn, carry):
    n_far, n_chunks = _far_split(i)
    carry = lax.fori_loop(0, n_chunks, lambda c, cr: group_fn(1 + FAR * c, FAR, cr), carry)
    carry = lax.fori_loop(1 + FAR * n_chunks, 1 + n_far, lambda kb, cr: group_fn(kb, 1, cr), carry)
    return special_fn(carry)


def _prenorm_kernel(x_ref, meta_ref, w_ref, o_ref, of_ref):
    def norm(x):
        ms = jnp.mean(x * x, axis=-1, keepdims=True)
        return (x * lax.rsqrt(ms + EPS) * w_ref[...]).astype(o_ref.dtype)

    is_meta = lax.rem(pl.program_id(0), NKB) == 0

    @pl.when(is_meta)
    def _():
        o_ref[...] = norm(meta_ref[...])

    @pl.when(jnp.logical_not(is_meta))
    def _():
        u = norm(x_ref[...])
        o_ref[...] = u
        of_ref[...] = u


def _prenorm(x2, meta_block, w):
    frame_block = lambda r: (jnp.maximum(r - r // NKB - 1, 0), 0)
    return pl.pallas_call(
        _prenorm_kernel,
        grid=(ROWS // BLK,),
        in_specs=[pl.BlockSpec((BLK, D_MODEL), frame_block),
                  pl.BlockSpec((BLK, D_MODEL), lambda r: (0, 0)),
                  pl.BlockSpec((1, D_MODEL), lambda r: (0, 0))],
        out_specs=[pl.BlockSpec((BLK, D_MODEL), lambda r: (r, 0)),
                   pl.BlockSpec((BLK, D_MODEL), frame_block)],
        out_shape=[jax.ShapeDtypeStruct((ROWS, D_MODEL), BF16),
                   jax.ShapeDtypeStruct((BATCH * SEQ, D_MODEL), BF16)],
        compiler_params=pltpu.CompilerParams(dimension_semantics=("arbitrary",)),
        name="prenorm",
    )(x2, meta_block, w)


IN_OFFS = tuple(int(v) for v in np.concatenate([[0], np.cumsum(IN_SIZES)]))
W_HEAD = 3 * A_WIDTH + KV_RANK + BLK
W_REST = IN_OFFS[-1] - IN_OFFS[6]


def _wprep_kernel(w_ref, head_ref, rest_ref):
    seg = lambda k: w_ref[:, IN_OFFS[k]:IN_OFFS[k + 1]]
    tail = w_ref[:, IN_OFFS[4]:IN_OFFS[4] + BLK]
    head_ref[...] = jnp.concatenate([seg(0), seg(3), seg(2), seg(1), tail], axis=1).astype(BF16)
    rest_ref[...] = w_ref[:, IN_OFFS[6]:].astype(BF16)


def _wprep(w):
    tr = BLK
    k, n = w.shape
    assert all(IN_OFFS[j] % BLK == 0 for j in range(5)) and W_REST % BLK == 0
    return pl.pallas_call(
        _wprep_kernel,
        grid=(k // tr,),
        in_specs=[pl.BlockSpec((tr, n), lambda r: (r, 0))],
        out_specs=[pl.BlockSpec((tr, W_HEAD), lambda r: (r, 0)),
                   pl.BlockSpec((tr, W_REST), lambda r: (r, 0))],
        out_shape=[jax.ShapeDtypeStruct((k, W_HEAD), BF16), jax.ShapeDtypeStruct((k, W_REST), BF16)],
        compiler_params=pltpu.CompilerParams(dimension_semantics=("arbitrary",), vmem_limit_bytes=VMEM_LIMIT),
        name="wprep",
    )(w)


def _mm_kernel(a_ref, w_ref, o_ref):
    o_ref[...] = jnp.dot(a_ref[...], w_ref[...], preferred_element_type=F32).astype(o_ref.dtype)


def _mm_scaled_kernel(a_ref, w_ref, cs_ref, o_ref):
    acc = jnp.dot(a_ref[...], w_ref[...], preferred_element_type=F32)
    o_ref[...] = (acc * cs_ref[...]).astype(o_ref.dtype)


def _matmul(a, w, col0, n, out_dtype, tm, tn, name, col_scale=None):
    m, k = a.shape
    assert col0 % tn == 0 and n % tn == 0 and m % tm == 0
    c0 = col0 // tn
    in_specs = [pl.BlockSpec((tm, k), lambda i, j: (i, 0)),
                pl.BlockSpec((k, tn), lambda i, j: (0, c0 + j))]
    args = (a, w)
    if col_scale is not None:
        in_specs.append(pl.BlockSpec((1, tn), lambda i, j: (0, j)))
        args += (col_scale,)
    return pl.pallas_call(
        _mm_kernel if col_scale is None else _mm_scaled_kernel,
        grid=(m // tm, n // tn),
        in_specs=in_specs,
        out_specs=pl.BlockSpec((tm, tn), lambda i, j: (i, j)),
        out_shape=jax.ShapeDtypeStruct((m, n), out_dtype),
        compiler_params=pltpu.CompilerParams(
            dimension_semantics=("arbitrary", "arbitrary"), vmem_limit_bytes=VMEM_LIMIT),
        name=name,
    )(*args)


def _kvprep_kernel(c_ref, v_ref, kvw_ref, ikw_ref, ikb_ref, ckv_ref, ckvt_ref, ik_ref, iwt_ref, vt_ref):
    for blk in range(2):
        for h in range(B_HEADS):
            vh = v_ref[blk * BLK:(blk + 1) * BLK, h * B_V_DIM:(h + 1) * B_V_DIM]
            vt_ref[blk, h] = vh.astype(F32).T.astype(BF16)
    c = c_ref[...]
    ckv = c[:, :KV_RANK]
    ms = jnp.mean(ckv * ckv, axis=-1, keepdims=True)
    ckvn = ckv * lax.rsqrt(ms + EPS) * kvw_ref[...]
    ckv_ref[...] = ckvn.astype(BF16)
    ckvt_ref[0] = ckvn[:BLK].T.astype(BF16)
    ckvt_ref[1] = ckvn[BLK:].T.astype(BF16)
    ik = c[:, KV_RANK:KV_RANK + IDX_DIM]
    mu = jnp.mean(ik, axis=-1, keepdims=True)
    var = jnp.mean(jnp.square(ik - mu), axis=-1, keepdims=True)
    ikn = (ik - mu) * lax.rsqrt(var + EPS) * ikw_ref[...] + ikb_ref[...]
    ik_ref[...] = ikn.astype(BF16)
    tail = c[:, KV_RANK:KV_RANK + BLK] * (IDX_HEADS ** -0.5 * IDX_DIM ** -0.5)
    iwt_ref[...] = tail.T[IDX_DIM:IDX_DIM + IDX_HEADS, :]


def _kvprep(c, kv_b, vcol, kvw, ikw, ikb):
    tm = 2 * BLK
    assert vcol % B_WIDTH == 0 and c.shape[1] == KV_RANK + BLK
    return pl.pallas_call(
        _kvprep_kernel,
        grid=(ROWS // tm,),
        in_specs=[pl.BlockSpec((tm, KV_RANK + BLK), lambda i: (i, 0)),
                  pl.BlockSpec((tm, B_WIDTH), lambda i: (i, vcol // B_WIDTH)),
                  pl.BlockSpec((1, KV_RANK), lambda i: (0, 0)),
                  pl.BlockSpec((1, IDX_DIM), lambda i: (0, 0)),
                  pl.BlockSpec((1, IDX_DIM), lambda i: (0, 0))],
        out_specs=[pl.BlockSpec((tm, KV_RANK), lambda i: (i, 0)),
                   pl.BlockSpec((2, KV_RANK, BLK), lambda i: (i, 0, 0)),
                   pl.BlockSpec((tm, IDX_DIM), lambda i: (i, 0)),
                   pl.BlockSpec((IDX_HEADS, tm), lambda i: (0, i)),
                   pl.BlockSpec((2, B_HEADS, B_V_DIM, BLK), lambda i: (i, 0, 0, 0))],
        out_shape=[jax.ShapeDtypeStruct((ROWS, KV_RANK), BF16),
                   jax.ShapeDtypeStruct((ROWS // BLK, KV_RANK, BLK), BF16),
                   jax.ShapeDtypeStruct((ROWS, IDX_DIM), BF16),
                   jax.ShapeDtypeStruct((IDX_HEADS, ROWS), F32),
                   jax.ShapeDtypeStruct((ROWS // BLK, B_HEADS, B_V_DIM, BLK), BF16)],
        name="kvprep",
    )(c, kv_b, kvw, ikw, ikb)


def _online_softmax_step(s, m, l):
    m_new = jnp.maximum(m, jnp.max(s, axis=0, keepdims=True))
    alpha = jnp.exp2(m - m_new)
    p = jnp.exp2(s - m_new)
    return m_new, alpha * l + jnp.sum(p, axis=0, keepdims=True), alpha, p


def _attn_a_kernel(qa_ref, iq_ref, iwt_ref, ckv_ref, ckvt_ref, ik_ref, wuk_ref, wuvt_ref, bias_ref,
                   o_ref, keys_ref, qlat_ref, acc_ref, iqt_ref):
    i = pl.program_id(1)
    nkb = i + 2
    NG = A_HEADS // 2
    GW = 2 * BLK
    t_meta = jnp.where(i == 0, T_META0, T_METAFAR)
    t_prev = jnp.where(i == 0, T_NONE, T_PREV)
    special_blocks = (0, i, i + 1)

    for h in range(A_HEADS):
        qh = qa_ref[:, h * BLK:(h + 1) * BLK]
        ql = lax.dot_general(wuk_ref[h], qh, NT_DIMS, preferred_element_type=F32)
        qlat_ref[:, h * BLK:(h + 1) * BLK] = (ql * (A_HEAD_DIM ** -0.5 * LOG2E)).astype(BF16)

    for pr in range(IDX_HEADS // 2):
        t = iq_ref[:, pr * BLK:(pr + 1) * BLK].astype(F32).T
        iqt_ref[:, (2 * pr) * BLK:(2 * pr + 1) * BLK] = t[:IDX_DIM].astype(BF16)
        iqt_ref[:, (2 * pr + 1) * BLK:(2 * pr + 2) * BLK] = t[IDX_DIM:].astype(BF16)

    iwt = iwt_ref[...]
    row = lax.broadcasted_iota(jnp.int32, (BLK, BLK), 0)
    lane = lax.broadcasted_iota(jnp.int32, (BLK, BLK), 1)

    def idx_keys(ikrows):
        sc = jnp.zeros((ikrows.shape[0], BLK), F32)
        for pr in range(IDX_HEADS // 2):
            s2 = jnp.dot(ikrows, iqt_ref[:, pr * 2 * BLK:(pr + 1) * 2 * BLK], preferred_element_type=F32)
            sc = sc + jnp.maximum(s2[:, :BLK], 0.0) * iwt[2 * pr:2 * pr + 1, :]
            sc = sc + jnp.maximum(s2[:, BLK:], 0.0) * iwt[2 * pr + 1:2 * pr + 2, :]
        bits = lax.bitcast_convert_type(sc, jnp.int32)
        return bits ^ ((bits >> 31) & 0x7FFFFFFF)

    def put_keys(off, n, key):
        keys_ref[pl.ds(off, n * BLK), :] = key

    def idx_group(kb, n, carry):
        off = pl.multiple_of(kb * BLK, BLK)
        put_keys(off, n, idx_keys(ik_ref[pl.ds(off, n * BLK), :]))
        return carry

    def idx_special(carry):
        offs = [pl.multiple_of(kb * BLK, BLK) for kb in special_blocks]
        key = idx_keys(jnp.concatenate([ik_ref[pl.ds(o, BLK), :] for o in offs], axis=0))
        put_keys(offs[1], 1, key[BLK:2 * BLK])
        put_keys(offs[2], 1, jnp.where((row >= CHUNK) & (lane < CHUNK), INT_MIN, key[2 * BLK:]))
        put_keys(offs[0], 1, jnp.where(row >= N_META, INT_MIN, key[:BLK]))
        return carry

    _visit_key_blocks(i, idx_group, idx_special, 0)
    put_keys(pl.multiple_of(nkb * BLK, BLK), FAR - 1, jnp.full(((FAR - 1) * BLK, BLK), INT_MIN, jnp.int32))
    n_search = lax.shift_right_logical(nkb + FAR - 1, 2)
    crow = lax.broadcasted_iota(jnp.int32, (FAR * BLK, BLK), 0)
    chunk_off = lambda c: pl.multiple_of(c * FAR * BLK, FAR * BLK)

    def count(pred_fn):
        def body(c, acc8):
            k = keys_ref[pl.ds(chunk_off(c), FAR * BLK), :]
            hit = pred_fn(k, chunk_off(c)).astype(jnp.int32)
            return acc8 + jnp.sum(hit.reshape(FAR * BLK // 8, 8, BLK), axis=0)
        acc8 = lax.fori_loop(0, n_search, body, jnp.zeros((8, BLK), jnp.int32))
        return jnp.sum(acc8, axis=0, keepdims=True)

    zero = jnp.zeros((1, BLK), jnp.int32)
    c0 = count(lambda k, off: k >= zero)
    prefix = jnp.where(c0 >= TOPK, 0, INT_MIN).astype(jnp.int32)

    def bit_body(t, prefix):
        cand = prefix | jnp.left_shift(jnp.int32(1), 30 - t)
        c = count(lambda k, off: k >= cand)
        return jnp.where(c >= TOPK, cand, prefix)

    thr = lax.fori_loop(0, 31, bit_body, prefix)
    full = thr == INT_MIN

    c_gt = count(lambda k, off: k > thr)
    c_eq = count(lambda k, off: k == thr)
    need = TOPK - c_gt
    tied = jnp.logical_and(jnp.logical_not(full), c_eq > need)
    j_default = jnp.where(full, -1, TP).astype(jnp.int32)

    def tie_search():
        def jbit(t, j):
            cand = j | jnp.left_shift(jnp.int32(1), 12 - t)
            c = count(lambda k, off: (k == thr) & ((off + crow) < cand))
            return jnp.where(c < need, cand, j)
        j = lax.fori_loop(0, 13, jbit, jnp.zeros((1, BLK), jnp.int32))
        return jnp.where(tied, j, j_default)

    any_tied = jnp.max(tied.astype(jnp.int32)) > 0
    jmax = lax.cond(any_tied, tie_search, lambda: j_default)

    acc_ref[...] = jnp.zeros_like(acc_ref)

    def att_update(rows, keys, rowidx, vt, bias, carry):
        sel = (keys > thr) | ((keys == thr) & (rowidx <= jmax))
        selb = jnp.where(sel, 0.0, NEG)
        add = jnp.concatenate([selb, selb], axis=1)
        gs = [slice(g * GW, (g + 1) * GW) for g in range(NG)]
        ss = [jnp.dot(rows, qlat_ref[:, gs[g]], preferred_element_type=F32) for g in range(NG)]
        ss = [ss[g] + (add if bias is None else add + bias[:, gs[g]]) for g in range(NG)]
        steps = [_online_softmax_step(ss[g], *carry[g]) for g in range(NG)]
        pvs = [jnp.dot(vt, steps[g][3].astype(BF16), preferred_element_type=F32) for g in range(NG)]
        for g in range(NG):
            acc_ref[:, gs[g]] = acc_ref[:, gs[g]] * steps[g][2] + pvs[g]
        return tuple((steps[g][0], steps[g][1]) for g in range(NG))

    def att_group(kb, n, carry):
        off = pl.multiple_of(kb * BLK, BLK)
        rowidx = off + lax.broadcasted_iota(jnp.int32, (n * BLK, BLK), 0)
        vt = jnp.concatenate([ckvt_ref[kb + u] for u in range(n)], axis=1) if n > 1 else ckvt_ref[kb]
        return att_update(ckv_ref[pl.ds(off, n * BLK), :], keys_ref[pl.ds(off, n * BLK), :], rowidx, vt, None, carry)

    def att_special(carry):
        offs = [pl.multiple_of(kb * BLK, BLK) for kb in special_blocks]
        rows = jnp.concatenate([ckv_ref[pl.ds(o, BLK), :] for o in offs], axis=0)
        keys = jnp.concatenate([keys_ref[pl.ds(o, BLK), :] for o in offs], axis=0)
        rowidx = jnp.concatenate([o + row for o in offs], axis=0)
        vt = jnp.concatenate([ckvt_ref[kb] for kb in special_blocks], axis=1)
        bias = jnp.concatenate([bias_ref[t_meta], bias_ref[t_prev], bias_ref[T_DIAG]], axis=0)
        return att_update(rows, keys, rowidx, vt, bias, carry)

    m0 = jnp.full((1, GW), NEG, F32)
    l0 = jnp.zeros((1, GW), F32)
    stats = _visit_key_blocks(i, att_group, att_special, tuple((m0, l0) for _ in range(NG)))
    l = jnp.concatenate([stats[g][1] for g in range(NG)], axis=1)

    olat = (acc_ref[...] / l).astype(BF16)
    ots = [jnp.dot(wuvt_ref[h], olat[:, h * BLK:(h + 1) * BLK], preferred_element_type=F32) for h in range(A_HEADS)]
    for h in range(A_HEADS):
        o_ref[:, h * BLK:(h + 1) * BLK] = ots[h].T


def _attn_a(proj_a, iwt, ckv, ckvt, ik, wuk, wuvt, bias_a):
    qrow = lambda b, i: b * NKB + 1 + i
    return pl.pallas_call(
        _attn_a_kernel,
        grid=(BATCH, NQB),
        in_specs=[
            pl.BlockSpec((BLK, A_WIDTH), lambda b, i: (qrow(b, i), 0)),
            pl.BlockSpec((BLK, IDX_HEADS * IDX_DIM), lambda b, i: (qrow(b, i), 1)),
            pl.BlockSpec((IDX_HEADS, BLK), lambda b, i: (0, qrow(b, i))),
            pl.BlockSpec((None, TP, KV_RANK), lambda b, i: (b, 0, 0)),
            pl.BlockSpec((None, NKB, KV_RANK, BLK), lambda b, i: (b, 0, 0, 0)),
            pl.BlockSpec((None, TP, IDX_DIM), lambda b, i: (b, 0, 0)),
            pl.BlockSpec((A_HEADS, KV_RANK, A_HEAD_DIM), lambda b, i: (0, 0, 0)),
            pl.BlockSpec((A_HEADS, A_HEAD_DIM, KV_RANK), lambda b, i: (0, 0, 0)),
            pl.BlockSpec((5, BLK, A_HEADS * BLK), lambda b, i: (0, 0, 0)),
        ],
        out_specs=pl.BlockSpec((BLK, A_WIDTH), lambda b, i: (b * NQB + i, 0)),
        out_shape=jax.ShapeDtypeStruct((BATCH * SEQ, A_WIDTH), F32),
        scratch_shapes=[pltpu.VMEM(((NKB + FAR - 1) * BLK, BLK), jnp.int32),
                        pltpu.VMEM((KV_RANK, A_HEADS * BLK), BF16),
                        pltpu.VMEM((KV_RANK, A_HEADS * BLK), F32),
                        pltpu.VMEM((IDX_DIM, IDX_HEADS * BLK), BF16)],
        compiler_params=pltpu.CompilerParams(
            dimension_semantics=("arbitrary", "arbitrary"), vmem_limit_bytes=VMEM_LIMIT),
        name="attn_a",
    )(proj_a, proj_a, iwt, ckv.reshape(BATCH, TP, KV_RANK), ckvt.reshape(BATCH, NKB, KV_RANK, BLK),
      ik.reshape(BATCH, TP, IDX_DIM), wuk, wuvt, bias_a)


def _attn_b_kernel(lam_ref, q_ref, k_ref, vt_ref, bias_ref, subw_ref, o_ref, acc_ref, *, lam_init):
    i = pl.program_id(2)
    t_meta = jnp.where(i == 0, T_META0, T_METAFAR)
    t_prev = jnp.where(i == 0, T_NONE, T_PREV)
    special_blocks = (0, i, i + 1)
    lp = lam_ref[...]
    lam = (jnp.exp(jnp.sum(lp[0:1] * lp[1:2], axis=-1, keepdims=True))
           - jnp.exp(jnp.sum(lp[2:3] * lp[3:4], axis=-1, keepdims=True)) + lam_init)

    lane = lax.broadcasted_iota(jnp.int32, (BLK, BLK), 1)
    qbd = []
    for hh in range(HPS):
        q = q_ref[:, hh * BLK:(hh + 1) * BLK]
        zq = jnp.zeros_like(q)
        qbd.append(jnp.concatenate([jnp.where(lane < B_QK_DIM, q, zq), jnp.where(lane >= B_QK_DIM, q, zq)], axis=0))

    acc_ref[...] = jnp.zeros_like(acc_ref)

    def update_all(rows, vts, biases, carry):
        ss = [lax.dot_general(rows[hh], qbd[hh], NT_DIMS, preferred_element_type=F32) for hh in range(HPS)]
        if biases is not None:
            ss = [ss[hh] + jnp.concatenate([biases[hh], biases[hh]], axis=1) for hh in range(HPS)]
        steps = [_online_softmax_step(ss[hh], *carry[hh]) for hh in range(HPS)]
        pvs = [jnp.dot(vts[hh], steps[hh][3].astype(BF16), preferred_element_type=F32) for hh in range(HPS)]
        for hh in range(HPS):
            acc_ref[hh] = acc_ref[hh] * steps[hh][2] + pvs[hh]
        return tuple((steps[hh][0], steps[hh][1]) for hh in range(HPS))

    def group(kb, n, carry):
        off = pl.multiple_of(kb * BLK, BLK)
        rows = [k_ref[pl.ds(off, n * BLK), hh * BLK:(hh + 1) * BLK] for hh in range(HPS)]
        vts = [jnp.concatenate([vt_ref[kb + u, hh] for u in range(n)], axis=1) if n > 1 else vt_ref[kb, hh]
               for hh in range(HPS)]
        return update_all(rows, vts, None, carry)

    def special(carry):
        offs = [pl.multiple_of(kb * BLK, BLK) for kb in special_blocks]
        rows = [jnp.concatenate([k_ref[pl.ds(o, BLK), hh * BLK:(hh + 1) * BLK] for o in offs], axis=0)
                for hh in range(HPS)]
        vts = [jnp.concatenate([vt_ref[kb, hh] for kb in special_blocks], axis=1) for hh in range(HPS)]
        biases = [jnp.concatenate([bias_ref[t_meta, hh], bias_ref[t_prev, hh], bias_ref[T_DIAG, hh]], axis=0)
                  for hh in range(HPS)]
        return update_all(rows, vts, biases, carry)

    m0 = jnp.full((1, 2 * BLK), NEG, F32)
    l0 = jnp.zeros((1, 2 * BLK), F32)
    stats = _visit_key_blocks(i, group, special, tuple((m0, l0) for _ in range(HPS)))

    for hh in range(HPS):
        a = acc_ref[hh] / stats[hh][1]
        o = a[:, :BLK] - lam * a[:, BLK:]
        ms = jnp.mean(o * o, axis=0, keepdims=True)
        y = o * lax.rsqrt(ms + EPS) * subw_ref[...] * (1.0 - lam_init)
        o_ref[:, hh * BLK:(hh + 1) * BLK] = y.T


def _attn_b(qkv_b, vt, bias_b, lam_p, subw, lam_init):
    qrow = lambda b, g, i: b * NKB + 1 + i
    wide = HPS * BLK
    qcol0 = 0
    kcol0 = 2 * B_HEADS * B_QK_DIM // wide
    return pl.pallas_call(
        functools.partial(_attn_b_kernel, lam_init=lam_init),
        grid=(BATCH, B_HEADS // HPS, NQB),
        in_specs=[
            pl.BlockSpec((4, B_QK_DIM), lambda b, g, i: (0, 0)),
            pl.BlockSpec((BLK, wide), lambda b, g, i: (qrow(b, g, i), qcol0 + g)),
            pl.BlockSpec((None, TP, wide), lambda b, g, i: (b, 0, kcol0 + g)),
            pl.BlockSpec((None, NKB, HPS, B_V_DIM, BLK), lambda b, g, i: (b, 0, g, 0, 0)),
            pl.BlockSpec((5, HPS, BLK, BLK), lambda b, g, i: (0, g, 0, 0)),
            pl.BlockSpec((B_V_DIM, BLK), lambda b, g, i: (0, 0)),
        ],
        out_specs=pl.BlockSpec((BLK, wide), lambda b, g, i: (b * NQB + i, g)),
        out_shape=jax.ShapeDtypeStruct((BATCH * SEQ, B_WIDTH), F32),
        scratch_shapes=[pltpu.VMEM((HPS, B_V_DIM, 2 * BLK), F32)],
        compiler_params=pltpu.CompilerParams(
            dimension_semantics=("arbitrary", "arbitrary", "arbitrary"), vmem_limit_bytes=VMEM_LIMIT),
        name="attn_b",
    )(lam_p, qkv_b, qkv_b.reshape(BATCH, TP, -1), vt, bias_b, subw)


def _out_kernel(oa_ref, za_ref, ob_ref, zb_ref, ga_ref, gb_ref, x_ref, woa_ref, wob_ref, wout_ref, pw_ref, o_ref):
    a = (oa_ref[...] * jax.nn.silu(za_ref[...])).astype(BF16)
    ya = jnp.dot(a, woa_ref[...], preferred_element_type=F32)
    b = (ob_ref[...] * jax.nn.silu(zb_ref[...])).astype(BF16)
    yb = jnp.dot(b, wob_ref[...], preferred_element_type=F32)
    mix = jax.nn.sigmoid(ga_ref[...]) * ya + jax.nn.sigmoid(gb_ref[...]) * yb
    out = jnp.dot(mix.astype(BF16), wout_ref[...], preferred_element_type=F32)
    ms = jnp.mean(out * out, axis=-1, keepdims=True)
    o_ref[...] = x_ref[...] + out * lax.rsqrt(ms + EPS) * pw_ref[...]


def _out_stage(o_a, o_b, z_a, z_b, gates, x2, woa, wob, wout, pw):
    tm = 2 * BLK
    const = lambda g: (0, 0)
    return pl.pallas_call(
        _out_kernel,
        grid=(BATCH * SEQ // tm,),
        in_specs=[
            pl.BlockSpec((tm, A_WIDTH), lambda g: (g, 0)),
            pl.BlockSpec((tm, A_WIDTH), lambda g: (g, 0)),
            pl.BlockSpec((tm, B_WIDTH), lambda g: (g, 0)),
            pl.BlockSpec((tm, B_WIDTH), lambda g: (g, 0)),
            pl.BlockSpec((tm, D_MODEL), lambda g: (g, 0)),
            pl.BlockSpec((tm, D_MODEL), lambda g: (g, 1)),
            pl.BlockSpec((tm, D_MODEL), lambda g: (g, 0)),
            pl.BlockSpec((A_WIDTH, D_MODEL), const, pipeline_mode=pl.Buffered(1)),
            pl.BlockSpec((B_WIDTH, D_MODEL), const, pipeline_mode=pl.Buffered(1)),
            pl.BlockSpec((D_MODEL, D_MODEL), const, pipeline_mode=pl.Buffered(1)),
            pl.BlockSpec((1, D_MODEL), const),
        ],
        out_specs=pl.BlockSpec((tm, D_MODEL), lambda g: (g, 0)),
        out_shape=jax.ShapeDtypeStruct((BATCH * SEQ, D_MODEL), F32),
        compiler_params=pltpu.CompilerParams(
            dimension_semantics=("arbitrary",), vmem_limit_bytes=VMEM_LIMIT),
        name="out_stage",
    )(o_a, z_a, o_b, z_b, gates, gates, x2, woa, wob, wout, pw)


def kernel(x, meta_tokens, rel_bias, pre_norm_w, w_in, kv_norm_w, w_uk, w_uv, idx_k_norm_w, idx_k_norm_b,
           diff_lambda, diff_subln_w, w_o_a, w_o_b, w_out, post_norm_w):
    assert x.shape == (BATCH, SEQ, D_MODEL) and w_in.shape[0] == 1
    layer = 0
    lam_init = 0.8 - 0.6 * math.exp(-0.3 * layer)

    meta_block = jnp.concatenate([meta_tokens.astype(F32), jnp.zeros((BLK - N_META, D_MODEL), F32)], axis=0)

    w_head, w_rest = _wprep(w_in[0])
    wd = A_WIDTH
    assert all(IN_SIZES[k] == wd for k in (0, 2, 3, 6, 7, 8, 9)) and IN_SIZES[10] == IN_SIZES[11] == 2 * wd
    scale_qb = jnp.concatenate([jnp.full((1, wd), B_QK_DIM ** -0.5 * LOG2E, F32), jnp.ones((1, 2 * wd), F32)], axis=1)

    u, u_f = _prenorm(x.reshape(BATCH * SEQ, D_MODEL), meta_block, pre_norm_w[0][None].astype(F32))
    tm, tmf = ROWS // 8, BATCH * SEQ // 8
    q_iq = _matmul(u, w_head, 0, 2 * wd, BF16, tm, wd, "proj_q_iq")
    lat = _matmul(u, w_head, 3 * wd, KV_RANK + BLK, F32, tm, KV_RANK + BLK, "proj_latent")
    qkv_b = _matmul(u, w_rest, 0, 3 * wd, BF16, tm, wd, "proj_qkv_b", col_scale=scale_qb)
    z_a = _matmul(u_f, w_head, 2 * wd, wd, F32, tmf, wd, "proj_z_a")
    z_b = _matmul(u_f, w_rest, 3 * wd, wd, F32, tmf, wd, "proj_z_b")
    gates = _matmul(u_f, w_rest, 4 * wd, 4 * wd, F32, tmf, wd, "proj_gates")

    ckv, ckvt, ik, iwt, vt = _kvprep(lat, qkv_b, 2 * wd, kv_norm_w[0][None].astype(F32),
                                     idx_k_norm_w[0][None].astype(F32), idx_k_norm_b[0][None].astype(F32))

    bias = _bias_tiles(rel_bias) * LOG2E
    bias_a = jnp.transpose(bias[:, :A_HEADS], (0, 2, 1, 3)).reshape(5, BLK, A_HEADS * BLK)
    bias_b = bias[:, A_HEADS:]

    wuk = jnp.transpose(w_uk[0], (1, 0, 2)).astype(BF16)
    wuvt = jnp.transpose(w_uv[0], (1, 2, 0)).astype(BF16)
    o_a = _attn_a(q_iq, iwt, ckv, ckvt, ik, wuk, wuvt, bias_a)

    vt = vt.reshape(BATCH, NKB, B_HEADS, B_V_DIM, BLK)
    subw = jnp.broadcast_to(diff_subln_w[0].astype(F32)[:, None], (B_V_DIM, BLK))
    o_b = _attn_b(qkv_b, vt, bias_b, diff_lambda[0].astype(F32), subw, lam_init)

    out = _out_stage(o_a, o_b, z_a, z_b, gates, x.reshape(BATCH * SEQ, D_MODEL),
                     w_o_a[0].astype(BF16), w_o_b[0].astype(BF16), w_out[0].astype(BF16),
                     post_norm_w[0][None].astype(F32))
    return out.reshape(BATCH, SEQ, D_MODEL)
```

```python
import functools
import math

import numpy as np
import jax
import jax.numpy as jnp
from jax import lax
from jax.experimental import pallas as pl
from jax.experimental.pallas import tpu as pltpu

D_MODEL = 2048
BATCH = 2
SEQ = 4096
CHUNK = 64
N_META = 16
N_BUCKETS = 32
MAX_DISTANCE = 128
A_HEADS = 8
A_HEAD_DIM = 128
KV_RANK = 256
IDX_HEADS = 16
IDX_DIM = 64
TOPK = 256
B_HEADS = 8
B_QK_DIM = 64
B_V_DIM = 128
A_WIDTH = A_HEADS * A_HEAD_DIM
B_WIDTH = B_HEADS * B_V_DIM
IN_SIZES = (A_WIDTH, KV_RANK, A_WIDTH, IDX_HEADS * IDX_DIM, IDX_DIM, IDX_HEADS,
            2 * B_HEADS * B_QK_DIM, 2 * B_HEADS * B_QK_DIM, B_WIDTH, B_WIDTH,
            D_MODEL, D_MODEL)
EPS = 1e-6

BLK = 128
NQB = SEQ // BLK
NKB = NQB + 1
TP = NKB * BLK
ROWS = BATCH * TP
FAR = 4
HPS = 8
NEG = -1e30
INT_MIN = -2 ** 31
LOG2E = math.log2(math.e)
VMEM_LIMIT = 56 * 1024 * 1024

F32 = jnp.float32
BF16 = jnp.bfloat16
NT_DIMS = (((1,), (1,)), ((), ()))


def _t5_bucket_np(rel):
    nb = N_BUCKETS // 2
    max_exact = nb // 2
    ret = np.where(rel > 0, nb, 0)
    n = np.abs(rel)
    nf = np.maximum(n, 1).astype(np.float32)
    large = max_exact + (np.log(nf / np.float32(max_exact))
                         / np.float32(math.log(MAX_DISTANCE / max_exact))
                         * np.float32(nb - max_exact)).astype(np.int32)
    large = np.minimum(large, nb - 1)
    return ret + np.where(n < max_exact, n, large)


T_DIAG, T_PREV, T_META0, T_METAFAR, T_NONE = range(5)


def _bias_tiles(rel_bias):
    a = np.arange(BLK)[:, None]
    b = np.arange(BLK)[None, :]
    nowhere = np.zeros((BLK, BLK), bool)
    pad_rows = (a >= N_META) | nowhere
    rels = np.stack([a - b, a - b - BLK, a - N_META - b])
    dis = np.stack([(a >= CHUNK) & (b < CHUNK), nowhere, pad_rows, pad_rows, ~nowhere])
    idx = _t5_bucket_np(rels)
    far_bucket = N_BUCKETS // 2 - 1
    assert _t5_bucket_np(np.array([-BLK - 1]))[0] == far_bucket
    rb = rel_bias.astype(F32)
    heads = A_HEADS + B_HEADS
    tiles = jnp.zeros((3, heads, BLK, BLK), F32)
    for k in np.unique(idx):
        tiles = jnp.where((idx == k)[:, None], rb[k][None, :, None, None], tiles)
    tiles = tiles - rb[far_bucket][None, :, None, None]
    tiles = jnp.concatenate([tiles, jnp.zeros((2, heads, BLK, BLK), F32)], axis=0)
    return jnp.where(dis[:, None], NEG, tiles)


def _far_split(i):
    n_far = jnp.maximum(i - 1, 0)
    return n_far, lax.shift_right_logical(n_far, 2)


def _visit_key_blocks(i, group_fn, special_fn, carry):
    n_far, n_chunks = _far_split(i)
    carry = lax.fori_loop(0, n_chunks, lambda c, cr: group_fn(1 + FAR * c, FAR, cr), carry)
    carry = lax.fori_loop(1 + FAR * n_chunks, 1 + n_far, lambda kb, cr: group_fn(kb, 1, cr), carry)
    return special_fn(carry)


def _prenorm_kernel(x_ref, meta_ref, w_ref, o_ref, o32_ref, of_ref):
    def norm(x):
        ms = jnp.mean(x * x, axis=-1, keepdims=True)
        return x * lax.rsqrt(ms + EPS) * w_ref[...]

    is_meta = lax.rem(pl.program_id(0), NKB) == 0

    @pl.when(is_meta)
    def _():
        u = norm(meta_ref[...])
        o32_ref[...] = u
        o_ref[...] = u.astype(o_ref.dtype)

    @pl.when(jnp.logical_not(is_meta))
    def _():
        u = norm(x_ref[...])
        o32_ref[...] = u
        o_ref[...] = u.astype(o_ref.dtype)
        of_ref[...] = u.astype(of_ref.dtype)


def _prenorm(x2, meta_block, w):
    frame_block = lambda r: (jnp.maximum(r - r // NKB - 1, 0), 0)
    return pl.pallas_call(
        _prenorm_kernel,
        grid=(ROWS // BLK,),
        in_specs=[pl.BlockSpec((BLK, D_MODEL), frame_block),
                  pl.BlockSpec((BLK, D_MODEL), lambda r: (0, 0)),
                  pl.BlockSpec((1, D_MODEL), lambda r: (0, 0))],
        out_specs=[pl.BlockSpec((BLK, D_MODEL), lambda r: (r, 0)),
                   pl.BlockSpec((BLK, D_MODEL), lambda r: (r, 0)),
                   pl.BlockSpec((BLK, D_MODEL), frame_block)],
        out_shape=[jax.ShapeDtypeStruct((ROWS, D_MODEL), BF16),
                   jax.ShapeDtypeStruct((ROWS, D_MODEL), F32),
                   jax.ShapeDtypeStruct((BATCH * SEQ, D_MODEL), BF16)],
        compiler_params=pltpu.CompilerParams(dimension_semantics=("arbitrary",)),
        name="prenorm",
    )(x2, meta_block, w)


IN_OFFS = tuple(int(v) for v in np.concatenate([[0], np.cumsum(IN_SIZES)]))
HEAD_BLK = 2 * BLK
HEAD_ORDER = (0, 2, 1)
IDX_ROWS = IN_SIZES[3] + HEAD_BLK
W_REST = IN_OFFS[-1] - IN_OFFS[6]
REST_BLK = 4 * BLK


def _head_blocks():
    blocks = []
    for k in HEAD_ORDER:
        assert IN_OFFS[k] % HEAD_BLK == 0 and IN_SIZES[k] % HEAD_BLK == 0
        blocks += list(range(IN_OFFS[k] // HEAD_BLK, IN_OFFS[k + 1] // HEAD_BLK))
    return blocks


def _cast_kernel(w_ref, o_ref):
    o_ref[...] = w_ref[...].astype(o_ref.dtype)


def _wprep_head(wt):
    src = _head_blocks()

    def src_block(t):
        b = jnp.int32(src[-1])
        for pos in range(len(src) - 2, -1, -1):
            b = jnp.where(t == pos, src[pos], b)
        return b, 0
    return pl.pallas_call(
        _cast_kernel,
        grid=(len(src),),
        in_specs=[pl.BlockSpec((HEAD_BLK, D_MODEL), src_block)],
        out_specs=pl.BlockSpec((HEAD_BLK, D_MODEL), lambda t: (t, 0)),
        out_shape=jax.ShapeDtypeStruct((len(src) * HEAD_BLK, D_MODEL), BF16),
        compiler_params=pltpu.CompilerParams(dimension_semantics=("arbitrary",)),
        name="wprep_head",
    )(wt)


def _shift_cast_kernel(lo_ref, hi_ref, o_ref, *, shift):
    o_ref[...] = jnp.concatenate([lo_ref[shift:], hi_ref[:shift]], axis=0).astype(o_ref.dtype)


def _wprep_rest(wt):
    base, shift = divmod(IN_OFFS[6], REST_BLK)
    assert shift % 8 == 0 and W_REST % REST_BLK == 0
    return pl.pallas_call(
        functools.partial(_shift_cast_kernel, shift=shift),
        grid=(W_REST // REST_BLK,),
        in_specs=[pl.BlockSpec((REST_BLK, D_MODEL), lambda j: (base + j, 0)),
                  pl.BlockSpec((REST_BLK, D_MODEL), lambda j: (base + j + 1, 0))],
        out_specs=pl.BlockSpec((REST_BLK, D_MODEL), lambda j: (j, 0)),
        out_shape=jax.ShapeDtypeStruct((W_REST, D_MODEL), BF16),
        compiler_params=pltpu.CompilerParams(dimension_semantics=("arbitrary",), vmem_limit_bytes=VMEM_LIMIT),
        name="wprep_rest",
    )(wt, wt)


def _mm_kernel(a_ref, w_ref, o_ref):
    acc = lax.dot_general(a_ref[...], w_ref[...], NT_DIMS, preferred_element_type=F32)
    o_ref[...] = acc.astype(o_ref.dtype)


def _mm_scaled_kernel(a_ref, w_ref, cs_ref, o_ref):
    acc = lax.dot_general(a_ref[...], w_ref[...], NT_DIMS, preferred_element_type=F32)
    o_ref[...] = (acc * cs_ref[...]).astype(o_ref.dtype)


def _matmul(a, wt, col0, n, out_dtype, tm, tn, name, col_scale=None):
    m, k = a.shape
    assert col0 % tn == 0 and n % tn == 0 and m % tm == 0 and wt.shape[1] == k
    c0 = col0 // tn
    in_specs = [pl.BlockSpec((tm, k), lambda i, j: (i, 0)),
                pl.BlockSpec((tn, k), lambda i, j: (c0 + j, 0))]
    args = (a, wt)
    if col_scale is not None:
        in_specs.append(pl.BlockSpec((1, tn), lambda i, j: (0, j)))
        args += (col_scale,)
    return pl.pallas_call(
        _mm_kernel if col_scale is None else _mm_scaled_kernel,
        grid=(m // tm, n // tn),
        in_specs=in_specs,
        out_specs=pl.BlockSpec((tm, tn), lambda i, j: (i, j)),
        out_shape=jax.ShapeDtypeStruct((m, n), out_dtype),
        compiler_params=pltpu.CompilerParams(
            dimension_semantics=("arbitrary", "arbitrary"), vmem_limit_bytes=VMEM_LIMIT),
        name=name,
    )(*args)


def _kvprep_kernel(c_ref, t_ref, v_ref, kvw_ref, ikw_ref, ikb_ref, ckv_ref, ckvt_ref, ik_ref, iwt_ref, vt_ref):
    for blk in range(2):
        for h in range(B_HEADS):
            vh = v_ref[blk * BLK:(blk + 1) * BLK, h * B_V_DIM:(h + 1) * B_V_DIM]
            vt_ref[blk, h] = vh.astype(F32).T.astype(BF16)
    ckv = c_ref[...]
    ms = jnp.mean(ckv * ckv, axis=-1, keepdims=True)
    ckvn = ckv * lax.rsqrt(ms + EPS) * kvw_ref[...]
    ckv_ref[...] = ckvn.astype(BF16)
    ckvt_ref[0] = ckvn[:BLK].T.astype(BF16)
    ckvt_ref[1] = ckvn[BLK:].T.astype(BF16)
    tail = t_ref[:, :BLK]
    ik = tail[:, :IDX_DIM]
    mu = jnp.mean(ik, axis=-1, keepdims=True)
    var = jnp.mean(jnp.square(ik - mu), axis=-1, keepdims=True)
    ik_ref[...] = (ik - mu) * lax.rsqrt(var + EPS) * ikw_ref[...] + ikb_ref[...]
    iwt_ref[...] = (tail * (IDX_HEADS ** -0.5 * IDX_DIM ** -0.5)).T[IDX_DIM:IDX_DIM + IDX_HEADS, :]


def _kvprep(c, idxp, kv_b, vcol, kvw, ikw, ikb):
    tm = 2 * BLK
    assert vcol % B_WIDTH == 0 and c.shape[1] == KV_RANK and idxp.shape[1] == IDX_ROWS
    return pl.pallas_call(
        _kvprep_kernel,
        grid=(ROWS // tm,),
        in_specs=[pl.BlockSpec((tm, KV_RANK), lambda i: (i, 0)),
                  pl.BlockSpec((tm, HEAD_BLK), lambda i: (i, IN_SIZES[3] // HEAD_BLK)),
                  pl.BlockSpec((tm, B_WIDTH), lambda i: (i, vcol // B_WIDTH)),
                  pl.BlockSpec((1, KV_RANK), lambda i: (0, 0)),
                  pl.BlockSpec((1, IDX_DIM), lambda i: (0, 0)),
                  pl.BlockSpec((1, IDX_DIM), lambda i: (0, 0))],
        out_specs=[pl.BlockSpec((tm, KV_RANK), lambda i: (i, 0)),
                   pl.BlockSpec((2, KV_RANK, BLK), lambda i: (i, 0, 0)),
                   pl.BlockSpec((tm, IDX_DIM), lambda i: (i, 0)),
                   pl.BlockSpec((IDX_HEADS, tm), lambda i: (0, i)),
                   pl.BlockSpec((2, B_HEADS, B_V_DIM, BLK), lambda i: (i, 0, 0, 0))],
        out_shape=[jax.ShapeDtypeStruct((ROWS, KV_RANK), BF16),
                   jax.ShapeDtypeStruct((ROWS // BLK, KV_RANK, BLK), BF16),
                   jax.ShapeDtypeStruct((ROWS, IDX_DIM), F32),
                   jax.ShapeDtypeStruct((IDX_HEADS, ROWS), F32),
                   jax.ShapeDtypeStruct((ROWS // BLK, B_HEADS, B_V_DIM, BLK), BF16)],
        name="kvprep",
    )(c, idxp, kv_b, kvw, ikw, ikb)


def _online_softmax_step(s, m, l):
    m_new = jnp.maximum(m, jnp.max(s, axis=0, keepdims=True))
    alpha = jnp.exp2(m - m_new)
    p = jnp.exp2(s - m_new)
    return m_new, alpha * l + jnp.sum(p, axis=0, keepdims=True), alpha, p


def _attn_a_kernel(qa_ref, iq_ref, iwt_ref, ckv_ref, ckvt_ref, ik_ref, wuk_ref, wuvt_ref, bias_ref,
                   o_ref, keys_ref, qlat_ref, acc_ref, iqt_ref):
    i = pl.program_id(1)
    nkb = i + 2
    NG = A_HEADS // 2
    GW = 2 * BLK
    t_meta = jnp.where(i == 0, T_META0, T_METAFAR)
    t_prev = jnp.where(i == 0, T_NONE, T_PREV)
    special_blocks = (0, i, i + 1)

    for h in range(A_HEADS):
        qh = qa_ref[:, h * BLK:(h + 1) * BLK]
        ql = lax.dot_general(wuk_ref[h], qh, NT_DIMS, preferred_element_type=F32)
        qlat_ref[:, h * BLK:(h + 1) * BLK] = (ql * (A_HEAD_DIM ** -0.5 * LOG2E)).astype(BF16)

    for pr in range(IDX_HEADS // 2):
        t = iq_ref[:, pr * BLK:(pr + 1) * BLK].T
        iqt_ref[:, (2 * pr) * BLK:(2 * pr + 1) * BLK] = t[:IDX_DIM]
        iqt_ref[:, (2 * pr + 1) * BLK:(2 * pr + 2) * BLK] = t[IDX_DIM:]

    iwt = iwt_ref[...]
    row = lax.broadcasted_iota(jnp.int32, (BLK, BLK), 0)
    lane = lax.broadcasted_iota(jnp.int32, (BLK, BLK), 1)

    def idx_keys(ikrows):
        sc = jnp.zeros((ikrows.shape[0], BLK), F32)
        for pr in range(IDX_HEADS // 2):
            s2 = jnp.dot(ikrows, iqt_ref[:, pr * 2 * BLK:(pr + 1) * 2 * BLK], preferred_element_type=F32)
            sc = sc + jnp.maximum(s2[:, :BLK], 0.0) * iwt[2 * pr:2 * pr + 1, :]
            sc = sc + jnp.maximum(s2[:, BLK:], 0.0) * iwt[2 * pr + 1:2 * pr + 2, :]
        bits = lax.bitcast_convert_type(sc, jnp.int32)
        return bits ^ ((bits >> 31) & 0x7FFFFFFF)

    def put_keys(off, n, key):
        keys_ref[pl.ds(off, n * BLK), :] = key

    def idx_group(kb, n, carry):
        off = pl.multiple_of(kb * BLK, BLK)
        put_keys(off, n, idx_keys(ik_ref[pl.ds(off, n * BLK), :]))
        return carry

    def idx_special(carry):
        offs = [pl.multiple_of(kb * BLK, BLK) for kb in special_blocks]
        key = idx_keys(jnp.concatenate([ik_ref[pl.ds(o, BLK), :] for o in offs], axis=0))
        put_keys(offs[1], 1, key[BLK:2 * BLK])
        put_keys(offs[2], 1, jnp.where((row >= CHUNK) & (lane < CHUNK), INT_MIN, key[2 * BLK:]))
        put_keys(offs[0], 1, jnp.where(row >= N_META, INT_MIN, key[:BLK]))
        return carry

    _visit_key_blocks(i, idx_group, idx_special, 0)
    put_keys(pl.multiple_of(nkb * BLK, BLK), FAR - 1, jnp.full(((FAR - 1) * BLK, BLK), INT_MIN, jnp.int32))
    n_search = lax.shift_right_logical(nkb + FAR - 1, 2)
    crow = lax.broadcasted_iota(jnp.int32, (FAR * BLK, BLK), 0)
    chunk_off = lambda c: pl.multiple_of(c * FAR * BLK, FAR * BLK)

    def count(pred_fn):
        def body(c, acc8):
            k = keys_ref[pl.ds(chunk_off(c), FAR * BLK), :]
            hit = pred_fn(k, chunk_off(c)).astype(jnp.int32)
            return acc8 + jnp.sum(hit.reshape(FAR * BLK // 8, 8, BLK), axis=0)
        acc8 = lax.fori_loop(0, n_search, body, jnp.zeros((8, BLK), jnp.int32))
        return jnp.sum(acc8, axis=0, keepdims=True)

    zero = jnp.zeros((1, BLK), jnp.int32)
    c0 = count(lambda k, off: k >= zero)
    prefix = jnp.where(c0 >= TOPK, 0, INT_MIN).astype(jnp.int32)

    def bit_body(t, prefix):
        cand = prefix | jnp.left_shift(jnp.int32(1), 30 - t)
        c = count(lambda k, off: k >= cand)
        return jnp.where(c >= TOPK, cand, prefix)

    thr = lax.fori_loop(0, 31, bit_body, prefix)
    full = thr == INT_MIN

    c_gt = count(lambda k, off: k > thr)
    c_eq = count(lambda k, off: k == thr)
    need = TOPK - c_gt
    tied = jnp.logical_and(jnp.logical_not(full), c_eq > need)
    j_default = jnp.where(full, -1, TP).astype(jnp.int32)

    def tie_search():
        def jbit(t, j):
            cand = j | jnp.left_shift(jnp.int32(1), 12 - t)
            c = count(lambda k, off: (k == thr) & ((off + crow) < cand))
            return jnp.where(c < need, cand, j)
        j = lax.fori_loop(0, 13, jbit, jnp.zeros((1, BLK), jnp.int32))
        return jnp.where(tied, j, j_default)

    any_tied = jnp.max(tied.astype(jnp.int32)) > 0
    jmax = lax.cond(any_tied, tie_search, lambda: j_default)

    acc_ref[...] = jnp.zeros_like(acc_ref)

    def att_update(rows, keys, rowidx, vt, bias, carry):
        sel = (keys > thr) | ((keys == thr) & (rowidx <= jmax))
        selb = jnp.where(sel, 0.0, NEG)
        add = jnp.concatenate([selb, selb], axis=1)
        gs = [slice(g * GW, (g + 1) * GW) for g in range(NG)]
        ss = [jnp.dot(rows, qlat_ref[:, gs[g]], preferred_element_type=F32) for g in range(NG)]
        ss = [ss[g] + (add if bias is None else add + bias[:, gs[g]]) for g in range(NG)]
        steps = [_online_softmax_step(ss[g], *carry[g]) for g in range(NG)]
        pvs = [jnp.dot(vt, steps[g][3].astype(BF16), preferred_element_type=F32) for g in range(NG)]
        for g in range(NG):
            acc_ref[:, gs[g]] = acc_ref[:, gs[g]] * steps[g][2] + pvs[g]
        return tuple((steps[g][0], steps[g][1]) for g in range(NG))

    def att_group(kb, n, carry):
        off = pl.multiple_of(kb * BLK, BLK)
        rowidx = off + lax.broadcasted_iota(jnp.int32, (n * BLK, BLK), 0)
        vt = jnp.concatenate([ckvt_ref[kb + u] for u in range(n)], axis=1) if n > 1 else ckvt_ref[kb]
        return att_update(ckv_ref[pl.ds(off, n * BLK), :], keys_ref[pl.ds(off, n * BLK), :], rowidx, vt, None, carry)

    def att_special(carry):
        offs = [pl.multiple_of(kb * BLK, BLK) for kb in special_blocks]
        rows = jnp.concatenate([ckv_ref[pl.ds(o, BLK), :] for o in offs], axis=0)
        keys = jnp.concatenate([keys_ref[pl.ds(o, BLK), :] for o in offs], axis=0)
        rowidx = jnp.concatenate([o + row for o in offs], axis=0)
        vt = jnp.concatenate([ckvt_ref[kb] for kb in special_blocks], axis=1)
        bias = jnp.concatenate([bias_ref[t_meta], bias_ref[t_prev], bias_ref[T_DIAG]], axis=0)
        return att_update(rows, keys, rowidx, vt, bias, carry)

    m0 = jnp.full((1, GW), NEG, F32)
    l0 = jnp.zeros((1, GW), F32)
    stats = _visit_key_blocks(i, att_group, att_special, tuple((m0, l0) for _ in range(NG)))
    l = jnp.concatenate([stats[g][1] for g in range(NG)], axis=1)

    olat = (acc_ref[...] / l).astype(BF16)
    ots = [jnp.dot(wuvt_ref[h], olat[:, h * BLK:(h + 1) * BLK], preferred_element_type=F32) for h in range(A_HEADS)]
    for h in range(A_HEADS):
        o_ref[:, h * BLK:(h + 1) * BLK] = ots[h].T


def _attn_a(q_a, idxp, iwt, ckv, ckvt, ik, wuk, wuvt, bias_a):
    qrow = lambda b, i: b * NKB + 1 + i
    return pl.pallas_call(
        _attn_a_kernel,
        grid=(BATCH, NQB),
        in_specs=[
            pl.BlockSpec((BLK, A_WIDTH), lambda b, i: (qrow(b, i), 0)),
            pl.BlockSpec((BLK, IDX_HEADS * IDX_DIM), lambda b, i: (qrow(b, i), 0)),
            pl.BlockSpec((IDX_HEADS, BLK), lambda b, i: (0, qrow(b, i))),
            pl.BlockSpec((None, TP, KV_RANK), lambda b, i: (b, 0, 0)),
            pl.BlockSpec((None, NKB, KV_RANK, BLK), lambda b, i: (b, 0, 0, 0)),
            pl.BlockSpec((None, TP, IDX_DIM), lambda b, i: (b, 0, 0)),
            pl.BlockSpec((A_HEADS, KV_RANK, A_HEAD_DIM), lambda b, i: (0, 0, 0)),
            pl.BlockSpec((A_HEADS, A_HEAD_DIM, KV_RANK), lambda b, i: (0, 0, 0)),
            pl.BlockSpec((5, BLK, A_HEADS * BLK), lambda b, i: (0, 0, 0)),
        ],
        out_specs=pl.BlockSpec((BLK, A_WIDTH), lambda b, i: (b * NQB + i, 0)),
        out_shape=jax.ShapeDtypeStruct((BATCH * SEQ, A_WIDTH), F32),
        scratch_shapes=[pltpu.VMEM(((NKB + FAR - 1) * BLK, BLK), jnp.int32),
                        pltpu.VMEM((KV_RANK, A_HEADS * BLK), BF16),
                        pltpu.VMEM((KV_RANK, A_HEADS * BLK), F32),
                        pltpu.VMEM((IDX_DIM, IDX_HEADS * BLK), F32)],
        compiler_params=pltpu.CompilerParams(
            dimension_semantics=("arbitrary", "arbitrary"), vmem_limit_bytes=VMEM_LIMIT),
        name="attn_a",
    )(q_a, idxp, iwt, ckv.reshape(BATCH, TP, KV_RANK), ckvt.reshape(BATCH, NKB, KV_RANK, BLK),
      ik.reshape(BATCH, TP, IDX_DIM), wuk, wuvt, bias_a)


def _attn_b_kernel(lam_ref, q_ref, k_ref, vt_ref, bias_ref, subw_ref, o_ref, acc_ref, *, lam_init):
    i = pl.program_id(2)
    t_meta = jnp.where(i == 0, T_META0, T_METAFAR)
    t_prev = jnp.where(i == 0, T_NONE, T_PREV)
    special_blocks = (0, i, i + 1)
    lp = lam_ref[...]
    lam = (jnp.exp(jnp.sum(lp[0:1] * lp[1:2], axis=-1, keepdims=True))
           - jnp.exp(jnp.sum(lp[2:3] * lp[3:4], axis=-1, keepdims=True)) + lam_init)

    lane = lax.broadcasted_iota(jnp.int32, (BLK, BLK), 1)
    qbd = []
    for hh in range(HPS):
        q = q_ref[:, hh * BLK:(hh + 1) * BLK]
        zq = jnp.zeros_like(q)
        qbd.append(jnp.concatenate([jnp.where(lane < B_QK_DIM, q, zq), jnp.where(lane >= B_QK_DIM, q, zq)], axis=0))

    acc_ref[...] = jnp.zeros_like(acc_ref)

    def update_all(rows, vts, biases, carry):
        ss = [lax.dot_general(rows[hh], qbd[hh], NT_DIMS, preferred_element_type=F32) for hh in range(HPS)]
        if biases is not None:
            ss = [ss[hh] + jnp.concatenate([biases[hh], biases[hh]], axis=1) for hh in range(HPS)]
        steps = [_online_softmax_step(ss[hh], *carry[hh]) for hh in range(HPS)]
        pvs = [jnp.dot(vts[hh], steps[hh][3].astype(BF16), preferred_element_type=F32) for hh in range(HPS)]
        for hh in range(HPS):
            acc_ref[hh] = acc_ref[hh] * steps[hh][2] + pvs[hh]
        return tuple((steps[hh][0], steps[hh][1]) for hh in range(HPS))

    def group(kb, n, carry):
        off = pl.multiple_of(kb * BLK, BLK)
        rows = [k_ref[pl.ds(off, n * BLK), hh * BLK:(hh + 1) * BLK] for hh in range(HPS)]
        vts = [jnp.concatenate([vt_ref[kb + u, hh] for u in range(n)], axis=1) if n > 1 else vt_ref[kb, hh]
               for hh in range(HPS)]
        return update_all(rows, vts, None, carry)

    def special(carry):
        offs = [pl.multiple_of(kb * BLK, BLK) for kb in special_blocks]
        rows = [jnp.concatenate([k_ref[pl.ds(o, BLK), hh * BLK:(hh + 1) * BLK] for o in offs], axis=0)
                for hh in range(HPS)]
        vts = [jnp.concatenate([vt_ref[kb, hh] for kb in special_blocks], axis=1) for hh in range(HPS)]
        biases = [jnp.concatenate([bias_ref[t_meta, hh], bias_ref[t_prev, hh], bias_ref[T_DIAG, hh]], axis=0)
                  for hh in range(HPS)]
        return update_all(rows, vts, biases, carry)

    m0 = jnp.full((1, 2 * BLK), NEG, F32)
    l0 = jnp.zeros((1, 2 * BLK), F32)
    stats = _visit_key_blocks(i, group, special, tuple((m0, l0) for _ in range(HPS)))

    for hh in range(HPS):
        a = acc_ref[hh] / stats[hh][1]
        o = a[:, :BLK] - lam * a[:, BLK:]
        ms = jnp.mean(o * o, axis=0, keepdims=True)
        y = o * lax.rsqrt(ms + EPS) * subw_ref[...] * (1.0 - lam_init)
        o_ref[:, hh * BLK:(hh + 1) * BLK] = y.T


def _attn_b(qkv_b, vt, bias_b, lam_p, subw, lam_init):
    qrow = lambda b, g, i: b * NKB + 1 + i
    wide = HPS * BLK
    qcol0 = 0
    kcol0 = 2 * B_HEADS * B_QK_DIM // wide
    return pl.pallas_call(
        functools.partial(_attn_b_kernel, lam_init=lam_init),
        grid=(BATCH, B_HEADS // HPS, NQB),
        in_specs=[
            pl.BlockSpec((4, B_QK_DIM), lambda b, g, i: (0, 0)),
            pl.BlockSpec((BLK, wide), lambda b, g, i: (qrow(b, g, i), qcol0 + g)),
            pl.BlockSpec((None, TP, wide), lambda b, g, i: (b, 0, kcol0 + g)),
            pl.BlockSpec((None, NKB, HPS, B_V_DIM, BLK), lambda b, g, i: (b, 0, g, 0, 0)),
            pl.BlockSpec((5, HPS, BLK, BLK), lambda b, g, i: (0, g, 0, 0)),
            pl.BlockSpec((B_V_DIM, BLK), lambda b, g, i: (0, 0)),
        ],
        out_specs=pl.BlockSpec((BLK, wide), lambda b, g, i: (b * NQB + i, g)),
        out_shape=jax.ShapeDtypeStruct((BATCH * SEQ, B_WIDTH), F32),
        scratch_shapes=[pltpu.VMEM((HPS, B_V_DIM, 2 * BLK), F32)],
        compiler_params=pltpu.CompilerParams(
            dimension_semantics=("arbitrary", "arbitrary", "arbitrary"), vmem_limit_bytes=VMEM_LIMIT),
        name="attn_b",
    )(lam_p, qkv_b, qkv_b.reshape(BATCH, TP, -1), vt, bias_b, subw)


def _out_kernel(oa_ref, za_ref, ob_ref, zb_ref, ga_ref, gb_ref, x_ref, woa_ref, wob_ref, wout_ref, pw_ref, o_ref):
    a = (oa_ref[...] * jax.nn.silu(za_ref[...])).astype(BF16)
    ya = jnp.dot(a, woa_ref[...], preferred_element_type=F32)
    b = (ob_ref[...] * jax.nn.silu(zb_ref[...])).astype(BF16)
    yb = jnp.dot(b, wob_ref[...], preferred_element_type=F32)
    mix = jax.nn.sigmoid(ga_ref[...]) * ya + jax.nn.sigmoid(gb_ref[...]) * yb
    out = jnp.dot(mix.astype(BF16), wout_ref[...], preferred_element_type=F32)
    ms = jnp.mean(out * out, axis=-1, keepdims=True)
    o_ref[...] = x_ref[...] + out * lax.rsqrt(ms + EPS) * pw_ref[...]


def _out_stage(o_a, o_b, z_a, z_b, gates, x2, woa, wob, wout, pw):
    tm = 2 * BLK
    const = lambda g: (0, 0)
    return pl.pallas_call(
        _out_kernel,
        grid=(BATCH * SEQ // tm,),
        in_specs=[
            pl.BlockSpec((tm, A_WIDTH), lambda g: (g, 0)),
            pl.BlockSpec((tm, A_WIDTH), lambda g: (g, 0)),
            pl.BlockSpec((tm, B_WIDTH), lambda g: (g, 0)),
            pl.BlockSpec((tm, B_WIDTH), lambda g: (g, 0)),
            pl.BlockSpec((tm, D_MODEL), lambda g: (g, 0)),
            pl.BlockSpec((tm, D_MODEL), lambda g: (g, 1)),
            pl.BlockSpec((tm, D_MODEL), lambda g: (g, 0)),
            pl.BlockSpec((A_WIDTH, D_MODEL), const, pipeline_mode=pl.Buffered(1)),
            pl.BlockSpec((B_WIDTH, D_MODEL), const, pipeline_mode=pl.Buffered(1)),
            pl.BlockSpec((D_MODEL, D_MODEL), const, pipeline_mode=pl.Buffered(1)),
            pl.BlockSpec((1, D_MODEL), const),
        ],
        out_specs=pl.BlockSpec((tm, D_MODEL), lambda g: (g, 0)),
        out_shape=jax.ShapeDtypeStruct((BATCH * SEQ, D_MODEL), F32),
        compiler_params=pltpu.CompilerParams(
            dimension_semantics=("arbitrary",), vmem_limit_bytes=VMEM_LIMIT),
        name="out_stage",
    )(o_a, z_a, o_b, z_b, gates, gates, x2, woa, wob, wout, pw)


def kernel(x, meta_tokens, rel_bias, pre_norm_w, w_in, kv_norm_w, w_uk, w_uv, idx_k_norm_w, idx_k_norm_b,
           diff_lambda, diff_subln_w, w_o_a, w_o_b, w_out, post_norm_w):
    assert x.shape == (BATCH, SEQ, D_MODEL) and w_in.shape[0] == 1
    layer = 0
    lam_init = 0.8 - 0.6 * math.exp(-0.3 * layer)

    meta_block = jnp.concatenate([meta_tokens.astype(F32), jnp.zeros((BLK - N_META, D_MODEL), F32)], axis=0)

    wt = w_in[0].T
    w_head, w_rest = _wprep_head(wt), _wprep_rest(wt)
    wd = A_WIDTH
    assert all(IN_SIZES[k] == wd for k in (0, 2, 3, 6, 7, 8, 9)) and IN_SIZES[10] == IN_SIZES[11] == 2 * wd
    scale_qb = jnp.concatenate([jnp.full((1, wd), B_QK_DIM ** -0.5 * LOG2E, F32), jnp.ones((1, 2 * wd), F32)], axis=1)

    u, u32, u_f = _prenorm(x.reshape(BATCH * SEQ, D_MODEL), meta_block, pre_norm_w[0][None].astype(F32))
    tm, tmf = ROWS // 8, BATCH * SEQ // 8
    q_a = _matmul(u, w_head, 0, wd, BF16, tm, wd, "proj_q_a")
    lat = _matmul(u, w_head, 2 * wd, KV_RANK, F32, tm, KV_RANK, "proj_latent")
    qkv_b = _matmul(u, w_rest, 0, 3 * wd, BF16, tm, wd, "proj_qkv_b", col_scale=scale_qb)
    idxp = _matmul(u32, wt, IN_OFFS[3], IDX_ROWS, F32, tm, HEAD_BLK, "proj_indexer")
    z_a = _matmul(u_f, w_head, wd, wd, F32, tmf, wd, "proj_z_a")
    z_b = _matmul(u_f, w_rest, 3 * wd, wd, F32, tmf, wd, "proj_z_b")
    gates = _matmul(u_f, w_rest, 4 * wd, 4 * wd, F32, tmf, wd, "proj_gates")

    ckv, ckvt, ik, iwt, vt = _kvprep(lat, idxp, qkv_b, 2 * wd, kv_norm_w[0][None].astype(F32),
                                     idx_k_norm_w[0][None].astype(F32), idx_k_norm_b[0][None].astype(F32))

    bias = _bias_tiles(rel_bias) * LOG2E
    bias_a = jnp.transpose(bias[:, :A_HEADS], (0, 2, 1, 3)).reshape(5, BLK, A_HEADS * BLK)
    bias_b = bias[:, A_HEADS:]

    wuk = jnp.transpose(w_uk[0], (1, 0, 2)).astype(BF16)
    wuvt = jnp.transpose(w_uv[0], (1, 2, 0)).astype(BF16)
    o_a = _attn_a(q_a, idxp, iwt, ckv, ckvt, ik, wuk, wuvt, bias_a)

    vt = vt.reshape(BATCH, NKB, B_HEADS, B_V_DIM, BLK)
    subw = jnp.broadcast_to(diff_subln_w[0].astype(F32)[:, None], (B_V_DIM, BLK))
    o_b = _attn_b(qkv_b, vt, bias_b, diff_lambda[0].astype(F32), subw, lam_init)

    out = _out_stage(o_a, o_b, z_a, z_b, gates, x.reshape(BATCH * SEQ, D_MODEL),
                     w_o_a[0].astype(BF16), w_o_b[0].astype(BF16), w_out[0].astype(BF16),
                     post_norm_w[0][None].astype(F32))
    return out.reshape(BATCH, SEQ, D_MODEL)
```

```python
import functools
import math

import numpy as np
import jax
import jax.numpy as jnp
from jax import lax
from jax.experimental import pallas as pl
from jax.experimental.pallas import tpu as pltpu

D_MODEL = 2048
BATCH = 2
SEQ = 4096
CHUNK = 64
N_META = 16
N_BUCKETS = 32
MAX_DISTANCE = 128
A_HEADS = 8
A_HEAD_DIM = 128
KV_RANK = 256
IDX_HEADS = 16
IDX_DIM = 64
TOPK = 256
B_HEADS = 8
B_QK_DIM = 64
B_V_DIM = 128
A_WIDTH = A_HEADS * A_HEAD_DIM
B_WIDTH = B_HEADS * B_V_DIM
IN_SIZES = (A_WIDTH, KV_RANK, A_WIDTH, IDX_HEADS * IDX_DIM, IDX_DIM, IDX_HEADS,
            2 * B_HEADS * B_QK_DIM, 2 * B_HEADS * B_QK_DIM, B_WIDTH, B_WIDTH,
            D_MODEL, D_MODEL)
EPS = 1e-6

BLK = 128
NQB = SEQ // BLK
NKB = NQB + 1
TP = NKB * BLK
ROWS = BATCH * TP
FAR = 4
HPS = 8
B_FAR_SIZES = (FAR, 1)
NEG = -1e30
INT_MIN = -2 ** 31
LOG2E = math.log2(math.e)
VMEM_LIMIT = 56 * 1024 * 1024

F32 = jnp.float32
BF16 = jnp.bfloat16
NT_DIMS = (((1,), (1,)), ((), ()))
NN_DIMS = (((1,), (0,)), ((), ()))


def _t5_bucket_np(rel):
    nb = N_BUCKETS // 2
    max_exact = nb // 2
    ret = np.where(rel > 0, nb, 0)
    n = np.abs(rel)
    nf = np.maximum(n, 1).astype(np.float32)
    large = max_exact + (np.log(nf / np.float32(max_exact))
                         / np.float32(math.log(MAX_DISTANCE / max_exact))
                         * np.float32(nb - max_exact)).astype(np.int32)
    large = np.minimum(large, nb - 1)
    return ret + np.where(n < max_exact, n, large)


T_DIAG, T_PREV, T_META0, T_METAFAR, T_NONE = range(5)


def _bias_tiles(rel_bias):
    a = np.arange(BLK)[:, None]
    b = np.arange(BLK)[None, :]
    nowhere = np.zeros((BLK, BLK), bool)
    pad_rows = (a >= N_META) | nowhere
    rels = np.stack([a - b, a - b - BLK, a - N_META - b])
    dis = np.stack([(a >= CHUNK) & (b < CHUNK), nowhere, pad_rows, pad_rows, ~nowhere])
    idx = _t5_bucket_np(rels)
    far_bucket = N_BUCKETS // 2 - 1
    assert _t5_bucket_np(np.array([-BLK - 1]))[0] == far_bucket
    rb = rel_bias.astype(F32)
    heads = A_HEADS + B_HEADS
    tiles = jnp.zeros((3, heads, BLK, BLK), F32)
    for k in np.unique(idx):
        tiles = jnp.where((idx == k)[:, None], rb[k][None, :, None, None], tiles)
    tiles = tiles - rb[far_bucket][None, :, None, None]
    tiles = jnp.concatenate([tiles, jnp.zeros((2, heads, BLK, BLK), F32)], axis=0)
    return jnp.where(dis[:, None], NEG, tiles)


def _visit_key_blocks(i, group_fn, special_fn, carry, sizes=(FAR, 1)):
    n_far = jnp.maximum(i - 1, 0)
    start = jnp.int32(1)
    for size in sizes:
        shift = size.bit_length() - 1
        assert size == 1 << shift
        n_groups = lax.shift_right_logical(1 + n_far - start, shift)
        carry = lax.fori_loop(0, n_groups, lambda c, cr, s=start, z=size: group_fn(s + z * c, z, cr), carry)
        start = start + size * n_groups
    return special_fn(carry)


def _prenorm_kernel(x_ref, meta_ref, w_ref, o_ref, o32_ref, of_ref):
    def norm(x):
        ms = jnp.mean(x * x, axis=-1, keepdims=True)
        return x * lax.rsqrt(ms + EPS) * w_ref[...]

    is_meta = lax.rem(pl.program_id(0), NKB) == 0

    @pl.when(is_meta)
    def _():
        u = norm(meta_ref[...])
        o32_ref[...] = u
        o_ref[...] = u.astype(o_ref.dtype)

    @pl.when(jnp.logical_not(is_meta))
    def _():
        u = norm(x_ref[...])
        o32_ref[...] = u
        o_ref[...] = u.astype(o_ref.dtype)
        of_ref[...] = u.astype(of_ref.dtype)


def _prenorm(x2, meta_block, w):
    frame_block = lambda r: (jnp.maximum(r - r // NKB - 1, 0), 0)
    return pl.pallas_call(
        _prenorm_kernel,
        grid=(ROWS // BLK,),
        in_specs=[pl.BlockSpec((BLK, D_MODEL), frame_block),
                  pl.BlockSpec((BLK, D_MODEL), lambda r: (0, 0)),
                  pl.BlockSpec((1, D_MODEL), lambda r: (0, 0))],
        out_specs=[pl.BlockSpec((BLK, D_MODEL), lambda r: (r, 0)),
                   pl.BlockSpec((BLK, D_MODEL), lambda r: (r, 0)),
                   pl.BlockSpec((BLK, D_MODEL), frame_block)],
        out_shape=[jax.ShapeDtypeStruct((ROWS, D_MODEL), BF16),
                   jax.ShapeDtypeStruct((ROWS, D_MODEL), F32),
                   jax.ShapeDtypeStruct((BATCH * SEQ, D_MODEL), BF16)],
        compiler_params=pltpu.CompilerParams(dimension_semantics=("arbitrary",)),
        name="prenorm",
    )(x2, meta_block, w)


IN_OFFS = tuple(int(v) for v in np.concatenate([[0], np.cumsum(IN_SIZES)]))
HEAD_BLK = 2 * BLK
HEAD_ORDER = (0, 2, 1)
IDX_ROWS = IN_SIZES[3] + HEAD_BLK
W_REST = IN_OFFS[-1] - IN_OFFS[6]
REST_BLK = 4 * BLK


def _head_blocks():
    blocks = []
    for k in HEAD_ORDER:
        assert IN_OFFS[k] % HEAD_BLK == 0 and IN_SIZES[k] % HEAD_BLK == 0
        blocks += list(range(IN_OFFS[k] // HEAD_BLK, IN_OFFS[k + 1] // HEAD_BLK))
    return blocks


def _cast_kernel(w_ref, o_ref):
    o_ref[...] = w_ref[...].astype(o_ref.dtype)


def _wprep_head(wt):
    src = _head_blocks()

    def src_block(t):
        b = jnp.int32(src[-1])
        for pos in range(len(src) - 2, -1, -1):
            b = jnp.where(t == pos, src[pos], b)
        return b, 0
    return pl.pallas_call(
        _cast_kernel,
        grid=(len(src),),
        in_specs=[pl.BlockSpec((HEAD_BLK, D_MODEL), src_block)],
        out_specs=pl.BlockSpec((HEAD_BLK, D_MODEL), lambda t: (t, 0)),
        out_shape=jax.ShapeDtypeStruct((len(src) * HEAD_BLK, D_MODEL), BF16),
        compiler_params=pltpu.CompilerParams(dimension_semantics=("arbitrary",)),
        name="wprep_head",
    )(wt)


def _shift_cast_kernel(lo_ref, hi_ref, o_ref, *, shift):
    o_ref[...] = jnp.concatenate([lo_ref[shift:], hi_ref[:shift]], axis=0).astype(o_ref.dtype)


def _wprep_rest(wt):
    base, shift = divmod(IN_OFFS[6], REST_BLK)
    assert shift % 8 == 0 and W_REST % REST_BLK == 0
    return pl.pallas_call(
        functools.partial(_shift_cast_kernel, shift=shift),
        grid=(W_REST // REST_BLK,),
        in_specs=[pl.BlockSpec((REST_BLK, D_MODEL), lambda j: (base + j, 0)),
                  pl.BlockSpec((REST_BLK, D_MODEL), lambda j: (base + j + 1, 0))],
        out_specs=pl.BlockSpec((REST_BLK, D_MODEL), lambda j: (j, 0)),
        out_shape=jax.ShapeDtypeStruct((W_REST, D_MODEL), BF16),
        compiler_params=pltpu.CompilerParams(dimension_semantics=("arbitrary",), vmem_limit_bytes=VMEM_LIMIT),
        name="wprep_rest",
    )(wt, wt)


def _mm_kernel(a_ref, w_ref, o_ref):
    acc = lax.dot_general(a_ref[...], w_ref[...], NT_DIMS, preferred_element_type=F32)
    o_ref[...] = acc.astype(o_ref.dtype)


def _mm_scaled_kernel(a_ref, w_ref, cs_ref, o_ref):
    acc = lax.dot_general(a_ref[...], w_ref[...], NT_DIMS, preferred_element_type=F32)
    o_ref[...] = (acc * cs_ref[...]).astype(o_ref.dtype)


def _matmul(a, wt, col0, n, out_dtype, tm, tn, name, col_scale=None):
    m, k = a.shape
    assert col0 % tn == 0 and n % tn == 0 and m % tm == 0 and wt.shape[1] == k
    c0 = col0 // tn
    in_specs = [pl.BlockSpec((tm, k), lambda i, j: (i, 0)),
                pl.BlockSpec((tn, k), lambda i, j: (c0 + j, 0))]
    args = (a, wt)
    if col_scale is not None:
        in_specs.append(pl.BlockSpec((1, tn), lambda i, j: (0, j)))
        args += (col_scale,)
    return pl.pallas_call(
        _mm_kernel if col_scale is None else _mm_scaled_kernel,
        grid=(m // tm, n // tn),
        in_specs=in_specs,
        out_specs=pl.BlockSpec((tm, tn), lambda i, j: (i, j)),
        out_shape=jax.ShapeDtypeStruct((m, n), out_dtype),
        compiler_params=pltpu.CompilerParams(
            dimension_semantics=("arbitrary", "arbitrary"), vmem_limit_bytes=VMEM_LIMIT),
        name=name,
    )(*args)


def _kvprep_kernel(c_ref, t_ref, v_ref, kvw_ref, ikw_ref, ikb_ref, ckv_ref, ckvt_ref, ik_ref, iwt_ref, vt_ref):
    for blk in range(2):
        for h in range(B_HEADS):
            vh = v_ref[blk * BLK:(blk + 1) * BLK, h * B_V_DIM:(h + 1) * B_V_DIM]
            vt_ref[blk, h] = vh.astype(F32).T.astype(BF16)
    ckv = c_ref[...]
    ms = jnp.mean(ckv * ckv, axis=-1, keepdims=True)
    ckvn = ckv * lax.rsqrt(ms + EPS) * kvw_ref[...]
    ckv_ref[...] = ckvn.astype(BF16)
    ckvt_ref[0] = ckvn[:BLK].T.astype(BF16)
    ckvt_ref[1] = ckvn[BLK:].T.astype(BF16)
    tail = t_ref[:, :BLK]
    ik = tail[:, :IDX_DIM]
    mu = jnp.mean(ik, axis=-1, keepdims=True)
    var = jnp.mean(jnp.square(ik - mu), axis=-1, keepdims=True)
    ik_ref[...] = (ik - mu) * lax.rsqrt(var + EPS) * ikw_ref[...] + ikb_ref[...]
    iwt_ref[...] = (tail * (IDX_HEADS ** -0.5 * IDX_DIM ** -0.5)).T[IDX_DIM:IDX_DIM + IDX_HEADS, :]


def _kvprep(c, idxp, kv_b, vcol, kvw, ikw, ikb):
    tm = 2 * BLK
    assert vcol % B_WIDTH == 0 and c.shape[1] == KV_RANK and idxp.shape[1] == IDX_ROWS
    return pl.pallas_call(
        _kvprep_kernel,
        grid=(ROWS // tm,),
        in_specs=[pl.BlockSpec((tm, KV_RANK), lambda i: (i, 0)),
                  pl.BlockSpec((tm, HEAD_BLK), lambda i: (i, IN_SIZES[3] // HEAD_BLK)),
                  pl.BlockSpec((tm, B_WIDTH), lambda i: (i, vcol // B_WIDTH)),
                  pl.BlockSpec((1, KV_RANK), lambda i: (0, 0)),
                  pl.BlockSpec((1, IDX_DIM), lambda i: (0, 0)),
                  pl.BlockSpec((1, IDX_DIM), lambda i: (0, 0))],
        out_specs=[pl.BlockSpec((tm, KV_RANK), lambda i: (i, 0)),
                   pl.BlockSpec((2, KV_RANK, BLK), lambda i: (i, 0, 0)),
                   pl.BlockSpec((tm, IDX_DIM), lambda i: (i, 0)),
                   pl.BlockSpec((IDX_HEADS, tm), lambda i: (0, i)),
                   pl.BlockSpec((2, B_HEADS, B_V_DIM, BLK), lambda i: (i, 0, 0, 0))],
        out_shape=[jax.ShapeDtypeStruct((ROWS, KV_RANK), BF16),
                   jax.ShapeDtypeStruct((ROWS // BLK, KV_RANK, BLK), BF16),
                   jax.ShapeDtypeStruct((ROWS, IDX_DIM), F32),
                   jax.ShapeDtypeStruct((IDX_HEADS, ROWS), F32),
                   jax.ShapeDtypeStruct((ROWS // BLK, B_HEADS, B_V_DIM, BLK), BF16)],
        name="kvprep",
    )(c, idxp, kv_b, kvw, ikw, ikb)


def _split_rows_dot(lhs, rhs, dims):
    rows = lhs.shape[0]
    if rows < 2 * BLK:
        return lax.dot_general(lhs, rhs, dims, preferred_element_type=F32)
    half = rows // 2
    return jnp.concatenate([lax.dot_general(lhs[:half], rhs, dims, preferred_element_type=F32),
                            lax.dot_general(lhs[half:], rhs, dims, preferred_element_type=F32)], axis=0)


def _online_softmax_step(s, m, l):
    m_new = jnp.maximum(m, jnp.max(s, axis=0, keepdims=True))
    alpha = jnp.exp2(m - m_new)
    p = jnp.exp2(s - m_new)
    return m_new, alpha * l + jnp.sum(p, axis=0, keepdims=True), alpha, p


def _attn_a_kernel(qa_ref, iq_ref, iwt_ref, ckv_ref, ckvt_ref, ik_ref, wuk_ref, wuvt_ref, bias_ref,
                   o_ref, keys_ref, qlat_ref, acc_ref, iqt_ref):
    i = pl.program_id(1)
    nkb = i + 2
    NG = A_HEADS // 2
    GW = 2 * BLK
    t_meta = jnp.where(i == 0, T_META0, T_METAFAR)
    t_prev = jnp.where(i == 0, T_NONE, T_PREV)
    special_blocks = (0, i, i + 1)

    for h in range(A_HEADS):
        qh = qa_ref[:, h * BLK:(h + 1) * BLK]
        ql = lax.dot_general(wuk_ref[h], qh, NT_DIMS, preferred_element_type=F32)
        qlat_ref[:, h * BLK:(h + 1) * BLK] = (ql * (A_HEAD_DIM ** -0.5 * LOG2E)).astype(BF16)

    for pr in range(IDX_HEADS // 2):
        t = iq_ref[:, pr * BLK:(pr + 1) * BLK].T
        iqt_ref[:, (2 * pr) * BLK:(2 * pr + 1) * BLK] = t[:IDX_DIM]
        iqt_ref[:, (2 * pr + 1) * BLK:(2 * pr + 2) * BLK] = t[IDX_DIM:]

    iwt = iwt_ref[...]
    row = lax.broadcasted_iota(jnp.int32, (BLK, BLK), 0)
    lane = lax.broadcasted_iota(jnp.int32, (BLK, BLK), 1)

    def idx_keys(ikrows):
        sc = jnp.zeros((ikrows.shape[0], BLK), F32)
        for pr in range(IDX_HEADS // 2):
            s2 = jnp.dot(ikrows, iqt_ref[:, pr * 2 * BLK:(pr + 1) * 2 * BLK], preferred_element_type=F32)
            sc = sc + jnp.maximum(s2[:, :BLK], 0.0) * iwt[2 * pr:2 * pr + 1, :]
            sc = sc + jnp.maximum(s2[:, BLK:], 0.0) * iwt[2 * pr + 1:2 * pr + 2, :]
        bits = lax.bitcast_convert_type(sc, jnp.int32)
        return bits ^ ((bits >> 31) & 0x7FFFFFFF)

    def put_keys(off, n, key):
        keys_ref[pl.ds(off, n * BLK), :] = key

    def idx_group(kb, n, carry):
        off = pl.multiple_of(kb * BLK, BLK)
        put_keys(off, n, idx_keys(ik_ref[pl.ds(off, n * BLK), :]))
        return carry

    def idx_special(carry):
        offs = [pl.multiple_of(kb * BLK, BLK) for kb in special_blocks]
        key = idx_keys(jnp.concatenate([ik_ref[pl.ds(o, BLK), :] for o in offs], axis=0))
        put_keys(offs[1], 1, key[BLK:2 * BLK])
        put_keys(offs[2], 1, jnp.where((row >= CHUNK) & (lane < CHUNK), INT_MIN, key[2 * BLK:]))
        put_keys(offs[0], 1, jnp.where(row >= N_META, INT_MIN, key[:BLK]))
        return carry

    _visit_key_blocks(i, idx_group, idx_special, 0)
    put_keys(pl.multiple_of(nkb * BLK, BLK), FAR - 1, jnp.full(((FAR - 1) * BLK, BLK), INT_MIN, jnp.int32))
    n_search = lax.shift_right_logical(nkb + FAR - 1, 2)
    crow = lax.broadcasted_iota(jnp.int32, (FAR * BLK, BLK), 0)
    chunk_off = lambda c: pl.multiple_of(c * FAR * BLK, FAR * BLK)

    def count(pred_fn):
        def body(c, acc8):
            k = keys_ref[pl.ds(chunk_off(c), FAR * BLK), :]
            hit = pred_fn(k, chunk_off(c)).astype(jnp.int32)
            return acc8 + jnp.sum(hit.reshape(FAR * BLK // 8, 8, BLK), axis=0)
        acc8 = lax.fori_loop(0, n_search, body, jnp.zeros((8, BLK), jnp.int32))
        return jnp.sum(acc8, axis=0, keepdims=True)

    zero = jnp.zeros((1, BLK), jnp.int32)
    c0 = count(lambda k, off: k >= zero)
    prefix = jnp.where(c0 >= TOPK, 0, INT_MIN).astype(jnp.int32)

    def bit_body(t, prefix):
        cand = prefix | jnp.left_shift(jnp.int32(1), 30 - t)
        c = count(lambda k, off: k >= cand)
        return jnp.where(c >= TOPK, cand, prefix)

    thr = lax.fori_loop(0, 31, bit_body, prefix)
    full = thr == INT_MIN

    c_gt = count(lambda k, off: k > thr)
    c_eq = count(lambda k, off: k == thr)
    need = TOPK - c_gt
    tied = jnp.logical_and(jnp.logical_not(full), c_eq > need)
    j_default = jnp.where(full, -1, TP).astype(jnp.int32)

    def tie_search():
        def jbit(t, j):
            cand = j | jnp.left_shift(jnp.int32(1), 12 - t)
            c = count(lambda k, off: (k == thr) & ((off + crow) < cand))
            return jnp.where(c < need, cand, j)
        j = lax.fori_loop(0, 13, jbit, jnp.zeros((1, BLK), jnp.int32))
        return jnp.where(tied, j, j_default)

    any_tied = jnp.max(tied.astype(jnp.int32)) > 0
    jmax = lax.cond(any_tied, tie_search, lambda: j_default)

    acc_ref[...] = jnp.zeros_like(acc_ref)

    def att_update(rows, keys, rowidx, vt, bias, carry):
        sel = (keys > thr) | ((keys == thr) & (rowidx <= jmax))
        selb = jnp.where(sel, 0.0, NEG)
        add = jnp.concatenate([selb, selb], axis=1)
        gs = [slice(g * GW, (g + 1) * GW) for g in range(NG)]
        ss = [_split_rows_dot(rows, qlat_ref[:, gs[g]], NN_DIMS) for g in range(NG)]
        ss = [ss[g] + (add if bias is None else add + bias[:, gs[g]]) for g in range(NG)]
        steps = [_online_softmax_step(ss[g], *carry[g]) for g in range(NG)]
        pvs = [jnp.dot(vt, steps[g][3].astype(BF16), preferred_element_type=F32) for g in range(NG)]
        for g in range(NG):
            acc_ref[:, gs[g]] = acc_ref[:, gs[g]] * steps[g][2] + pvs[g]
        return tuple((steps[g][0], steps[g][1]) for g in range(NG))

    def att_group(kb, n, carry):
        off = pl.multiple_of(kb * BLK, BLK)
        rowidx = off + lax.broadcasted_iota(jnp.int32, (n * BLK, BLK), 0)
        vt = jnp.concatenate([ckvt_ref[kb + u] for u in range(n)], axis=1) if n > 1 else ckvt_ref[kb]
        return att_update(ckv_ref[pl.ds(off, n * BLK), :], keys_ref[pl.ds(off, n * BLK), :], rowidx, vt, None, carry)

    def att_special(carry):
        offs = [pl.multiple_of(kb * BLK, BLK) for kb in special_blocks]
        rows = jnp.concatenate([ckv_ref[pl.ds(o, BLK), :] for o in offs], axis=0)
        keys = jnp.concatenate([keys_ref[pl.ds(o, BLK), :] for o in offs], axis=0)
        rowidx = jnp.concatenate([o + row for o in offs], axis=0)
        vt = jnp.concatenate([ckvt_ref[kb] for kb in special_blocks], axis=1)
        bias = jnp.concatenate([bias_ref[t_meta], bias_ref[t_prev], bias_ref[T_DIAG]], axis=0)
        return att_update(rows, keys, rowidx, vt, bias, carry)

    m0 = jnp.full((1, GW), NEG, F32)
    l0 = jnp.zeros((1, GW), F32)
    stats = _visit_key_blocks(i, att_group, att_special, tuple((m0, l0) for _ in range(NG)))
    l = jnp.concatenate([stats[g][1] for g in range(NG)], axis=1)

    olat = (acc_ref[...] / l).astype(BF16)
    ots = [jnp.dot(wuvt_ref[h], olat[:, h * BLK:(h + 1) * BLK], preferred_element_type=F32) for h in range(A_HEADS)]
    for h in range(A_HEADS):
        o_ref[:, h * BLK:(h + 1) * BLK] = ots[h].T


def _attn_a(q_a, idxp, iwt, ckv, ckvt, ik, wuk, wuvt, bias_a):
    qrow = lambda b, i: b * NKB + 1 + i
    return pl.pallas_call(
        _attn_a_kernel,
        grid=(BATCH, NQB),
        in_specs=[
            pl.BlockSpec((BLK, A_WIDTH), lambda b, i: (qrow(b, i), 0)),
            pl.BlockSpec((BLK, IDX_HEADS * IDX_DIM), lambda b, i: (qrow(b, i), 0)),
            pl.BlockSpec((IDX_HEADS, BLK), lambda b, i: (0, qrow(b, i))),
            pl.BlockSpec((None, TP, KV_RANK), lambda b, i: (b, 0, 0)),
            pl.BlockSpec((None, NKB, KV_RANK, BLK), lambda b, i: (b, 0, 0, 0)),
            pl.BlockSpec((None, TP, IDX_DIM), lambda b, i: (b, 0, 0)),
            pl.BlockSpec((A_HEADS, KV_RANK, A_HEAD_DIM), lambda b, i: (0, 0, 0)),
            pl.BlockSpec((A_HEADS, A_HEAD_DIM, KV_RANK), lambda b, i: (0, 0, 0)),
            pl.BlockSpec((5, BLK, A_HEADS * BLK), lambda b, i: (0, 0, 0)),
        ],
        out_specs=pl.BlockSpec((BLK, A_WIDTH), lambda b, i: (b * NQB + i, 0)),
        out_shape=jax.ShapeDtypeStruct((BATCH * SEQ, A_WIDTH), F32),
        scratch_shapes=[pltpu.VMEM(((NKB + FAR - 1) * BLK, BLK), jnp.int32),
                        pltpu.VMEM((KV_RANK, A_HEADS * BLK), BF16),
                        pltpu.VMEM((KV_RANK, A_HEADS * BLK), F32),
                        pltpu.VMEM((IDX_DIM, IDX_HEADS * BLK), F32)],
        compiler_params=pltpu.CompilerParams(
            dimension_semantics=("arbitrary", "arbitrary"), vmem_limit_bytes=VMEM_LIMIT),
        name="attn_a",
    )(q_a, idxp, iwt, ckv.reshape(BATCH, TP, KV_RANK), ckvt.reshape(BATCH, NKB, KV_RANK, BLK),
      ik.reshape(BATCH, TP, IDX_DIM), wuk, wuvt, bias_a)


def _attn_b_kernel(lam_ref, q_ref, k_ref, vt_ref, bias_ref, subw_ref, o_ref, acc_ref, *, lam_init):
    i = pl.program_id(2)
    t_meta = jnp.where(i == 0, T_META0, T_METAFAR)
    t_prev = jnp.where(i == 0, T_NONE, T_PREV)
    special_blocks = (0, i, i + 1)
    lp = lam_ref[...]
    lam = (jnp.exp(jnp.sum(lp[0:1] * lp[1:2], axis=-1, keepdims=True))
           - jnp.exp(jnp.sum(lp[2:3] * lp[3:4], axis=-1, keepdims=True)) + lam_init)

    lane = lax.broadcasted_iota(jnp.int32, (BLK, BLK), 1)
    qbd = []
    for hh in range(HPS):
        q = q_ref[:, hh * BLK:(hh + 1) * BLK]
        zq = jnp.zeros_like(q)
        qbd.append(jnp.concatenate([jnp.where(lane < B_QK_DIM, q, zq), jnp.where(lane >= B_QK_DIM, q, zq)], axis=0))

    acc_ref[...] = jnp.zeros_like(acc_ref)

    def update_all(rows, vts, biases, carry):
        ss = [_split_rows_dot(rows[hh], qbd[hh], NT_DIMS) for hh in range(HPS)]
        if biases is not None:
            ss = [ss[hh] + jnp.concatenate([biases[hh], biases[hh]], axis=1) for hh in range(HPS)]
        steps = [_online_softmax_step(ss[hh], *carry[hh]) for hh in range(HPS)]
        pvs = [jnp.dot(vts[hh], steps[hh][3].astype(BF16), preferred_element_type=F32) for hh in range(HPS)]
        for hh in range(HPS):
            acc_ref[hh] = acc_ref[hh] * steps[hh][2] + pvs[hh]
        return tuple((steps[hh][0], steps[hh][1]) for hh in range(HPS))

    def group(kb, n, carry):
        off = pl.multiple_of(kb * BLK, BLK)
        rows = [k_ref[pl.ds(off, n * BLK), hh * BLK:(hh + 1) * BLK] for hh in range(HPS)]
        vts = [jnp.concatenate([vt_ref[kb + u, hh] for u in range(n)], axis=1) if n > 1 else vt_ref[kb, hh]
               for hh in range(HPS)]
        return update_all(rows, vts, None, carry)

    def special(carry):
        offs = [pl.multiple_of(kb * BLK, BLK) for kb in special_blocks]
        rows = [jnp.concatenate([k_ref[pl.ds(o, BLK), hh * BLK:(hh + 1) * BLK] for o in offs], axis=0)
                for hh in range(HPS)]
        vts = [jnp.concatenate([vt_ref[kb, hh] for kb in special_blocks], axis=1) for hh in range(HPS)]
        biases = [jnp.concatenate([bias_ref[t_meta, hh], bias_ref[t_prev, hh], bias_ref[T_DIAG, hh]], axis=0)
                  for hh in range(HPS)]
        return update_all(rows, vts, biases, carry)

    m0 = jnp.full((1, 2 * BLK), NEG, F32)
    l0 = jnp.zeros((1, 2 * BLK), F32)
    stats = _visit_key_blocks(i, group, special, tuple((m0, l0) for _ in range(HPS)), sizes=B_FAR_SIZES)

    for hh in range(HPS):
        a = acc_ref[hh] / stats[hh][1]
        o = a[:, :BLK] - lam * a[:, BLK:]
        ms = jnp.mean(o * o, axis=0, keepdims=True)
        y = o * lax.rsqrt(ms + EPS) * subw_ref[...] * (1.0 - lam_init)
        o_ref[:, hh * BLK:(hh + 1) * BLK] = y.T


def _attn_b(qkv_b, vt, bias_b, lam_p, subw, lam_init):
    qrow = lambda b, g, i: b * NKB + 1 + i
    wide = HPS * BLK
    qcol0 = 0
    kcol0 = 2 * B_HEADS * B_QK_DIM // wide
    return pl.pallas_call(
        functools.partial(_attn_b_kernel, lam_init=lam_init),
        grid=(BATCH, B_HEADS // HPS, NQB),
        in_specs=[
            pl.BlockSpec((4, B_QK_DIM), lambda b, g, i: (0, 0)),
            pl.BlockSpec((BLK, wide), lambda b, g, i: (qrow(b, g, i), qcol0 + g)),
            pl.BlockSpec((None, TP, wide), lambda b, g, i: (b, 0, kcol0 + g)),
            pl.BlockSpec((None, NKB, HPS, B_V_DIM, BLK), lambda b, g, i: (b, 0, g, 0, 0)),
            pl.BlockSpec((5, HPS, BLK, BLK), lambda b, g, i: (0, g, 0, 0)),
            pl.BlockSpec((B_V_DIM, BLK), lambda b, g, i: (0, 0)),
        ],
        out_specs=pl.BlockSpec((BLK, wide), lambda b, g, i: (b * NQB + i, g)),
        out_shape=jax.ShapeDtypeStruct((BATCH * SEQ, B_WIDTH), F32),
        scratch_shapes=[pltpu.VMEM((HPS, B_V_DIM, 2 * BLK), F32)],
        compiler_params=pltpu.CompilerParams(
            dimension_semantics=("arbitrary", "arbitrary", "arbitrary"), vmem_limit_bytes=VMEM_LIMIT),
        name="attn_b",
    )(lam_p, qkv_b, qkv_b.reshape(BATCH, TP, -1), vt, bias_b, subw)


def _out_kernel(oa_ref, za_ref, ob_ref, zb_ref, ga_ref, gb_ref, x_ref, woa_ref, wob_ref, wout_ref, pw_ref, o_ref):
    a = (oa_ref[...] * jax.nn.silu(za_ref[...])).astype(BF16)
    ya = jnp.dot(a, woa_ref[...], preferred_element_type=F32)
    b = (ob_ref[...] * jax.nn.silu(zb_ref[...])).astype(BF16)
    yb = jnp.dot(b, wob_ref[...], preferred_element_type=F32)
    mix = jax.nn.sigmoid(ga_ref[...]) * ya + jax.nn.sigmoid(gb_ref[...]) * yb
    out = jnp.dot(mix.astype(BF16), wout_ref[...], preferred_element_type=F32)
    ms = jnp.mean(out * out, axis=-1, keepdims=True)
    o_ref[...] = x_ref[...] + out * lax.rsqrt(ms + EPS) * pw_ref[...]


def _out_stage(o_a, o_b, z_a, z_b, gates, x2, woa, wob, wout, pw):
    tm = 2 * BLK
    const = lambda g: (0, 0)
    return pl.pallas_call(
        _out_kernel,
        grid=(BATCH * SEQ // tm,),
        in_specs=[
            pl.BlockSpec((tm, A_WIDTH), lambda g: (g, 0)),
            pl.BlockSpec((tm, A_WIDTH), lambda g: (g, 0)),
            pl.BlockSpec((tm, B_WIDTH), lambda g: (g, 0)),
            pl.BlockSpec((tm, B_WIDTH), lambda g: (g, 0)),
            pl.BlockSpec((tm, D_MODEL), lambda g: (g, 0)),
            pl.BlockSpec((tm, D_MODEL), lambda g: (g, 1)),
            pl.BlockSpec((tm, D_MODEL), lambda g: (g, 0)),
            pl.BlockSpec((A_WIDTH, D_MODEL), const, pipeline_mode=pl.Buffered(1)),
            pl.BlockSpec((B_WIDTH, D_MODEL), const, pipeline_mode=pl.Buffered(1)),
            pl.BlockSpec((D_MODEL, D_MODEL), const, pipeline_mode=pl.Buffered(1)),
            pl.BlockSpec((1, D_MODEL), const),
        ],
        out_specs=pl.BlockSpec((tm, D_MODEL), lambda g: (g, 0)),
        out_shape=jax.ShapeDtypeStruct((BATCH * SEQ, D_MODEL), F32),
        compiler_params=pltpu.CompilerParams(
            dimension_semantics=("arbitrary",), vmem_limit_bytes=VMEM_LIMIT),
        name="out_stage",
    )(o_a, z_a, o_b, z_b, gates, gates, x2, woa, wob, wout, pw)


def kernel(x, meta_tokens, rel_bias, pre_norm_w, w_in, kv_norm_w, w_uk, w_uv, idx_k_norm_w, idx_k_norm_b,
           diff_lambda, diff_subln_w, w_o_a, w_o_b, w_out, post_norm_w):
    assert x.shape == (BATCH, SEQ, D_MODEL) and w_in.shape[0] == 1
    layer = 0
    lam_init = 0.8 - 0.6 * math.exp(-0.3 * layer)

    meta_block = jnp.concatenate([meta_tokens.astype(F32), jnp.zeros((BLK - N_META, D_MODEL), F32)], axis=0)

    wt = w_in[0].T
    w_head, w_rest = _wprep_head(wt), _wprep_rest(wt)
    wd = A_WIDTH
    assert all(IN_SIZES[k] == wd for k in (0, 2, 3, 6, 7, 8, 9)) and IN_SIZES[10] == IN_SIZES[11] == 2 * wd
    scale_qb = jnp.concatenate([jnp.full((1, wd), B_QK_DIM ** -0.5 * LOG2E, F32), jnp.ones((1, 2 * wd), F32)], axis=1)

    u, u32, u_f = _prenorm(x.reshape(BATCH * SEQ, D_MODEL), meta_block, pre_norm_w[0][None].astype(F32))
    tm, tmf = ROWS // 8, BATCH * SEQ // 8
    q_a = _matmul(u, w_head, 0, wd, BF16, tm, wd, "proj_q_a")
    lat = _matmul(u, w_head, 2 * wd, KV_RANK, F32, tm, KV_RANK, "proj_latent")
    qkv_b = _matmul(u, w_rest, 0, 3 * wd, BF16, tm, wd, "proj_qkv_b", col_scale=scale_qb)
    w_idx = wt[IN_OFFS[3]:IN_OFFS[3] + IDX_ROWS]
    idxp = _matmul(u32, w_idx, 0, IDX_ROWS, F32, tm, IDX_ROWS // 2, "proj_indexer")
    z_a = _matmul(u_f, w_head, wd, wd, F32, tmf, wd, "proj_z_a")
    z_b = _matmul(u_f, w_rest, 3 * wd, wd, F32, tmf, wd, "proj_z_b")
    gates = _matmul(u_f, w_rest, 4 * wd, 4 * wd, F32, tmf, wd, "proj_gates")

    ckv, ckvt, ik, iwt, vt = _kvprep(lat, idxp, qkv_b, 2 * wd, kv_norm_w[0][None].astype(F32),
                                     idx_k_norm_w[0][None].astype(F32), idx_k_norm_b[0][None].astype(F32))

    bias = _bias_tiles(rel_bias) * LOG2E
    bias_a = jnp.transpose(bias[:, :A_HEADS], (0, 2, 1, 3)).reshape(5, BLK, A_HEADS * BLK)
    bias_b = bias[:, A_HEADS:]

    wuk = jnp.transpose(w_uk[0], (1, 0, 2)).astype(BF16)
    wuvt = jnp.transpose(w_uv[0], (1, 2, 0)).astype(BF16)
    o_a = _attn_a(q_a, idxp, iwt, ckv, ckvt, ik, wuk, wuvt, bias_a)

    vt = vt.reshape(BATCH, NKB, B_HEADS, B_V_DIM, BLK)
    subw = jnp.broadcast_to(diff_subln_w[0].astype(F32)[:, None], (B_V_DIM, BLK))
    o_b = _attn_b(qkv_b, vt, bias_b, diff_lambda[0].astype(F32), subw, lam_init)

    out = _out_stage(o_a, o_b, z_a, z_b, gates, x.reshape(BATCH * SEQ, D_MODEL),
                     w_o_a[0].astype(BF16), w_o_b[0].astype(BF16), w_out[0].astype(BF16),
                     post_norm_w[0][None].astype(F32))
    return out.reshape(BATCH, SEQ, D_MODEL)
```

```python
import functools
import math

import numpy as np
import jax
import jax.numpy as jnp
from jax import lax
from jax.experimental import pallas as pl
from jax.experimental.pallas import tpu as pltpu

D_MODEL = 2048
BATCH = 2
SEQ = 4096
CHUNK = 64
N_META = 16
N_BUCKETS = 32
MAX_DISTANCE = 128
A_HEADS = 8
A_HEAD_DIM = 128
KV_RANK = 256
IDX_HEADS = 16
IDX_DIM = 64
TOPK = 256
B_HEADS = 8
B_QK_DIM = 64
B_V_DIM = 128
A_WIDTH = A_HEADS * A_HEAD_DIM
B_WIDTH = B_HEADS * B_V_DIM
IN_SIZES = (A_WIDTH, KV_RANK, A_WIDTH, IDX_HEADS * IDX_DIM, IDX_DIM, IDX_HEADS,
            2 * B_HEADS * B_QK_DIM, 2 * B_HEADS * B_QK_DIM, B_WIDTH, B_WIDTH,
            D_MODEL, D_MODEL)
EPS = 1e-6

BLK = 128
NQB = SEQ // BLK
NKB = NQB + 1
TP = NKB * BLK
ROWS = BATCH * TP
FAR = 4
HPS = 8
B_FAR_SIZES = (FAR, 1)
NEG = -1e30
INT_MIN = -2 ** 31
LOG2E = math.log2(math.e)
VMEM_LIMIT = 56 * 1024 * 1024

F32 = jnp.float32
BF16 = jnp.bfloat16
NT_DIMS = (((1,), (1,)), ((), ()))
NN_DIMS = (((1,), (0,)), ((), ()))


def _t5_bucket_np(rel):
    nb = N_BUCKETS // 2
    max_exact = nb // 2
    ret = np.where(rel > 0, nb, 0)
    n = np.abs(rel)
    nf = np.maximum(n, 1).astype(np.float32)
    large = max_exact + (np.log(nf / np.float32(max_exact))
                         / np.float32(math.log(MAX_DISTANCE / max_exact))
                         * np.float32(nb - max_exact)).astype(np.int32)
    large = np.minimum(large, nb - 1)
    return ret + np.where(n < max_exact, n, large)


T_DIAG, T_PREV, T_META0, T_METAFAR, T_NONE = range(5)


def _bias_tiles(rel_bias):
    a = np.arange(BLK)[:, None]
    b = np.arange(BLK)[None, :]
    nowhere = np.zeros((BLK, BLK), bool)
    pad_rows = (a >= N_META) | nowhere
    rels = np.stack([a - b, a - b - BLK, a - N_META - b])
    dis = np.stack([(a >= CHUNK) & (b < CHUNK), nowhere, pad_rows, pad_rows, ~nowhere])
    idx = _t5_bucket_np(rels)
    far_bucket = N_BUCKETS // 2 - 1
    assert _t5_bucket_np(np.array([-BLK - 1]))[0] == far_bucket
    rb = rel_bias.astype(F32)
    heads = A_HEADS + B_HEADS
    tiles = jnp.zeros((3, heads, BLK, BLK), F32)
    for k in np.unique(idx):
        tiles = jnp.where((idx == k)[:, None], rb[k][None, :, None, None], tiles)
    tiles = tiles - rb[far_bucket][None, :, None, None]
    tiles = jnp.concatenate([tiles, jnp.zeros((2, heads, BLK, BLK), F32)], axis=0)
    return jnp.where(dis[:, None], NEG, tiles)


def _visit_key_blocks(i, group_fn, special_fn, carry, sizes=(FAR, 1)):
    n_far = jnp.maximum(i - 1, 0)
    start = jnp.int32(1)
    for size in sizes:
        shift = size.bit_length() - 1
        assert size == 1 << shift
        n_groups = lax.shift_right_logical(1 + n_far - start, shift)
        carry = lax.fori_loop(0, n_groups, lambda c, cr, s=start, z=size: group_fn(s + z * c, z, cr), carry)
        start = start + size * n_groups
    return special_fn(carry)


def _prenorm_kernel(x_ref, meta_ref, w_ref, o_ref, o32_ref, of_ref):
    def norm(x):
        ms = jnp.mean(x * x, axis=-1, keepdims=True)
        return x * lax.rsqrt(ms + EPS) * w_ref[...]

    is_meta = lax.rem(pl.program_id(0), NKB) == 0

    @pl.when(is_meta)
    def _():
        u = norm(meta_ref[...])
        o32_ref[...] = u
        o_ref[...] = u.astype(o_ref.dtype)

    @pl.when(jnp.logical_not(is_meta))
    def _():
        u = norm(x_ref[...])
        o32_ref[...] = u
        o_ref[...] = u.astype(o_ref.dtype)
        of_ref[...] = u.astype(of_ref.dtype)


def _prenorm(x2, meta_block, w):
    frame_block = lambda r: (jnp.maximum(r - r // NKB - 1, 0), 0)
    return pl.pallas_call(
        _prenorm_kernel,
        grid=(ROWS // BLK,),
        in_specs=[pl.BlockSpec((BLK, D_MODEL), frame_block),
                  pl.BlockSpec((BLK, D_MODEL), lambda r: (0, 0)),
                  pl.BlockSpec((1, D_MODEL), lambda r: (0, 0))],
        out_specs=[pl.BlockSpec((BLK, D_MODEL), lambda r: (r, 0)),
                   pl.BlockSpec((BLK, D_MODEL), lambda r: (r, 0)),
                   pl.BlockSpec((BLK, D_MODEL), frame_block)],
        out_shape=[jax.ShapeDtypeStruct((ROWS, D_MODEL), BF16),
                   jax.ShapeDtypeStruct((ROWS, D_MODEL), F32),
                   jax.ShapeDtypeStruct((BATCH * SEQ, D_MODEL), BF16)],
        compiler_params=pltpu.CompilerParams(dimension_semantics=("arbitrary",)),
        name="prenorm",
    )(x2, meta_block, w)


IN_OFFS = tuple(int(v) for v in np.concatenate([[0], np.cumsum(IN_SIZES)]))
HEAD_BLK = 2 * BLK
HEAD_ORDER = (0, 2, 1)
IDX_ROWS = IN_SIZES[3] + HEAD_BLK
W_REST = IN_OFFS[-1] - IN_OFFS[6]
REST_BLK = 4 * BLK


def _head_blocks():
    blocks = []
    for k in HEAD_ORDER:
        assert IN_OFFS[k] % HEAD_BLK == 0 and IN_SIZES[k] % HEAD_BLK == 0
        blocks += list(range(IN_OFFS[k] // HEAD_BLK, IN_OFFS[k + 1] // HEAD_BLK))
    return blocks


def _cast_kernel(w_ref, o_ref):
    o_ref[...] = w_ref[...].astype(o_ref.dtype)


def _wprep_head(wt):
    src = _head_blocks()

    def src_block(t):
        b = jnp.int32(src[-1])
        for pos in range(len(src) - 2, -1, -1):
            b = jnp.where(t == pos, src[pos], b)
        return b, 0
    return pl.pallas_call(
        _cast_kernel,
        grid=(len(src),),
        in_specs=[pl.BlockSpec((HEAD_BLK, D_MODEL), src_block)],
        out_specs=pl.BlockSpec((HEAD_BLK, D_MODEL), lambda t: (t, 0)),
        out_shape=jax.ShapeDtypeStruct((len(src) * HEAD_BLK, D_MODEL), BF16),
        compiler_params=pltpu.CompilerParams(dimension_semantics=("arbitrary",)),
        name="wprep_head",
    )(wt)


def _shift_cast_kernel(w_ref, o_ref, carry_ref, *, shift, n_out):
    s = pl.program_id(0)

    @pl.when(s > 0)
    def _():
        o_ref[...] = jnp.concatenate([carry_ref[...], w_ref[:shift]], axis=0).astype(o_ref.dtype)

    @pl.when(s < n_out)
    def _():
        carry_ref[...] = w_ref[shift:]


def _wprep_rest(wt):
    base, shift = divmod(IN_OFFS[6], REST_BLK)
    n_out = W_REST // REST_BLK
    assert shift % 8 == 0 and W_REST % REST_BLK == 0 and (base + n_out) * REST_BLK + shift == wt.shape[0]
    return pl.pallas_call(
        functools.partial(_shift_cast_kernel, shift=shift, n_out=n_out),
        grid=(n_out + 1,),
        in_specs=[pl.BlockSpec((REST_BLK, D_MODEL), lambda s: (base + s, 0))],
        out_specs=pl.BlockSpec((REST_BLK, D_MODEL), lambda s: (jnp.maximum(s - 1, 0), 0)),
        out_shape=jax.ShapeDtypeStruct((W_REST, D_MODEL), BF16),
        scratch_shapes=[pltpu.VMEM((REST_BLK - shift, D_MODEL), F32)],
        compiler_params=pltpu.CompilerParams(dimension_semantics=("arbitrary",), vmem_limit_bytes=VMEM_LIMIT),
        name="wprep_rest",
    )(wt)


def _mm_kernel(a_ref, w_ref, o_ref):
    acc = lax.dot_general(a_ref[...], w_ref[...], NT_DIMS, preferred_element_type=F32)
    o_ref[...] = acc.astype(o_ref.dtype)


def _mm_scaled_kernel(a_ref, w_ref, cs_ref, o_ref):
    acc = lax.dot_general(a_ref[...], w_ref[...], NT_DIMS, preferred_element_type=F32)
    o_ref[...] = (acc * cs_ref[...]).astype(o_ref.dtype)


def _matmul(a, wt, col0, n, out_dtype, tm, tn, name, col_scale=None):
    m, k = a.shape
    assert col0 % tn == 0 and n % tn == 0 and m % tm == 0 and wt.shape[1] == k
    c0 = col0 // tn
    resident = {"pipeline_mode": pl.Buffered(1)} if n == tn else {}
    in_specs = [pl.BlockSpec((tm, k), lambda i, j: (i, 0)),
                pl.BlockSpec((tn, k), lambda i, j: (c0 + j, 0), **resident)]
    args = (a, wt)
    if col_scale is not None:
        in_specs.append(pl.BlockSpec((1, tn), lambda i, j: (0, j)))
        args += (col_scale,)
    return pl.pallas_call(
        _mm_kernel if col_scale is None else _mm_scaled_kernel,
        grid=(m // tm, n // tn),
        in_specs=in_specs,
        out_specs=pl.BlockSpec((tm, tn), lambda i, j: (i, j)),
        out_shape=jax.ShapeDtypeStruct((m, n), out_dtype),
        compiler_params=pltpu.CompilerParams(
            dimension_semantics=("arbitrary", "arbitrary"), vmem_limit_bytes=VMEM_LIMIT),
        name=name,
    )(*args)


def _kvprep_kernel(c_ref, t_ref, v_ref, kvw_ref, ikw_ref, ikb_ref, ckv_ref, ckvt_ref, ik_ref, iwt_ref, vt_ref):
    for blk in range(2):
        for h in range(B_HEADS):
            vh = v_ref[blk * BLK:(blk + 1) * BLK, h * B_V_DIM:(h + 1) * B_V_DIM]
            vt_ref[blk, h] = vh.astype(F32).T.astype(BF16)
    ckv = c_ref[...]
    ms = jnp.mean(ckv * ckv, axis=-1, keepdims=True)
    ckvn = ckv * lax.rsqrt(ms + EPS) * kvw_ref[...]
    ckv_ref[...] = ckvn.astype(BF16)
    ckvt_ref[0] = ckvn[:BLK].T.astype(BF16)
    ckvt_ref[1] = ckvn[BLK:].T.astype(BF16)
    tail = t_ref[:, :BLK]
    ik = tail[:, :IDX_DIM]
    mu = jnp.mean(ik, axis=-1, keepdims=True)
    var = jnp.mean(jnp.square(ik - mu), axis=-1, keepdims=True)
    ik_ref[...] = (ik - mu) * lax.rsqrt(var + EPS) * ikw_ref[...] + ikb_ref[...]
    iwt_ref[...] = (tail * (IDX_HEADS ** -0.5 * IDX_DIM ** -0.5)).T[IDX_DIM:IDX_DIM + IDX_HEADS, :]


def _kvprep(c, idxp, kv_b, vcol, kvw, ikw, ikb):
    tm = 2 * BLK
    assert vcol % B_WIDTH == 0 and c.shape[1] == KV_RANK and idxp.shape[1] == IDX_ROWS
    return pl.pallas_call(
        _kvprep_kernel,
        grid=(ROWS // tm,),
        in_specs=[pl.BlockSpec((tm, KV_RANK), lambda i: (i, 0)),
                  pl.BlockSpec((tm, HEAD_BLK), lambda i: (i, IN_SIZES[3] // HEAD_BLK)),
                  pl.BlockSpec((tm, B_WIDTH), lambda i: (i, vcol // B_WIDTH)),
                  pl.BlockSpec((1, KV_RANK), lambda i: (0, 0)),
                  pl.BlockSpec((1, IDX_DIM), lambda i: (0, 0)),
                  pl.BlockSpec((1, IDX_DIM), lambda i: (0, 0))],
        out_specs=[pl.BlockSpec((tm, KV_RANK), lambda i: (i, 0)),
                   pl.BlockSpec((2, KV_RANK, BLK), lambda i: (i, 0, 0)),
                   pl.BlockSpec((tm, IDX_DIM), lambda i: (i, 0)),
                   pl.BlockSpec((IDX_HEADS, tm), lambda i: (0, i)),
                   pl.BlockSpec((2, B_HEADS, B_V_DIM, BLK), lambda i: (i, 0, 0, 0))],
        out_shape=[jax.ShapeDtypeStruct((ROWS, KV_RANK), BF16),
                   jax.ShapeDtypeStruct((ROWS // BLK, KV_RANK, BLK), BF16),
                   jax.ShapeDtypeStruct((ROWS, IDX_DIM), F32),
                   jax.ShapeDtypeStruct((IDX_HEADS, ROWS), F32),
                   jax.ShapeDtypeStruct((ROWS // BLK, B_HEADS, B_V_DIM, BLK), BF16)],
        name="kvprep",
    )(c, idxp, kv_b, kvw, ikw, ikb)


def _split_rows_dot(lhs, rhs, dims):
    rows = lhs.shape[0]
    if rows < 2 * BLK:
        return lax.dot_general(lhs, rhs, dims, preferred_element_type=F32)
    half = rows // 2
    return jnp.concatenate([lax.dot_general(lhs[:half], rhs, dims, preferred_element_type=F32),
                            lax.dot_general(lhs[half:], rhs, dims, preferred_element_type=F32)], axis=0)


def _online_softmax_step(s, m, l):
    m_new = jnp.maximum(m, jnp.max(s, axis=0, keepdims=True))
    alpha = jnp.exp2(m - m_new)
    p = jnp.exp2(s - m_new)
    return m_new, alpha * l + jnp.sum(p, axis=0, keepdims=True), alpha, p


def _attn_a_kernel(qa_ref, iq_ref, iwt_ref, ckv_ref, ckvt_ref, ik_ref, wuk_ref, wuvt_ref, bias_ref,
                   o_ref, keys_ref, qlat_ref, acc_ref, iqt_ref):
    i = pl.program_id(1)
    nkb = i + 2
    NG = A_HEADS // 2
    GW = 2 * BLK
    t_meta = jnp.where(i == 0, T_META0, T_METAFAR)
    t_prev = jnp.where(i == 0, T_NONE, T_PREV)
    special_blocks = (0, i, i + 1)

    for h in range(A_HEADS):
        qh = qa_ref[:, h * BLK:(h + 1) * BLK]
        ql = lax.dot_general(wuk_ref[h], qh, NT_DIMS, preferred_element_type=F32)
        qlat_ref[:, h * BLK:(h + 1) * BLK] = (ql * (A_HEAD_DIM ** -0.5 * LOG2E)).astype(BF16)

    for pr in range(IDX_HEADS // 2):
        t = iq_ref[:, pr * BLK:(pr + 1) * BLK].T
        iqt_ref[:, (2 * pr) * BLK:(2 * pr + 1) * BLK] = t[:IDX_DIM]
        iqt_ref[:, (2 * pr + 1) * BLK:(2 * pr + 2) * BLK] = t[IDX_DIM:]

    iwt = iwt_ref[...]
    row = lax.broadcasted_iota(jnp.int32, (BLK, BLK), 0)
    lane = lax.broadcasted_iota(jnp.int32, (BLK, BLK), 1)

    def idx_keys(ikrows):
        sc = jnp.zeros((ikrows.shape[0], BLK), F32)
        for pr in range(IDX_HEADS // 2):
            s2 = jnp.dot(ikrows, iqt_ref[:, pr * 2 * BLK:(pr + 1) * 2 * BLK], preferred_element_type=F32)
            sc = sc + jnp.maximum(s2[:, :BLK], 0.0) * iwt[2 * pr:2 * pr + 1, :]
            sc = sc + jnp.maximum(s2[:, BLK:], 0.0) * iwt[2 * pr + 1:2 * pr + 2, :]
        bits = lax.bitcast_convert_type(sc, jnp.int32)
        return bits ^ ((bits >> 31) & 0x7FFFFFFF)

    def put_keys(off, n, key):
        keys_ref[pl.ds(off, n * BLK), :] = key

    def idx_group(kb, n, carry):
        off = pl.multiple_of(kb * BLK, BLK)
        put_keys(off, n, idx_keys(ik_ref[pl.ds(off, n * BLK), :]))
        return carry

    def idx_special(carry):
        offs = [pl.multiple_of(kb * BLK, BLK) for kb in special_blocks]
        key = idx_keys(jnp.concatenate([ik_ref[pl.ds(o, BLK), :] for o in offs], axis=0))
        put_keys(offs[1], 1, key[BLK:2 * BLK])
        put_keys(offs[2], 1, jnp.where((row >= CHUNK) & (lane < CHUNK), INT_MIN, key[2 * BLK:]))
        put_keys(offs[0], 1, jnp.where(row >= N_META, INT_MIN, key[:BLK]))
        return carry

    _visit_key_blocks(i, idx_group, idx_special, 0)
    put_keys(pl.multiple_of(nkb * BLK, BLK), FAR - 1, jnp.full(((FAR - 1) * BLK, BLK), INT_MIN, jnp.int32))
    n_search = lax.shift_right_logical(nkb + FAR - 1, 2)
    crow = lax.broadcasted_iota(jnp.int32, (FAR * BLK, BLK), 0)
    chunk_off = lambda c: pl.multiple_of(c * FAR * BLK, FAR * BLK)

    def count(pred_fn):
        def body(c, acc8):
            k = keys_ref[pl.ds(chunk_off(c), FAR * BLK), :]
            hit = pred_fn(k, chunk_off(c)).astype(jnp.int32)
            return acc8 + jnp.sum(hit.reshape(FAR * BLK // 8, 8, BLK), axis=0)
        acc8 = lax.fori_loop(0, n_search, body, jnp.zeros((8, BLK), jnp.int32))
        return jnp.sum(acc8, axis=0, keepdims=True)

    zero = jnp.zeros((1, BLK), jnp.int32)
    c0 = count(lambda k, off: k >= zero)
    prefix = jnp.where(c0 >= TOPK, 0, INT_MIN).astype(jnp.int32)

    def bit_body(t, carry):
        prefix, n_ge = carry
        cand = prefix | jnp.left_shift(jnp.int32(1), 30 - t)
        c = count(lambda k, off: k >= cand)
        return jnp.where(c >= TOPK, cand, prefix), jnp.where(c >= TOPK, c, n_ge)

    thr, n_ge = lax.fori_loop(0, 31, bit_body, (prefix, c0))
    full = thr == INT_MIN
    tied = jnp.logical_and(jnp.logical_not(full), n_ge > TOPK)

    @pl.when(jnp.max(tied.astype(jnp.int32)) > 0)
    def _():
        need = TOPK - count(lambda k, off: k > thr)

        def jbit(t, j):
            cand = j | jnp.left_shift(jnp.int32(1), 12 - t)
            c = count(lambda k, off: (k == thr) & ((off + crow) < cand))
            return jnp.where(c < need, cand, j)
        jmax = lax.fori_loop(0, 13, jbit, jnp.zeros((1, BLK), jnp.int32))

        def strike(c, carry):
            k = keys_ref[pl.ds(chunk_off(c), FAR * BLK), :]
            surplus = tied & (k == thr) & ((chunk_off(c) + crow) > jmax)
            keys_ref[pl.ds(chunk_off(c), FAR * BLK), :] = jnp.where(surplus, INT_MIN, k)
            return carry
        lax.fori_loop(0, n_search, strike, 0)

    thr_sel = jnp.where(full, INT_MIN + 1, thr)

    acc_ref[...] = jnp.zeros_like(acc_ref)

    def att_update(rows, keys, vt, bias, carry):
        selb = jnp.where(keys >= thr_sel, 0.0, NEG)
        add = jnp.concatenate([selb, selb], axis=1)
        gs = [slice(g * GW, (g + 1) * GW) for g in range(NG)]
        ss = [_split_rows_dot(rows, qlat_ref[:, gs[g]], NN_DIMS) for g in range(NG)]
        ss = [ss[g] + (add if bias is None else add + bias[:, gs[g]]) for g in range(NG)]
        steps = [_online_softmax_step(ss[g], *carry[g]) for g in range(NG)]
        pvs = [jnp.dot(vt, steps[g][3].astype(BF16), preferred_element_type=F32) for g in range(NG)]
        for g in range(NG):
            acc_ref[:, gs[g]] = acc_ref[:, gs[g]] * steps[g][2] + pvs[g]
        return tuple((steps[g][0], steps[g][1]) for g in range(NG))

    def att_group(kb, n, carry):
        off = pl.multiple_of(kb * BLK, BLK)
        vt = jnp.concatenate([ckvt_ref[kb + u] for u in range(n)], axis=1) if n > 1 else ckvt_ref[kb]
        return att_update(ckv_ref[pl.ds(off, n * BLK), :], keys_ref[pl.ds(off, n * BLK), :], vt, None, carry)

    def att_special(carry):
        offs = [pl.multiple_of(kb * BLK, BLK) for kb in special_blocks]
        rows = jnp.concatenate([ckv_ref[pl.ds(o, BLK), :] for o in offs], axis=0)
        keys = jnp.concatenate([keys_ref[pl.ds(o, BLK), :] for o in offs], axis=0)
        vt = jnp.concatenate([ckvt_ref[kb] for kb in special_blocks], axis=1)
        bias = jnp.concatenate([bias_ref[t_meta], bias_ref[t_prev], bias_ref[T_DIAG]], axis=0)
        return att_update(rows, keys, vt, bias, carry)

    m0 = jnp.full((1, GW), NEG, F32)
    l0 = jnp.zeros((1, GW), F32)
    stats = _visit_key_blocks(i, att_group, att_special, tuple((m0, l0) for _ in range(NG)))
    l = jnp.concatenate([stats[g][1] for g in range(NG)], axis=1)

    olat = (acc_ref[...] / l).astype(BF16)
    ots = [jnp.dot(wuvt_ref[h], olat[:, h * BLK:(h + 1) * BLK], preferred_element_type=F32) for h in range(A_HEADS)]
    for h in range(A_HEADS):
        o_ref[:, h * BLK:(h + 1) * BLK] = ots[h].T


def _attn_a(q_a, idxp, iwt, ckv, ckvt, ik, wuk, wuvt, bias_a):
    qrow = lambda b, i: b * NKB + 1 + i
    return pl.pallas_call(
        _attn_a_kernel,
        grid=(BATCH, NQB),
        in_specs=[
            pl.BlockSpec((BLK, A_WIDTH), lambda b, i: (qrow(b, i), 0)),
            pl.BlockSpec((BLK, IDX_HEADS * IDX_DIM), lambda b, i: (qrow(b, i), 0)),
            pl.BlockSpec((IDX_HEADS, BLK), lambda b, i: (0, qrow(b, i))),
            pl.BlockSpec((None, TP, KV_RANK), lambda b, i: (b, 0, 0)),
            pl.BlockSpec((None, NKB, KV_RANK, BLK), lambda b, i: (b, 0, 0, 0)),
            pl.BlockSpec((None, TP, IDX_DIM), lambda b, i: (b, 0, 0)),
            pl.BlockSpec((A_HEADS, KV_RANK, A_HEAD_DIM), lambda b, i: (0, 0, 0)),
            pl.BlockSpec((A_HEADS, A_HEAD_DIM, KV_RANK), lambda b, i: (0, 0, 0)),
            pl.BlockSpec((5, BLK, A_HEADS * BLK), lambda b, i: (0, 0, 0)),
        ],
        out_specs=pl.BlockSpec((BLK, A_WIDTH), lambda b, i: (b * NQB + i, 0)),
        out_shape=jax.ShapeDtypeStruct((BATCH * SEQ, A_WIDTH), F32),
        scratch_shapes=[pltpu.VMEM(((NKB + FAR - 1) * BLK, BLK), jnp.int32),
                        pltpu.VMEM((KV_RANK, A_HEADS * BLK), BF16),
                        pltpu.VMEM((KV_RANK, A_HEADS * BLK), F32),
                        pltpu.VMEM((IDX_DIM, IDX_HEADS * BLK), F32)],
        compiler_params=pltpu.CompilerParams(
            dimension_semantics=("arbitrary", "arbitrary"), vmem_limit_bytes=VMEM_LIMIT),
        name="attn_a",
    )(q_a, idxp, iwt, ckv.reshape(BATCH, TP, KV_RANK), ckvt.reshape(BATCH, NKB, KV_RANK, BLK),
      ik.reshape(BATCH, TP, IDX_DIM), wuk, wuvt, bias_a)


def _attn_b_kernel(lam_ref, q_ref, k_ref, vt_ref, bias_ref, subw_ref, o_ref, acc_ref, *, lam_init):
    i = pl.program_id(2)
    t_meta = jnp.where(i == 0, T_META0, T_METAFAR)
    t_prev = jnp.where(i == 0, T_NONE, T_PREV)
    special_blocks = (0, i, i + 1)
    lp = lam_ref[...]
    lam = (jnp.exp(jnp.sum(lp[0:1] * lp[1:2], axis=-1, keepdims=True))
           - jnp.exp(jnp.sum(lp[2:3] * lp[3:4], axis=-1, keepdims=True)) + lam_init)

    lane = lax.broadcasted_iota(jnp.int32, (BLK, BLK), 1)
    qbd = []
    for hh in range(HPS):
        q = q_ref[:, hh * BLK:(hh + 1) * BLK]
        zq = jnp.zeros_like(q)
        qbd.append(jnp.concatenate([jnp.where(lane < B_QK_DIM, q, zq), jnp.where(lane >= B_QK_DIM, q, zq)], axis=0))

    acc_ref[...] = jnp.zeros_like(acc_ref)

    def update_all(rows, vts, biases, carry):
        ss = [_split_rows_dot(rows[hh], qbd[hh], NT_DIMS) for hh in range(HPS)]
        if biases is not None:
            ss = [ss[hh] + jnp.concatenate([biases[hh], biases[hh]], axis=1) for hh in range(HPS)]
        steps = [_online_softmax_step(ss[hh], *carry[hh]) for hh in range(HPS)]
        pvs = [jnp.dot(vts[hh], steps[hh][3].astype(BF16), preferred_element_type=F32) for hh in range(HPS)]
        for hh in range(HPS):
            acc_ref[hh] = acc_ref[hh] * steps[hh][2] + pvs[hh]
        return tuple((steps[hh][0], steps[hh][1]) for hh in range(HPS))

    def group(kb, n, carry):
        off = pl.multiple_of(kb * BLK, BLK)
        rows = [k_ref[pl.ds(off, n * BLK), hh * BLK:(hh + 1) * BLK] for hh in range(HPS)]
        vts = [jnp.concatenate([vt_ref[kb + u, hh] for u in range(n)], axis=1) if n > 1 else vt_ref[kb, hh]
               for hh in range(HPS)]
        return update_all(rows, vts, None, carry)

    def special(carry):
        offs = [pl.multiple_of(kb * BLK, BLK) for kb in special_blocks]
        rows = [jnp.concatenate([k_ref[pl.ds(o, BLK), hh * BLK:(hh + 1) * BLK] for o in offs], axis=0)
                for hh in range(HPS)]
        vts = [jnp.concatenate([vt_ref[kb, hh] for kb in special_blocks], axis=1) for hh in range(HPS)]
        biases = [jnp.concatenate([bias_ref[t_meta, hh], bias_ref[t_prev, hh], bias_ref[T_DIAG, hh]], axis=0)
                  for hh in range(HPS)]
        return update_all(rows, vts, biases, carry)

    m0 = jnp.full((1, 2 * BLK), NEG, F32)
    l0 = jnp.zeros((1, 2 * BLK), F32)
    stats = _visit_key_blocks(i, group, special, tuple((m0, l0) for _ in range(HPS)), sizes=B_FAR_SIZES)

    for hh in range(HPS):
        a = acc_ref[hh] / stats[hh][1]
        o = a[:, :BLK] - lam * a[:, BLK:]
        ms = jnp.mean(o * o, axis=0, keepdims=True)
        y = o * lax.rsqrt(ms + EPS) * subw_ref[...] * (1.0 - lam_init)
        o_ref[:, hh * BLK:(hh + 1) * BLK] = y.T


def _attn_b(qkv_b, vt, bias_b, lam_p, subw, lam_init):
    qrow = lambda b, g, i: b * NKB + 1 + i
    wide = HPS * BLK
    qcol0 = 0
    kcol0 = 2 * B_HEADS * B_QK_DIM // wide
    return pl.pallas_call(
        functools.partial(_attn_b_kernel, lam_init=lam_init),
        grid=(BATCH, B_HEADS // HPS, NQB),
        in_specs=[
            pl.BlockSpec((4, B_QK_DIM), lambda b, g, i: (0, 0)),
            pl.BlockSpec((BLK, wide), lambda b, g, i: (qrow(b, g, i), qcol0 + g)),
            pl.BlockSpec((None, TP, wide), lambda b, g, i: (b, 0, kcol0 + g)),
            pl.BlockSpec((None, NKB, HPS, B_V_DIM, BLK), lambda b, g, i: (b, 0, g, 0, 0)),
            pl.BlockSpec((5, HPS, BLK, BLK), lambda b, g, i: (0, g, 0, 0)),
            pl.BlockSpec((B_V_DIM, BLK), lambda b, g, i: (0, 0)),
        ],
        out_specs=pl.BlockSpec((BLK, wide), lambda b, g, i: (b * NQB + i, g)),
        out_shape=jax.ShapeDtypeStruct((BATCH * SEQ, B_WIDTH), F32),
        scratch_shapes=[pltpu.VMEM((HPS, B_V_DIM, 2 * BLK), F32)],
        compiler_params=pltpu.CompilerParams(
            dimension_semantics=("arbitrary", "arbitrary", "arbitrary"), vmem_limit_bytes=VMEM_LIMIT),
        name="attn_b",
    )(lam_p, qkv_b, qkv_b.reshape(BATCH, TP, -1), vt, bias_b, subw)


def _out_kernel(oa_ref, za_ref, ob_ref, zb_ref, ga_ref, gb_ref, x_ref, woa_ref, wob_ref, wout_ref, pw_ref, o_ref):
    a = (oa_ref[...] * jax.nn.silu(za_ref[...])).astype(BF16)
    ya = jnp.dot(a, woa_ref[...], preferred_element_type=F32)
    b = (ob_ref[...] * jax.nn.silu(zb_ref[...])).astype(BF16)
    yb = jnp.dot(b, wob_ref[...], preferred_element_type=F32)
    mix = jax.nn.sigmoid(ga_ref[...]) * ya + jax.nn.sigmoid(gb_ref[...]) * yb
    out = jnp.dot(mix.astype(BF16), wout_ref[...], preferred_element_type=F32)
    ms = jnp.mean(out * out, axis=-1, keepdims=True)
    o_ref[...] = x_ref[...] + out * lax.rsqrt(ms + EPS) * pw_ref[...]


def _out_stage(o_a, o_b, z_a, z_b, gates, x2, woa, wob, wout, pw):
    tm = 2 * BLK
    const = lambda g: (0, 0)
    return pl.pallas_call(
        _out_kernel,
        grid=(BATCH * SEQ // tm,),
        in_specs=[
            pl.BlockSpec((tm, A_WIDTH), lambda g: (g, 0)),
            pl.BlockSpec((tm, A_WIDTH), lambda g: (g, 0)),
            pl.BlockSpec((tm, B_WIDTH), lambda g: (g, 0)),
            pl.BlockSpec((tm, B_WIDTH), lambda g: (g, 0)),
            pl.BlockSpec((tm, D_MODEL), lambda g: (g, 0)),
            pl.BlockSpec((tm, D_MODEL), lambda g: (g, 1)),
            pl.BlockSpec((tm, D_MODEL), lambda g: (g, 0)),
            pl.BlockSpec((A_WIDTH, D_MODEL), const, pipeline_mode=pl.Buffered(1)),
            pl.BlockSpec((B_WIDTH, D_MODEL), const, pipeline_mode=pl.Buffered(1)),
            pl.BlockSpec((D_MODEL, D_MODEL), const, pipeline_mode=pl.Buffered(1)),
            pl.BlockSpec((1, D_MODEL), const),
        ],
        out_specs=pl.BlockSpec((tm, D_MODEL), lambda g: (g, 0)),
        out_shape=jax.ShapeDtypeStruct((BATCH * SEQ, D_MODEL), F32),
        compiler_params=pltpu.CompilerParams(
            dimension_semantics=("arbitrary",), vmem_limit_bytes=VMEM_LIMIT),
        name="out_stage",
    )(o_a, z_a, o_b, z_b, gates, gates, x2, woa, wob, wout, pw)


def kernel(x, meta_tokens, rel_bias, pre_norm_w, w_in, kv_norm_w, w_uk, w_uv, idx_k_norm_w, idx_k_norm_b,
           diff_lambda, diff_subln_w, w_o_a, w_o_b, w_out, post_norm_w):
    assert x.shape == (BATCH, SEQ, D_MODEL) and w_in.shape[0] == 1
    layer = 0
    lam_init = 0.8 - 0.6 * math.exp(-0.3 * layer)

    meta_block = jnp.concatenate([meta_tokens.astype(F32), jnp.zeros((BLK - N_META, D_MODEL), F32)], axis=0)

    wt = w_in[0].T
    w_head, w_rest = _wprep_head(wt), _wprep_rest(wt)
    wd = A_WIDTH
    assert all(IN_SIZES[k] == wd for k in (0, 2, 3, 6, 7, 8, 9)) and IN_SIZES[10] == IN_SIZES[11] == 2 * wd
    scale_qb = jnp.concatenate([jnp.full((1, wd), B_QK_DIM ** -0.5 * LOG2E, F32), jnp.ones((1, 2 * wd), F32)], axis=1)

    u, u32, u_f = _prenorm(x.reshape(BATCH * SEQ, D_MODEL), meta_block, pre_norm_w[0][None].astype(F32))
    tm, tmf = ROWS // 8, BATCH * SEQ // 8
    q_a = _matmul(u, w_head, 0, wd, BF16, tm, wd, "proj_q_a")
    lat = _matmul(u, w_head, 2 * wd, KV_RANK, F32, tm, KV_RANK, "proj_latent")
    qkv_b = _matmul(u, w_rest, 0, 3 * wd, BF16, tm, wd, "proj_qkv_b", col_scale=scale_qb)
    w_idx = wt[IN_OFFS[3]:IN_OFFS[3] + IDX_ROWS]
    idxp = _matmul(u32, w_idx, 0, IDX_ROWS, F32, tm, IDX_ROWS, "proj_indexer")
    z_a = _matmul(u_f, w_head, wd, wd, F32, tmf, wd, "proj_z_a")
    z_b = _matmul(u_f, w_rest, 3 * wd, wd, F32, tmf, wd, "proj_z_b")
    gates = _matmul(u_f, w_rest, 4 * wd, 4 * wd, F32, tmf, wd, "proj_gates")

    ckv, ckvt, ik, iwt, vt = _kvprep(lat, idxp, qkv_b, 2 * wd, kv_norm_w[0][None].astype(F32),
                                     idx_k_norm_w[0][None].astype(F32), idx_k_norm_b[0][None].astype(F32))

    bias = _bias_tiles(rel_bias) * LOG2E
    bias_a = jnp.transpose(bias[:, :A_HEADS], (0, 2, 1, 3)).reshape(5, BLK, A_HEADS * BLK)
    bias_b = bias[:, A_HEADS:]

    wuk = jnp.transpose(w_uk[0], (1, 0, 2)).astype(BF16)
    wuvt = jnp.transpose(w_uv[0], (1, 2, 0)).astype(BF16)
    o_a = _attn_a(q_a, idxp, iwt, ckv, ckvt, ik, wuk, wuvt, bias_a)

    vt = vt.reshape(BATCH, NKB, B_HEADS, B_V_DIM, BLK)
    subw = jnp.broadcast_to(diff_subln_w[0].astype(F32)[:, None], (B_V_DIM, BLK))
    o_b = _attn_b(qkv_b, vt, bias_b, diff_lambda[0].astype(F32), subw, lam_init)

    out = _out_stage(o_a, o_b, z_a, z_b, gates, x.reshape(BATCH * SEQ, D_MODEL),
                     w_o_a[0].astype(BF16), w_o_b[0].astype(BF16), w_out[0].astype(BF16),
                     post_norm_w[0][None].astype(F32))
    return out.reshape(BATCH, SEQ, D_MODEL)
```

```python
import functools
import math

import numpy as np
import jax
import jax.numpy as jnp
from jax import lax
from jax.experimental import pallas as pl
from jax.experimental.pallas import tpu as pltpu

D_MODEL = 2048
BATCH = 2
SEQ = 4096
CHUNK = 64
N_META = 16
N_BUCKETS = 32
MAX_DISTANCE = 128
A_HEADS = 8
A_HEAD_DIM = 128
KV_RANK = 256
IDX_HEADS = 16
IDX_DIM = 64
TOPK = 256
B_HEADS = 8
B_QK_DIM = 64
B_V_DIM = 128
A_WIDTH = A_HEADS * A_HEAD_DIM
B_WIDTH = B_HEADS * B_V_DIM
IN_SIZES = (A_WIDTH, KV_RANK, A_WIDTH, IDX_HEADS * IDX_DIM, IDX_DIM, IDX_HEADS,
            2 * B_HEADS * B_QK_DIM, 2 * B_HEADS * B_QK_DIM, B_WIDTH, B_WIDTH,
            D_MODEL, D_MODEL)
EPS = 1e-6

BLK = 128
NQB = SEQ // BLK
NKB = NQB + 1
TP = NKB * BLK
ROWS = BATCH * TP
FAR = 4
HPS = 8
B_FAR_SIZES = (FAR,)
NEG = -1e30
INT_MIN = -2 ** 31
LOG2E = math.log2(math.e)
VMEM_LIMIT = 56 * 1024 * 1024

F32 = jnp.float32
BF16 = jnp.bfloat16
NT_DIMS = (((1,), (1,)), ((), ()))
NN_DIMS = (((1,), (0,)), ((), ()))


def _t5_bucket_np(rel):
    nb = N_BUCKETS // 2
    max_exact = nb // 2
    ret = np.where(rel > 0, nb, 0)
    n = np.abs(rel)
    nf = np.maximum(n, 1).astype(np.float32)
    large = max_exact + (np.log(nf / np.float32(max_exact))
                         / np.float32(math.log(MAX_DISTANCE / max_exact))
                         * np.float32(nb - max_exact)).astype(np.int32)
    large = np.minimum(large, nb - 1)
    return ret + np.where(n < max_exact, n, large)


T_DIAG, T_PREV, T_META0, T_METAFAR, T_NONE = range(5)


def _bias_tiles(rel_bias):
    a = np.arange(BLK)[:, None]
    b = np.arange(BLK)[None, :]
    nowhere = np.zeros((BLK, BLK), bool)
    pad_rows = (a >= N_META) | nowhere
    rels = np.stack([a - b, a - b - BLK, a - N_META - b])
    dis = np.stack([(a >= CHUNK) & (b < CHUNK), nowhere, pad_rows, pad_rows, ~nowhere])
    idx = _t5_bucket_np(rels)
    far_bucket = N_BUCKETS // 2 - 1
    assert _t5_bucket_np(np.array([-BLK - 1]))[0] == far_bucket
    rb = rel_bias.astype(F32)
    heads = A_HEADS + B_HEADS
    tiles = jnp.zeros((3, heads, BLK, BLK), F32)
    for k in np.unique(idx):
        tiles = jnp.where((idx == k)[:, None], rb[k][None, :, None, None], tiles)
    tiles = tiles - rb[far_bucket][None, :, None, None]
    tiles = jnp.concatenate([tiles, jnp.zeros((2, heads, BLK, BLK), F32)], axis=0)
    return jnp.where(dis[:, None], NEG, tiles)


def _visit_key_blocks(i, group_fn, special_fn, carry, sizes=(FAR,)):
    n_far = jnp.maximum(i - 1, 0)
    start = jnp.int32(1)
    for size in sizes:
        shift = size.bit_length() - 1
        assert size == 1 << shift
        n_groups = lax.shift_right_logical(1 + n_far - start, shift)
        carry = lax.fori_loop(0, n_groups, lambda c, cr, s=start, z=size: group_fn(s + z * c, z, cr), carry)
        start = start + size * n_groups
    branches = [functools.partial(special_fn, first=start, n_left=r) for r in range(sizes[-1])]
    return lax.switch(1 + n_far - start, branches, carry)


def _prenorm_kernel(x_ref, meta_ref, w_ref, o_ref, o32_ref, of_ref):
    def norm(x):
        ms = jnp.mean(x * x, axis=-1, keepdims=True)
        return x * lax.rsqrt(ms + EPS) * w_ref[...]

    is_meta = lax.rem(pl.program_id(0), NKB) == 0

    @pl.when(is_meta)
    def _():
        u = norm(meta_ref[...])
        o32_ref[...] = u
        o_ref[...] = u.astype(o_ref.dtype)

    @pl.when(jnp.logical_not(is_meta))
    def _():
        u = norm(x_ref[...])
        o32_ref[...] = u
        o_ref[...] = u.astype(o_ref.dtype)
        of_ref[...] = u.astype(of_ref.dtype)


def _prenorm(x2, meta_block, w):
    frame_block = lambda r: (jnp.maximum(r - r // NKB - 1, 0), 0)
    return pl.pallas_call(
        _prenorm_kernel,
        grid=(ROWS // BLK,),
        in_specs=[pl.BlockSpec((BLK, D_MODEL), frame_block),
                  pl.BlockSpec((BLK, D_MODEL), lambda r: (0, 0)),
                  pl.BlockSpec((1, D_MODEL), lambda r: (0, 0))],
        out_specs=[pl.BlockSpec((BLK, D_MODEL), lambda r: (r, 0)),
                   pl.BlockSpec((BLK, D_MODEL), lambda r: (r, 0)),
                   pl.BlockSpec((BLK, D_MODEL), frame_block)],
        out_shape=[jax.ShapeDtypeStruct((ROWS, D_MODEL), BF16),
                   jax.ShapeDtypeStruct((ROWS, D_MODEL), F32),
                   jax.ShapeDtypeStruct((BATCH * SEQ, D_MODEL), BF16)],
        compiler_params=pltpu.CompilerParams(dimension_semantics=("arbitrary",)),
        name="prenorm",
    )(x2, meta_block, w)


IN_OFFS = tuple(int(v) for v in np.concatenate([[0], np.cumsum(IN_SIZES)]))
HEAD_BLK = 2 * BLK
HEAD_ORDER = (0, 2, 1)
IDX_ROWS = IN_SIZES[3] + HEAD_BLK
W_REST = IN_OFFS[-1] - IN_OFFS[6]
REST_BLK = 4 * BLK


def _head_blocks():
    blocks = []
    for k in HEAD_ORDER:
        assert IN_OFFS[k] % HEAD_BLK == 0 and IN_SIZES[k] % HEAD_BLK == 0
        blocks += list(range(IN_OFFS[k] // HEAD_BLK, IN_OFFS[k + 1] // HEAD_BLK))
    return blocks


def _cast_kernel(w_ref, o_ref):
    o_ref[...] = w_ref[...].astype(o_ref.dtype)


def _wprep_head(wt):
    src = _head_blocks()

    def src_block(t):
        b = jnp.int32(src[-1])
        for pos in range(len(src) - 2, -1, -1):
            b = jnp.where(t == pos, src[pos], b)
        return b, 0
    return pl.pallas_call(
        _cast_kernel,
        grid=(len(src),),
        in_specs=[pl.BlockSpec((HEAD_BLK, D_MODEL), src_block)],
        out_specs=pl.BlockSpec((HEAD_BLK, D_MODEL), lambda t: (t, 0)),
        out_shape=jax.ShapeDtypeStruct((len(src) * HEAD_BLK, D_MODEL), BF16),
        compiler_params=pltpu.CompilerParams(dimension_semantics=("arbitrary",)),
        name="wprep_head",
    )(wt)


def _shift_cast_kernel(w_ref, o_ref, carry_ref, *, shift, n_out):
    s = pl.program_id(0)

    @pl.when(s > 0)
    def _():
        o_ref[...] = jnp.concatenate([carry_ref[...], w_ref[:shift]], axis=0).astype(o_ref.dtype)

    @pl.when(s < n_out)
    def _():
        carry_ref[...] = w_ref[shift:]


def _wprep_rest(wt):
    base, shift = divmod(IN_OFFS[6], REST_BLK)
    n_out = W_REST // REST_BLK
    assert shift % 8 == 0 and W_REST % REST_BLK == 0 and (base + n_out) * REST_BLK + shift == wt.shape[0]
    return pl.pallas_call(
        functools.partial(_shift_cast_kernel, shift=shift, n_out=n_out),
        grid=(n_out + 1,),
        in_specs=[pl.BlockSpec((REST_BLK, D_MODEL), lambda s: (base + s, 0))],
        out_specs=pl.BlockSpec((REST_BLK, D_MODEL), lambda s: (jnp.maximum(s - 1, 0), 0)),
        out_shape=jax.ShapeDtypeStruct((W_REST, D_MODEL), BF16),
        scratch_shapes=[pltpu.VMEM((REST_BLK - shift, D_MODEL), F32)],
        compiler_params=pltpu.CompilerParams(dimension_semantics=("arbitrary",), vmem_limit_bytes=VMEM_LIMIT),
        name="wprep_rest",
    )(wt)


def _mm_kernel(a_ref, w_ref, o_ref):
    acc = lax.dot_general(a_ref[...], w_ref[...], NT_DIMS, preferred_element_type=F32)
    o_ref[...] = acc.astype(o_ref.dtype)


def _mm_scaled_kernel(a_ref, w_ref, cs_ref, o_ref):
    acc = lax.dot_general(a_ref[...], w_ref[...], NT_DIMS, preferred_element_type=F32)
    o_ref[...] = (acc * cs_ref[...]).astype(o_ref.dtype)


def _matmul(a, wt, col0, n, out_dtype, tm, tn, name, col_scale=None):
    m, k = a.shape
    assert col0 % tn == 0 and n % tn == 0 and m % tm == 0 and wt.shape[1] == k
    c0 = col0 // tn
    resident = {"pipeline_mode": pl.Buffered(1)} if n == tn else {}
    in_specs = [pl.BlockSpec((tm, k), lambda i, j: (i, 0)),
                pl.BlockSpec((tn, k), lambda i, j: (c0 + j, 0), **resident)]
    args = (a, wt)
    if col_scale is not None:
        in_specs.append(pl.BlockSpec((1, tn), lambda i, j: (0, j)))
        args += (col_scale,)
    return pl.pallas_call(
        _mm_kernel if col_scale is None else _mm_scaled_kernel,
        grid=(m // tm, n // tn),
        in_specs=in_specs,
        out_specs=pl.BlockSpec((tm, tn), lambda i, j: (i, j)),
        out_shape=jax.ShapeDtypeStruct((m, n), out_dtype),
        compiler_params=pltpu.CompilerParams(
            dimension_semantics=("arbitrary", "arbitrary"), vmem_limit_bytes=VMEM_LIMIT),
        name=name,
    )(*args)


def _kvprep_kernel(c_ref, t_ref, v_ref, kvw_ref, ikw_ref, ikb_ref, ckv_ref, ckvt_ref, ik_ref, iwt_ref, vt_ref):
    for blk in range(2):
        for h in range(B_HEADS):
            vh = v_ref[blk * BLK:(blk + 1) * BLK, h * B_V_DIM:(h + 1) * B_V_DIM]
            vt_ref[blk, h] = vh.astype(F32).T.astype(BF16)
    ckv = c_ref[...]
    ms = jnp.mean(ckv * ckv, axis=-1, keepdims=True)
    ckvn = ckv * lax.rsqrt(ms + EPS) * kvw_ref[...]
    ckv_ref[...] = ckvn.astype(BF16)
    ckvt_ref[0] = ckvn[:BLK].T.astype(BF16)
    ckvt_ref[1] = ckvn[BLK:].T.astype(BF16)
    tail = t_ref[:, :BLK]
    ik = tail[:, :IDX_DIM]
    mu = jnp.mean(ik, axis=-1, keepdims=True)
    var = jnp.mean(jnp.square(ik - mu), axis=-1, keepdims=True)
    ik_ref[...] = (ik - mu) * lax.rsqrt(var + EPS) * ikw_ref[...] + ikb_ref[...]
    iwt_ref[...] = (tail * (IDX_HEADS ** -0.5 * IDX_DIM ** -0.5)).T[IDX_DIM:IDX_DIM + IDX_HEADS, :]


def _kvprep(c, idxp, kv_b, vcol, kvw, ikw, ikb):
    tm = 2 * BLK
    assert vcol % B_WIDTH == 0 and c.shape[1] == KV_RANK and idxp.shape[1] == IDX_ROWS
    return pl.pallas_call(
        _kvprep_kernel,
        grid=(ROWS // tm,),
        in_specs=[pl.BlockSpec((tm, KV_RANK), lambda i: (i, 0)),
                  pl.BlockSpec((tm, HEAD_BLK), lambda i: (i, IN_SIZES[3] // HEAD_BLK)),
                  pl.BlockSpec((tm, B_WIDTH), lambda i: (i, vcol // B_WIDTH)),
                  pl.BlockSpec((1, KV_RANK), lambda i: (0, 0)),
                  pl.BlockSpec((1, IDX_DIM), lambda i: (0, 0)),
                  pl.BlockSpec((1, IDX_DIM), lambda i: (0, 0))],
        out_specs=[pl.BlockSpec((tm, KV_RANK), lambda i: (i, 0)),
                   pl.BlockSpec((2, KV_RANK, BLK), lambda i: (i, 0, 0)),
                   pl.BlockSpec((tm, IDX_DIM), lambda i: (i, 0)),
                   pl.BlockSpec((IDX_HEADS, tm), lambda i: (0, i)),
                   pl.BlockSpec((2, B_HEADS, B_V_DIM, BLK), lambda i: (i, 0, 0, 0))],
        out_shape=[jax.ShapeDtypeStruct((ROWS, KV_RANK), BF16),
                   jax.ShapeDtypeStruct((ROWS // BLK, KV_RANK, BLK), BF16),
                   jax.ShapeDtypeStruct((ROWS, IDX_DIM), F32),
                   jax.ShapeDtypeStruct((IDX_HEADS, ROWS), F32),
                   jax.ShapeDtypeStruct((ROWS // BLK, B_HEADS, B_V_DIM, BLK), BF16)],
        name="kvprep",
    )(c, idxp, kv_b, kvw, ikw, ikb)


def _split_rows_dot(lhs, rhs, dims):
    rows = lhs.shape[0]
    if rows < 2 * BLK:
        return lax.dot_general(lhs, rhs, dims, preferred_element_type=F32)
    half = rows // 2
    return jnp.concatenate([lax.dot_general(lhs[:half], rhs, dims, preferred_element_type=F32),
                            lax.dot_general(lhs[half:], rhs, dims, preferred_element_type=F32)], axis=0)


def _online_softmax_step(s, m, l):
    m_new = jnp.maximum(m, jnp.max(s, axis=0, keepdims=True))
    alpha = jnp.exp2(m - m_new)
    p = jnp.exp2(s - m_new)
    return m_new, alpha * l + jnp.sum(p, axis=0, keepdims=True), alpha, p


def _attn_a_kernel(qa_ref, iq_ref, iwt_ref, ckv_ref, ckvt_ref, ik_ref, wuk_ref, wuvt_ref, bias_ref,
                   o_ref, keys_ref, qlat_ref, acc_ref, iqt_ref):
    i = pl.program_id(1)
    nkb = i + 2
    NG = A_HEADS // 2
    GW = 2 * BLK
    t_meta = jnp.where(i == 0, T_META0, T_METAFAR)
    t_prev = jnp.where(i == 0, T_NONE, T_PREV)
    special_blocks = (0, i, i + 1)

    for h in range(A_HEADS):
        qh = qa_ref[:, h * BLK:(h + 1) * BLK]
        ql = lax.dot_general(wuk_ref[h], qh, NT_DIMS, preferred_element_type=F32)
        qlat_ref[:, h * BLK:(h + 1) * BLK] = (ql * (A_HEAD_DIM ** -0.5 * LOG2E)).astype(BF16)

    for pr in range(IDX_HEADS // 2):
        t = iq_ref[:, pr * BLK:(pr + 1) * BLK].T
        iqt_ref[:, (2 * pr) * BLK:(2 * pr + 1) * BLK] = t[:IDX_DIM]
        iqt_ref[:, (2 * pr + 1) * BLK:(2 * pr + 2) * BLK] = t[IDX_DIM:]

    iwt = iwt_ref[...]
    row = lax.broadcasted_iota(jnp.int32, (BLK, BLK), 0)
    lane = lax.broadcasted_iota(jnp.int32, (BLK, BLK), 1)

    def idx_keys(ikrows):
        sc = jnp.zeros((ikrows.shape[0], BLK), F32)
        for pr in range(IDX_HEADS // 2):
            s2 = jnp.dot(ikrows, iqt_ref[:, pr * 2 * BLK:(pr + 1) * 2 * BLK], preferred_element_type=F32)
            sc = sc + jnp.maximum(s2[:, :BLK], 0.0) * iwt[2 * pr:2 * pr + 1, :]
            sc = sc + jnp.maximum(s2[:, BLK:], 0.0) * iwt[2 * pr + 1:2 * pr + 2, :]
        bits = lax.bitcast_convert_type(sc, jnp.int32)
        return bits ^ ((bits >> 31) & 0x7FFFFFFF)

    def put_keys(off, n, key):
        keys_ref[pl.ds(off, n * BLK), :] = key

    def idx_group(kb, n, carry):
        off = pl.multiple_of(kb * BLK, BLK)
        put_keys(off, n, idx_keys(ik_ref[pl.ds(off, n * BLK), :]))
        return carry

    def idx_special(carry, first, n_left):
        offs = [pl.multiple_of(kb * BLK, BLK) for kb in special_blocks]
        left_off = pl.multiple_of(first * BLK, BLK)
        left = [ik_ref[pl.ds(left_off, n_left * BLK), :]] if n_left else []
        key = idx_keys(jnp.concatenate(left + [ik_ref[pl.ds(o, BLK), :] for o in offs], axis=0))
        if n_left:
            put_keys(left_off, n_left, key[:n_left * BLK])
        key = key[n_left * BLK:]
        put_keys(offs[1], 1, key[BLK:2 * BLK])
        put_keys(offs[2], 1, jnp.where((row >= CHUNK) & (lane < CHUNK), INT_MIN, key[2 * BLK:]))
        put_keys(offs[0], 1, jnp.where(row >= N_META, INT_MIN, key[:BLK]))
        return carry

    _visit_key_blocks(i, idx_group, idx_special, 0)
    put_keys(pl.multiple_of(nkb * BLK, BLK), FAR - 1, jnp.full(((FAR - 1) * BLK, BLK), INT_MIN, jnp.int32))
    n_search = lax.shift_right_logical(nkb + FAR - 1, 2)
    crow = lax.broadcasted_iota(jnp.int32, (FAR * BLK, BLK), 0)
    chunk_off = lambda c: pl.multiple_of(c * FAR * BLK, FAR * BLK)

    def count(pred_fn):
        def body(c, acc8):
            k = keys_ref[pl.ds(chunk_off(c), FAR * BLK), :]
            hit = pred_fn(k, chunk_off(c)).astype(jnp.int32)
            return acc8 + jnp.sum(hit.reshape(FAR * BLK // 8, 8, BLK), axis=0)
        acc8 = lax.fori_loop(0, n_search, body, jnp.zeros((8, BLK), jnp.int32))
        return jnp.sum(acc8, axis=0, keepdims=True)

    zero = jnp.zeros((1, BLK), jnp.int32)
    c0 = count(lambda k, off: k >= zero)
    prefix = jnp.where(c0 >= TOPK, 0, INT_MIN).astype(jnp.int32)

    def bit_body(t, carry):
        prefix, n_ge = carry
        cand = prefix | jnp.left_shift(jnp.int32(1), 30 - t)
        c = count(lambda k, off: k >= cand)
        return jnp.where(c >= TOPK, cand, prefix), jnp.where(c >= TOPK, c, n_ge)

    thr, n_ge = lax.fori_loop(0, 31, bit_body, (prefix, c0))
    full = thr == INT_MIN
    tied = jnp.logical_and(jnp.logical_not(full), n_ge > TOPK)

    @pl.when(jnp.max(tied.astype(jnp.int32)) > 0)
    def _():
        need = TOPK - count(lambda k, off: k > thr)

        def jbit(t, j):
            cand = j | jnp.left_shift(jnp.int32(1), 12 - t)
            c = count(lambda k, off: (k == thr) & ((off + crow) < cand))
            return jnp.where(c < need, cand, j)
        jmax = lax.fori_loop(0, 13, jbit, jnp.zeros((1, BLK), jnp.int32))

        def strike(c, carry):
            k = keys_ref[pl.ds(chunk_off(c), FAR * BLK), :]
            surplus = tied & (k == thr) & ((chunk_off(c) + crow) > jmax)
            keys_ref[pl.ds(chunk_off(c), FAR * BLK), :] = jnp.where(surplus, INT_MIN, k)
            return carry
        lax.fori_loop(0, n_search, strike, 0)

    thr_sel = jnp.where(full, INT_MIN + 1, thr)

    acc_ref[...] = jnp.zeros_like(acc_ref)

    def att_update(rows, keys, vt, bias, carry):
        selb = jnp.where(keys >= thr_sel, 0.0, NEG)
        add = jnp.concatenate([selb, selb], axis=1)
        gs = [slice(g * GW, (g + 1) * GW) for g in range(NG)]
        ss = [_split_rows_dot(rows, qlat_ref[:, gs[g]], NN_DIMS) for g in range(NG)]
        ss = [ss[g] + (add if bias is None else add + bias[:, gs[g]]) for g in range(NG)]
        steps = [_online_softmax_step(ss[g], *carry[g]) for g in range(NG)]
        pvs = [jnp.dot(vt, steps[g][3].astype(BF16), preferred_element_type=F32) for g in range(NG)]
        for g in range(NG):
            acc_ref[:, gs[g]] = acc_ref[:, gs[g]] * steps[g][2] + pvs[g]
        return tuple((steps[g][0], steps[g][1]) for g in range(NG))

    def att_group(kb, n, carry):
        off = pl.multiple_of(kb * BLK, BLK)
        vt = jnp.concatenate([ckvt_ref[kb + u] for u in range(n)], axis=1) if n > 1 else ckvt_ref[kb]
        return att_update(ckv_ref[pl.ds(off, n * BLK), :], keys_ref[pl.ds(off, n * BLK), :], vt, None, carry)

    def att_special(carry, first, n_left):
        offs = [pl.multiple_of(kb * BLK, BLK) for kb in special_blocks]
        left_off = pl.multiple_of(first * BLK, BLK)
        left = lambda ref: [ref[pl.ds(left_off, n_left * BLK), :]] if n_left else []
        rows = jnp.concatenate(left(ckv_ref) + [ckv_ref[pl.ds(o, BLK), :] for o in offs], axis=0)
        keys = jnp.concatenate(left(keys_ref) + [keys_ref[pl.ds(o, BLK), :] for o in offs], axis=0)
        vt = jnp.concatenate([ckvt_ref[first + u] for u in range(n_left)] + [ckvt_ref[kb] for kb in special_blocks],
                             axis=1)
        no_bias = [jnp.zeros((n_left * BLK, A_HEADS * BLK), F32)] if n_left else []
        bias = jnp.concatenate(no_bias + [bias_ref[t_meta], bias_ref[t_prev], bias_ref[T_DIAG]], axis=0)
        return att_update(rows, keys, vt, bias, carry)

    m0 = jnp.full((1, GW), NEG, F32)
    l0 = jnp.zeros((1, GW), F32)
    stats = _visit_key_blocks(i, att_group, att_special, tuple((m0, l0) for _ in range(NG)))
    l = jnp.concatenate([stats[g][1] for g in range(NG)], axis=1)

    olat = (acc_ref[...] / l).astype(BF16)
    ots = [jnp.dot(wuvt_ref[h], olat[:, h * BLK:(h + 1) * BLK], preferred_element_type=F32) for h in range(A_HEADS)]
    for h in range(A_HEADS):
        o_ref[:, h * BLK:(h + 1) * BLK] = ots[h].T


def _attn_a(q_a, idxp, iwt, ckv, ckvt, ik, wuk, wuvt, bias_a):
    qrow = lambda b, i: b * NKB + 1 + i
    return pl.pallas_call(
        _attn_a_kernel,
        grid=(BATCH, NQB),
        in_specs=[
            pl.BlockSpec((BLK, A_WIDTH), lambda b, i: (qrow(b, i), 0)),
            pl.BlockSpec((BLK, IDX_HEADS * IDX_DIM), lambda b, i: (qrow(b, i), 0)),
            pl.BlockSpec((IDX_HEADS, BLK), lambda b, i: (0, qrow(b, i))),
            pl.BlockSpec((None, TP, KV_RANK), lambda b, i: (b, 0, 0)),
            pl.BlockSpec((None, NKB, KV_RANK, BLK), lambda b, i: (b, 0, 0, 0)),
            pl.BlockSpec((None, TP, IDX_DIM), lambda b, i: (b, 0, 0)),
            pl.BlockSpec((A_HEADS, KV_RANK, A_HEAD_DIM), lambda b, i: (0, 0, 0)),
            pl.BlockSpec((A_HEADS, A_HEAD_DIM, KV_RANK), lambda b, i: (0, 0, 0)),
            pl.BlockSpec((5, BLK, A_HEADS * BLK), lambda b, i: (0, 0, 0)),
        ],
        out_specs=pl.BlockSpec((BLK, A_WIDTH), lambda b, i: (b * NQB + i, 0)),
        out_shape=jax.ShapeDtypeStruct((BATCH * SEQ, A_WIDTH), F32),
        scratch_shapes=[pltpu.VMEM(((NKB + FAR - 1) * BLK, BLK), jnp.int32),
                        pltpu.VMEM((KV_RANK, A_HEADS * BLK), BF16),
                        pltpu.VMEM((KV_RANK, A_HEADS * BLK), F32),
                        pltpu.VMEM((IDX_DIM, IDX_HEADS * BLK), F32)],
        compiler_params=pltpu.CompilerParams(
            dimension_semantics=("arbitrary", "arbitrary"), vmem_limit_bytes=VMEM_LIMIT),
        name="attn_a",
    )(q_a, idxp, iwt, ckv.reshape(BATCH, TP, KV_RANK), ckvt.reshape(BATCH, NKB, KV_RANK, BLK),
      ik.reshape(BATCH, TP, IDX_DIM), wuk, wuvt, bias_a)


def _attn_b_kernel(lam_ref, q_ref, k_ref, vt_ref, bias_ref, subw_ref, o_ref, acc_ref, *, lam_init):
    i = pl.program_id(2)
    t_meta = jnp.where(i == 0, T_META0, T_METAFAR)
    t_prev = jnp.where(i == 0, T_NONE, T_PREV)
    special_blocks = (0, i, i + 1)
    lp = lam_ref[...]
    lam = (jnp.exp(jnp.sum(lp[0:1] * lp[1:2], axis=-1, keepdims=True))
           - jnp.exp(jnp.sum(lp[2:3] * lp[3:4], axis=-1, keepdims=True)) + lam_init)

    lane = lax.broadcasted_iota(jnp.int32, (BLK, BLK), 1)
    qbd = []
    for hh in range(HPS):
        q = q_ref[:, hh * BLK:(hh + 1) * BLK]
        zq = jnp.zeros_like(q)
        qbd.append(jnp.concatenate([jnp.where(lane < B_QK_DIM, q, zq), jnp.where(lane >= B_QK_DIM, q, zq)], axis=0))

    acc_ref[...] = jnp.zeros_like(acc_ref)

    def update_all(rows, vts, biases, carry):
        ss = [_split_rows_dot(rows[hh], qbd[hh], NT_DIMS) for hh in range(HPS)]
        if biases is not None:
            ss = [ss[hh] + jnp.concatenate([biases[hh], biases[hh]], axis=1) for hh in range(HPS)]
        steps = [_online_softmax_step(ss[hh], *carry[hh]) for hh in range(HPS)]
        pvs = [jnp.dot(vts[hh], steps[hh][3].astype(BF16), preferred_element_type=F32) for hh in range(HPS)]
        for hh in range(HPS):
            acc_ref[hh] = acc_ref[hh] * steps[hh][2] + pvs[hh]
        return tuple((steps[hh][0], steps[hh][1]) for hh in range(HPS))

    def group(kb, n, carry):
        off = pl.multiple_of(kb * BLK, BLK)
        rows = [k_ref[pl.ds(off, n * BLK), hh * BLK:(hh + 1) * BLK] for hh in range(HPS)]
        vts = [jnp.concatenate([vt_ref[kb + u, hh] for u in range(n)], axis=1) if n > 1 else vt_ref[kb, hh]
               for hh in range(HPS)]
        return update_all(rows, vts, None, carry)

    def special(carry, first, n_left):
        offs = [pl.multiple_of(kb * BLK, BLK) for kb in special_blocks]
        left_off = pl.multiple_of(first * BLK, BLK)
        cols = lambda hh: slice(hh * BLK, (hh + 1) * BLK)
        left = lambda hh: [k_ref[pl.ds(left_off, n_left * BLK), cols(hh)]] if n_left else []
        rows = [jnp.concatenate(left(hh) + [k_ref[pl.ds(o, BLK), cols(hh)] for o in offs], axis=0)
                for hh in range(HPS)]
        vts = [jnp.concatenate([vt_ref[first + u, hh] for u in range(n_left)]
                               + [vt_ref[kb, hh] for kb in special_blocks], axis=1) for hh in range(HPS)]
        no_bias = [jnp.zeros((n_left * BLK, BLK), F32)] if n_left else []
        biases = [jnp.concatenate(no_bias + [bias_ref[t_meta, hh], bias_ref[t_prev, hh], bias_ref[T_DIAG, hh]], axis=0)
                  for hh in range(HPS)]
        return update_all(rows, vts, biases, carry)

    m0 = jnp.full((1, 2 * BLK), NEG, F32)
    l0 = jnp.zeros((1, 2 * BLK), F32)
    stats = _visit_key_blocks(i, group, special, tuple((m0, l0) for _ in range(HPS)), sizes=B_FAR_SIZES)

    for hh in range(HPS):
        a = acc_ref[hh] / stats[hh][1]
        o = a[:, :BLK] - lam * a[:, BLK:]
        ms = jnp.mean(o * o, axis=0, keepdims=True)
        y = o * lax.rsqrt(ms + EPS) * subw_ref[...] * (1.0 - lam_init)
        o_ref[:, hh * BLK:(hh + 1) * BLK] = y.T


def _attn_b(qkv_b, vt, bias_b, lam_p, subw, lam_init):
    qrow = lambda b, g, i: b * NKB + 1 + i
    wide = HPS * BLK
    qcol0 = 0
    kcol0 = 2 * B_HEADS * B_QK_DIM // wide
    return pl.pallas_call(
        functools.partial(_attn_b_kernel, lam_init=lam_init),
        grid=(BATCH, B_HEADS // HPS, NQB),
        in_specs=[
            pl.BlockSpec((4, B_QK_DIM), lambda b, g, i: (0, 0)),
            pl.BlockSpec((BLK, wide), lambda b, g, i: (qrow(b, g, i), qcol0 + g)),
            pl.BlockSpec((None, TP, wide), lambda b, g, i: (b, 0, kcol0 + g)),
            pl.BlockSpec((None, NKB, HPS, B_V_DIM, BLK), lambda b, g, i: (b, 0, g, 0, 0)),
            pl.BlockSpec((5, HPS, BLK, BLK), lambda b, g, i: (0, g, 0, 0)),
            pl.BlockSpec((B_V_DIM, BLK), lambda b, g, i: (0, 0)),
        ],
        out_specs=pl.BlockSpec((BLK, wide), lambda b, g, i: (b * NQB + i, g)),
        out_shape=jax.ShapeDtypeStruct((BATCH * SEQ, B_WIDTH), F32),
        scratch_shapes=[pltpu.VMEM((HPS, B_V_DIM, 2 * BLK), F32)],
        compiler_params=pltpu.CompilerParams(
            dimension_semantics=("arbitrary", "arbitrary", "arbitrary"), vmem_limit_bytes=VMEM_LIMIT),
        name="attn_b",
    )(lam_p, qkv_b, qkv_b.reshape(BATCH, TP, -1), vt, bias_b, subw)


def _out_kernel(oa_ref, za_ref, ob_ref, zb_ref, ga_ref, gb_ref, x_ref, woa_ref, wob_ref, wout_ref, pw_ref, o_ref):
    a = (oa_ref[...] * jax.nn.silu(za_ref[...])).astype(BF16)
    ya = jnp.dot(a, woa_ref[...], preferred_element_type=F32)
    b = (ob_ref[...] * jax.nn.silu(zb_ref[...])).astype(BF16)
    yb = jnp.dot(b, wob_ref[...], preferred_element_type=F32)
    mix = jax.nn.sigmoid(ga_ref[...]) * ya + jax.nn.sigmoid(gb_ref[...]) * yb
    out = jnp.dot(mix.astype(BF16), wout_ref[...], preferred_element_type=F32)
    ms = jnp.mean(out * out, axis=-1, keepdims=True)
    o_ref[...] = x_ref[...] + out * lax.rsqrt(ms + EPS) * pw_ref[...]


def _out_stage(o_a, o_b, z_a, z_b, gates, x2, woa, wob, wout, pw):
    tm = 2 * BLK
    const = lambda g: (0, 0)
    return pl.pallas_call(
        _out_kernel,
        grid=(BATCH * SEQ // tm,),
        in_specs=[
            pl.BlockSpec((tm, A_WIDTH), lambda g: (g, 0)),
            pl.BlockSpec((tm, A_WIDTH), lambda g: (g, 0)),
            pl.BlockSpec((tm, B_WIDTH), lambda g: (g, 0)),
            pl.BlockSpec((tm, B_WIDTH), lambda g: (g, 0)),
            pl.BlockSpec((tm, D_MODEL), lambda g: (g, 0)),
            pl.BlockSpec((tm, D_MODEL), lambda g: (g, 1)),
            pl.BlockSpec((tm, D_MODEL), lambda g: (g, 0)),
            pl.BlockSpec((A_WIDTH, D_MODEL), const, pipeline_mode=pl.Buffered(1)),
            pl.BlockSpec((B_WIDTH, D_MODEL), const, pipeline_mode=pl.Buffered(1)),
            pl.BlockSpec((D_MODEL, D_MODEL), const, pipeline_mode=pl.Buffered(1)),
            pl.BlockSpec((1, D_MODEL), const),
        ],
        out_specs=pl.BlockSpec((tm, D_MODEL), lambda g: (g, 0)),
        out_shape=jax.ShapeDtypeStruct((BATCH * SEQ, D_MODEL), F32),
        compiler_params=pltpu.CompilerParams(
            dimension_semantics=("arbitrary",), vmem_limit_bytes=VMEM_LIMIT),
        name="out_stage",
    )(o_a, z_a, o_b, z_b, gates, gates, x2, woa, wob, wout, pw)


def kernel(x, meta_tokens, rel_bias, pre_norm_w, w_in, kv_norm_w, w_uk, w_uv, idx_k_norm_w, idx_k_norm_b,
           diff_lambda, diff_subln_w, w_o_a, w_o_b, w_out, post_norm_w):
    assert x.shape == (BATCH, SEQ, D_MODEL) and w_in.shape[0] == 1
    layer = 0
    lam_init = 0.8 - 0.6 * math.exp(-0.3 * layer)

    meta_block = jnp.concatenate([meta_tokens.astype(F32), jnp.zeros((BLK - N_META, D_MODEL), F32)], axis=0)

    wt = w_in[0].T
    w_head, w_rest = _wprep_head(wt), _wprep_rest(wt)
    wd = A_WIDTH
    assert all(IN_SIZES[k] == wd for k in (0, 2, 3, 6, 7, 8, 9)) and IN_SIZES[10] == IN_SIZES[11] == 2 * wd
    scale_qb = jnp.concatenate([jnp.full((1, wd), B_QK_DIM ** -0.5 * LOG2E, F32), jnp.ones((1, 2 * wd), F32)], axis=1)

    u, u32, u_f = _prenorm(x.reshape(BATCH * SEQ, D_MODEL), meta_block, pre_norm_w[0][None].astype(F32))
    tm, tmf = ROWS // 8, BATCH * SEQ // 8
    q_a = _matmul(u, w_head, 0, wd, BF16, tm, wd, "proj_q_a")
    lat = _matmul(u, w_head, 2 * wd, KV_RANK, F32, tm, KV_RANK, "proj_latent")
    qkv_b = _matmul(u, w_rest, 0, 3 * wd, BF16, tm, wd, "proj_qkv_b", col_scale=scale_qb)
    w_idx = wt[IN_OFFS[3]:IN_OFFS[3] + IDX_ROWS]
    idxp = _matmul(u32, w_idx, 0, IDX_ROWS, F32, tm, IDX_ROWS, "proj_indexer")
    z_a = _matmul(u_f, w_head, wd, wd, F32, tmf, wd, "proj_z_a")
    z_b = _matmul(u_f, w_rest, 3 * wd, wd, F32, tmf, wd, "proj_z_b")
    gates = _matmul(u_f, w_rest, 4 * wd, 4 * wd, F32, tmf, wd, "proj_gates")

    ckv, ckvt, ik, iwt, vt = _kvprep(lat, idxp, qkv_b, 2 * wd, kv_norm_w[0][None].astype(F32),
                                     idx_k_norm_w[0][None].astype(F32), idx_k_norm_b[0][None].astype(F32))

    bias = _bias_tiles(rel_bias) * LOG2E
    bias_a = jnp.transpose(bias[:, :A_HEADS], (0, 2, 1, 3)).reshape(5, BLK, A_HEADS * BLK)
    bias_b = bias[:, A_HEADS:]

    wuk = jnp.transpose(w_uk[0], (1, 0, 2)).astype(BF16)
    wuvt = jnp.transpose(w_uv[0], (1, 2, 0)).astype(BF16)
    o_a = _attn_a(q_a, idxp, iwt, ckv, ckvt, ik, wuk, wuvt, bias_a)

    vt = vt.reshape(BATCH, NKB, B_HEADS, B_V_DIM, BLK)
    subw = jnp.broadcast_to(diff_subln_w[0].astype(F32)[:, None], (B_V_DIM, BLK))
    o_b = _attn_b(qkv_b, vt, bias_b, diff_lambda[0].astype(F32), subw, lam_init)

    out = _out_stage(o_a, o_b, z_a, z_b, gates, x.reshape(BATCH * SEQ, D_MODEL),
                     w_o_a[0].astype(BF16), w_o_b[0].astype(BF16), w_out[0].astype(BF16),
                     post_norm_w[0][None].astype(F32))
    return out.reshape(BATCH, SEQ, D_MODEL)
```

```python
import functools
import math

import numpy as np
import jax
import jax.numpy as jnp
from jax import lax
from jax.experimental import pallas as pl
from jax.experimental.pallas import tpu as pltpu

D_MODEL = 2048
BATCH = 2
SEQ = 4096
CHUNK = 64
N_META = 16
N_BUCKETS = 32
MAX_DISTANCE = 128
A_HEADS = 8
A_HEAD_DIM = 128
KV_RANK = 256
IDX_HEADS = 16
IDX_DIM = 64
TOPK = 256
B_HEADS = 8
B_QK_DIM = 64
B_V_DIM = 128
A_WIDTH = A_HEADS * A_HEAD_DIM
B_WIDTH = B_HEADS * B_V_DIM
IN_SIZES = (A_WIDTH, KV_RANK, A_WIDTH, IDX_HEADS * IDX_DIM, IDX_DIM, IDX_HEADS,
            2 * B_HEADS * B_QK_DIM, 2 * B_HEADS * B_QK_DIM, B_WIDTH, B_WIDTH,
            D_MODEL, D_MODEL)
EPS = 1e-6

BLK = 128
NQB = SEQ // BLK
NKB = NQB + 1
TP = NKB * BLK
ROWS = BATCH * TP
FAR = 4
HPS = 8
ATT_FAR_SIZES = (2 * FAR, FAR)
NEG = -1e30
INT_MIN = -2 ** 31
LOG2E = math.log2(math.e)
VMEM_LIMIT = 56 * 1024 * 1024

F32 = jnp.float32
BF16 = jnp.bfloat16
NT_DIMS = (((1,), (1,)), ((), ()))
NN_DIMS = (((1,), (0,)), ((), ()))


def _t5_bucket_np(rel):
    nb = N_BUCKETS // 2
    max_exact = nb // 2
    ret = np.where(rel > 0, nb, 0)
    n = np.abs(rel)
    nf = np.maximum(n, 1).astype(np.float32)
    large = max_exact + (np.log(nf / np.float32(max_exact))
                         / np.float32(math.log(MAX_DISTANCE / max_exact))
                         * np.float32(nb - max_exact)).astype(np.int32)
    large = np.minimum(large, nb - 1)
    return ret + np.where(n < max_exact, n, large)


T_DIAG, T_PREV, T_META0, T_METAFAR, T_NONE = range(5)


def _bias_tiles(rel_bias):
    a = np.arange(BLK)[:, None]
    b = np.arange(BLK)[None, :]
    nowhere = np.zeros((BLK, BLK), bool)
    pad_rows = (a >= N_META) | nowhere
    rels = np.stack([a - b, a - b - BLK, a - N_META - b])
    dis = np.stack([(a >= CHUNK) & (b < CHUNK), nowhere, pad_rows, pad_rows, ~nowhere])
    idx = _t5_bucket_np(rels)
    far_bucket = N_BUCKETS // 2 - 1
    assert _t5_bucket_np(np.array([-BLK - 1]))[0] == far_bucket
    rb = rel_bias.astype(F32)
    heads = A_HEADS + B_HEADS
    tiles = jnp.zeros((3, heads, BLK, BLK), F32)
    for k in np.unique(idx):
        tiles = jnp.where((idx == k)[:, None], rb[k][None, :, None, None], tiles)
    tiles = tiles - rb[far_bucket][None, :, None, None]
    tiles = jnp.concatenate([tiles, jnp.zeros((2, heads, BLK, BLK), F32)], axis=0)
    return jnp.where(dis[:, None], NEG, tiles)


def _visit_key_blocks(i, group_fn, special_fn, carry, sizes=(FAR,)):
    n_far = jnp.maximum(i - 1, 0)
    start = jnp.int32(1)
    for size in sizes:
        shift = size.bit_length() - 1
        assert size == 1 << shift
        n_groups = lax.shift_right_logical(1 + n_far - start, shift)
        carry = lax.fori_loop(0, n_groups, lambda c, cr, s=start, z=size: group_fn(s + z * c, z, cr), carry)
        start = start + size * n_groups
    branches = [functools.partial(special_fn, first=start, n_left=r) for r in range(sizes[-1])]
    return lax.switch(1 + n_far - start, branches, carry)


def _prenorm_kernel(x_ref, meta_ref, w_ref, o_ref, o32_ref, of_ref):
    def norm(x):
        ms = jnp.mean(x * x, axis=-1, keepdims=True)
        return x * lax.rsqrt(ms + EPS) * w_ref[...]

    is_meta = lax.rem(pl.program_id(0), NKB) == 0

    @pl.when(is_meta)
    def _():
        u = norm(meta_ref[...])
        o32_ref[...] = u
        o_ref[...] = u.astype(o_ref.dtype)

    @pl.when(jnp.logical_not(is_meta))
    def _():
        u = norm(x_ref[...])
        o32_ref[...] = u
        o_ref[...] = u.astype(o_ref.dtype)
        of_ref[...] = u.astype(of_ref.dtype)


def _prenorm(x2, meta_block, w):
    frame_block = lambda r: (jnp.maximum(r - r // NKB - 1, 0), 0)
    return pl.pallas_call(
        _prenorm_kernel,
        grid=(ROWS // BLK,),
        in_specs=[pl.BlockSpec((BLK, D_MODEL), frame_block),
                  pl.BlockSpec((BLK, D_MODEL), lambda r: (0, 0)),
                  pl.BlockSpec((1, D_MODEL), lambda r: (0, 0))],
        out_specs=[pl.BlockSpec((BLK, D_MODEL), lambda r: (r, 0)),
                   pl.BlockSpec((BLK, D_MODEL), lambda r: (r, 0)),
                   pl.BlockSpec((BLK, D_MODEL), frame_block)],
        out_shape=[jax.ShapeDtypeStruct((ROWS, D_MODEL), BF16),
                   jax.ShapeDtypeStruct((ROWS, D_MODEL), F32),
                   jax.ShapeDtypeStruct((BATCH * SEQ, D_MODEL), BF16)],
        compiler_params=pltpu.CompilerParams(dimension_semantics=("arbitrary",)),
        name="prenorm",
    )(x2, meta_block, w)


IN_OFFS = tuple(int(v) for v in np.concatenate([[0], np.cumsum(IN_SIZES)]))
HEAD_BLK = 2 * BLK
HEAD_ORDER = (0, 2, 1)
IDX_ROWS = IN_SIZES[3] + HEAD_BLK
W_REST = IN_OFFS[-1] - IN_OFFS[6]
REST_BLK = 4 * BLK


def _head_blocks():
    blocks = []
    for k in HEAD_ORDER:
        assert IN_OFFS[k] % HEAD_BLK == 0 and IN_SIZES[k] % HEAD_BLK == 0
        blocks += list(range(IN_OFFS[k] // HEAD_BLK, IN_OFFS[k + 1] // HEAD_BLK))
    return blocks


def _cast_kernel(w_ref, o_ref):
    o_ref[...] = w_ref[...].astype(o_ref.dtype)


def _wprep_head(wt):
    src = _head_blocks()

    def src_block(t):
        b = jnp.int32(src[-1])
        for pos in range(len(src) - 2, -1, -1):
            b = jnp.where(t == pos, src[pos], b)
        return b, 0
    return pl.pallas_call(
        _cast_kernel,
        grid=(len(src),),
        in_specs=[pl.BlockSpec((HEAD_BLK, D_MODEL), src_block)],
        out_specs=pl.BlockSpec((HEAD_BLK, D_MODEL), lambda t: (t, 0)),
        out_shape=jax.ShapeDtypeStruct((len(src) * HEAD_BLK, D_MODEL), BF16),
        compiler_params=pltpu.CompilerParams(dimension_semantics=("arbitrary",)),
        name="wprep_head",
    )(wt)


def _shift_cast_kernel(w_ref, o_ref, carry_ref, *, shift, n_out):
    s = pl.program_id(0)

    @pl.when(s > 0)
    def _():
        o_ref[...] = jnp.concatenate([carry_ref[...], w_ref[:shift]], axis=0).astype(o_ref.dtype)

    @pl.when(s < n_out)
    def _():
        carry_ref[...] = w_ref[shift:]


def _wprep_rest(wt):
    base, shift = divmod(IN_OFFS[6], REST_BLK)
    n_out = W_REST // REST_BLK
    assert shift % 8 == 0 and W_REST % REST_BLK == 0 and (base + n_out) * REST_BLK + shift == wt.shape[0]
    return pl.pallas_call(
        functools.partial(_shift_cast_kernel, shift=shift, n_out=n_out),
        grid=(n_out + 1,),
        in_specs=[pl.BlockSpec((REST_BLK, D_MODEL), lambda s: (base + s, 0))],
        out_specs=pl.BlockSpec((REST_BLK, D_MODEL), lambda s: (jnp.maximum(s - 1, 0), 0)),
        out_shape=jax.ShapeDtypeStruct((W_REST, D_MODEL), BF16),
        scratch_shapes=[pltpu.VMEM((REST_BLK - shift, D_MODEL), F32)],
        compiler_params=pltpu.CompilerParams(dimension_semantics=("arbitrary",), vmem_limit_bytes=VMEM_LIMIT),
        name="wprep_rest",
    )(wt)


def _mm_kernel(a_ref, w_ref, o_ref):
    acc = lax.dot_general(a_ref[...], w_ref[...], NT_DIMS, preferred_element_type=F32)
    o_ref[...] = acc.astype(o_ref.dtype)


def _mm_scaled_kernel(a_ref, w_ref, cs_ref, o_ref):
    acc = lax.dot_general(a_ref[...], w_ref[...], NT_DIMS, preferred_element_type=F32)
    o_ref[...] = (acc * cs_ref[...]).astype(o_ref.dtype)


def _matmul(a, wt, col0, n, out_dtype, tm, tn, name, col_scale=None):
    m, k = a.shape
    assert col0 % tn == 0 and n % tn == 0 and m % tm == 0 and wt.shape[1] == k
    c0 = col0 // tn
    resident = {"pipeline_mode": pl.Buffered(1)} if n == tn else {}
    in_specs = [pl.BlockSpec((tm, k), lambda i, j: (i, 0)),
                pl.BlockSpec((tn, k), lambda i, j: (c0 + j, 0), **resident)]
    args = (a, wt)
    if col_scale is not None:
        in_specs.append(pl.BlockSpec((1, tn), lambda i, j: (0, j)))
        args += (col_scale,)
    return pl.pallas_call(
        _mm_kernel if col_scale is None else _mm_scaled_kernel,
        grid=(m // tm, n // tn),
        in_specs=in_specs,
        out_specs=pl.BlockSpec((tm, tn), lambda i, j: (i, j)),
        out_shape=jax.ShapeDtypeStruct((m, n), out_dtype),
        compiler_params=pltpu.CompilerParams(
            dimension_semantics=("arbitrary", "arbitrary"), vmem_limit_bytes=VMEM_LIMIT),
        name=name,
    )(*args)


def _kvprep_kernel(c_ref, t_ref, v_ref, kvw_ref, ikw_ref, ikb_ref, ckv_ref, ckvt_ref, ik_ref, iwt_ref, vt_ref):
    for blk in range(2):
        for h in range(B_HEADS):
            vh = v_ref[blk * BLK:(blk + 1) * BLK, h * B_V_DIM:(h + 1) * B_V_DIM]
            vt_ref[blk, h] = vh.astype(F32).T.astype(BF16)
    ckv = c_ref[...]
    ms = jnp.mean(ckv * ckv, axis=-1, keepdims=True)
    ckvn = ckv * lax.rsqrt(ms + EPS) * kvw_ref[...]
    ckv_ref[...] = ckvn.astype(BF16)
    ckvt_ref[0] = ckvn[:BLK].T.astype(BF16)
    ckvt_ref[1] = ckvn[BLK:].T.astype(BF16)
    tail = t_ref[:, :BLK]
    ik = tail[:, :IDX_DIM]
    mu = jnp.mean(ik, axis=-1, keepdims=True)
    var = jnp.mean(jnp.square(ik - mu), axis=-1, keepdims=True)
    ik_ref[...] = (ik - mu) * lax.rsqrt(var + EPS) * ikw_ref[...] + ikb_ref[...]
    iwt_ref[...] = (tail * (IDX_HEADS ** -0.5 * IDX_DIM ** -0.5)).T[IDX_DIM:IDX_DIM + IDX_HEADS, :]


def _kvprep(c, idxp, kv_b, vcol, kvw, ikw, ikb):
    tm = 2 * BLK
    assert vcol % B_WIDTH == 0 and c.shape[1] == KV_RANK and idxp.shape[1] == IDX_ROWS
    return pl.pallas_call(
        _kvprep_kernel,
        grid=(ROWS // tm,),
        in_specs=[pl.BlockSpec((tm, KV_RANK), lambda i: (i, 0)),
                  pl.BlockSpec((tm, HEAD_BLK), lambda i: (i, IN_SIZES[3] // HEAD_BLK)),
                  pl.BlockSpec((tm, B_WIDTH), lambda i: (i, vcol // B_WIDTH)),
                  pl.BlockSpec((1, KV_RANK), lambda i: (0, 0)),
                  pl.BlockSpec((1, IDX_DIM), lambda i: (0, 0)),
                  pl.BlockSpec((1, IDX_DIM), lambda i: (0, 0))],
        out_specs=[pl.BlockSpec((tm, KV_RANK), lambda i: (i, 0)),
                   pl.BlockSpec((2, KV_RANK, BLK), lambda i: (i, 0, 0)),
                   pl.BlockSpec((tm, IDX_DIM), lambda i: (i, 0)),
                   pl.BlockSpec((IDX_HEADS, tm), lambda i: (0, i)),
                   pl.BlockSpec((2, B_HEADS, B_V_DIM, BLK), lambda i: (i, 0, 0, 0))],
        out_shape=[jax.ShapeDtypeStruct((ROWS, KV_RANK), BF16),
                   jax.ShapeDtypeStruct((ROWS // BLK, KV_RANK, BLK), BF16),
                   jax.ShapeDtypeStruct((ROWS, IDX_DIM), F32),
                   jax.ShapeDtypeStruct((IDX_HEADS, ROWS), F32),
                   jax.ShapeDtypeStruct((ROWS // BLK, B_HEADS, B_V_DIM, BLK), BF16)],
        name="kvprep",
    )(c, idxp, kv_b, kvw, ikw, ikb)


def _split_rows_dot(lhs, rhs, dims):
    rows = lhs.shape[0]
    if rows < 2 * BLK:
        return lax.dot_general(lhs, rhs, dims, preferred_element_type=F32)
    half = rows // 2
    return jnp.concatenate([lax.dot_general(lhs[:half], rhs, dims, preferred_element_type=F32),
                            lax.dot_general(lhs[half:], rhs, dims, preferred_element_type=F32)], axis=0)


def _online_softmax_step(s, m, l):
    m_new = jnp.maximum(m, jnp.max(s, axis=0, keepdims=True))
    alpha = jnp.exp2(m - m_new)
    p = jnp.exp2(s - m_new)
    return m_new, alpha * l + jnp.sum(p, axis=0, keepdims=True), alpha, p


def _attn_a_kernel(qa_ref, iq_ref, iwt_ref, ckv_ref, ckvt_ref, ik_ref, wuk_ref, wuvt_ref, bias_ref,
                   o_ref, keys_ref, qlat_ref, acc_ref, iqt_ref):
    i = pl.program_id(1)
    nkb = i + 2
    NG = A_HEADS // 2
    GW = 2 * BLK
    t_meta = jnp.where(i == 0, T_META0, T_METAFAR)
    t_prev = jnp.where(i == 0, T_NONE, T_PREV)
    special_blocks = (0, i, i + 1)

    for h in range(A_HEADS):
        qh = qa_ref[:, h * BLK:(h + 1) * BLK]
        ql = lax.dot_general(wuk_ref[h], qh, NT_DIMS, preferred_element_type=F32)
        qlat_ref[:, h * BLK:(h + 1) * BLK] = (ql * (A_HEAD_DIM ** -0.5 * LOG2E)).astype(BF16)

    for pr in range(IDX_HEADS // 2):
        t = iq_ref[:, pr * BLK:(pr + 1) * BLK].T
        iqt_ref[:, (2 * pr) * BLK:(2 * pr + 1) * BLK] = t[:IDX_DIM]
        iqt_ref[:, (2 * pr + 1) * BLK:(2 * pr + 2) * BLK] = t[IDX_DIM:]

    iwt = iwt_ref[...]
    row = lax.broadcasted_iota(jnp.int32, (BLK, BLK), 0)
    lane = lax.broadcasted_iota(jnp.int32, (BLK, BLK), 1)

    def idx_keys(ikrows):
        sc = jnp.zeros((ikrows.shape[0], BLK), F32)
        for pr in range(IDX_HEADS // 2):
            s2 = jnp.dot(ikrows, iqt_ref[:, pr * 2 * BLK:(pr + 1) * 2 * BLK], preferred_element_type=F32)
            sc = sc + jnp.maximum(s2[:, :BLK], 0.0) * iwt[2 * pr:2 * pr + 1, :]
            sc = sc + jnp.maximum(s2[:, BLK:], 0.0) * iwt[2 * pr + 1:2 * pr + 2, :]
        bits = lax.bitcast_convert_type(sc, jnp.int32)
        return bits ^ ((bits >> 31) & 0x7FFFFFFF)

    def put_keys(off, n, key):
        keys_ref[pl.ds(off, n * BLK), :] = key

    def idx_group(kb, n, carry):
        off = pl.multiple_of(kb * BLK, BLK)
        put_keys(off, n, idx_keys(ik_ref[pl.ds(off, n * BLK), :]))
        return carry

    def idx_special(carry, first, n_left):
        offs = [pl.multiple_of(kb * BLK, BLK) for kb in special_blocks]
        left_off = pl.multiple_of(first * BLK, BLK)
        left = [ik_ref[pl.ds(left_off, n_left * BLK), :]] if n_left else []
        key = idx_keys(jnp.concatenate(left + [ik_ref[pl.ds(o, BLK), :] for o in offs], axis=0))
        if n_left:
            put_keys(left_off, n_left, key[:n_left * BLK])
        key = key[n_left * BLK:]
        put_keys(offs[1], 1, key[BLK:2 * BLK])
        put_keys(offs[2], 1, jnp.where((row >= CHUNK) & (lane < CHUNK), INT_MIN, key[2 * BLK:]))
        put_keys(offs[0], 1, jnp.where(row >= N_META, INT_MIN, key[:BLK]))
        return carry

    _visit_key_blocks(i, idx_group, idx_special, 0)
    put_keys(pl.multiple_of(nkb * BLK, BLK), FAR - 1, jnp.full(((FAR - 1) * BLK, BLK), INT_MIN, jnp.int32))
    n_search = lax.shift_right_logical(nkb + FAR - 1, 2)
    crow = lax.broadcasted_iota(jnp.int32, (FAR * BLK, BLK), 0)
    chunk_off = lambda c: pl.multiple_of(c * FAR * BLK, FAR * BLK)

    def count(pred_fn):
        def body(c, acc8):
            k = keys_ref[pl.ds(chunk_off(c), FAR * BLK), :]
            hit = pred_fn(k, chunk_off(c)).astype(jnp.int32)
            return acc8 + jnp.sum(hit.reshape(FAR * BLK // 8, 8, BLK), axis=0)
        acc8 = lax.fori_loop(0, n_search, body, jnp.zeros((8, BLK), jnp.int32))
        return jnp.sum(acc8, axis=0, keepdims=True)

    zero = jnp.zeros((1, BLK), jnp.int32)
    c0 = count(lambda k, off: k >= zero)
    prefix = jnp.where(c0 >= TOPK, 0, INT_MIN).astype(jnp.int32)

    def bit_body(t, carry):
        prefix, n_ge = carry
        cand = prefix | jnp.left_shift(jnp.int32(1), 30 - t)
        c = count(lambda k, off: k >= cand)
        return jnp.where(c >= TOPK, cand, prefix), jnp.where(c >= TOPK, c, n_ge)

    thr, n_ge = lax.fori_loop(0, 31, bit_body, (prefix, c0))
    full = thr == INT_MIN
    tied = jnp.logical_and(jnp.logical_not(full), n_ge > TOPK)

    @pl.when(jnp.max(tied.astype(jnp.int32)) > 0)
    def _():
        need = TOPK - count(lambda k, off: k > thr)

        def jbit(t, j):
            cand = j | jnp.left_shift(jnp.int32(1), 12 - t)
            c = count(lambda k, off: (k == thr) & ((off + crow) < cand))
            return jnp.where(c < need, cand, j)
        jmax = lax.fori_loop(0, 13, jbit, jnp.zeros((1, BLK), jnp.int32))

        def strike(c, carry):
            k = keys_ref[pl.ds(chunk_off(c), FAR * BLK), :]
            surplus = tied & (k == thr) & ((chunk_off(c) + crow) > jmax)
            keys_ref[pl.ds(chunk_off(c), FAR * BLK), :] = jnp.where(surplus, INT_MIN, k)
            return carry
        lax.fori_loop(0, n_search, strike, 0)

    thr_sel = jnp.where(full, INT_MIN + 1, thr)

    acc_ref[...] = jnp.zeros_like(acc_ref)

    def att_update(rows, keys, vt, bias, carry):
        selb = jnp.where(keys >= thr_sel, 0.0, NEG)
        add = jnp.concatenate([selb, selb], axis=1)
        gs = [slice(g * GW, (g + 1) * GW) for g in range(NG)]
        ss = [_split_rows_dot(rows, qlat_ref[:, gs[g]], NN_DIMS) for g in range(NG)]
        ss = [ss[g] + (add if bias is None else add + bias[:, gs[g]]) for g in range(NG)]
        steps = [_online_softmax_step(ss[g], *carry[g]) for g in range(NG)]
        pvs = [jnp.dot(vt, steps[g][3].astype(BF16), preferred_element_type=F32) for g in range(NG)]
        for g in range(NG):
            acc_ref[:, gs[g]] = acc_ref[:, gs[g]] * steps[g][2] + pvs[g]
        return tuple((steps[g][0], steps[g][1]) for g in range(NG))

    def att_group(kb, n, carry):
        off = pl.multiple_of(kb * BLK, BLK)
        vt = jnp.concatenate([ckvt_ref[kb + u] for u in range(n)], axis=1) if n > 1 else ckvt_ref[kb]
        return att_update(ckv_ref[pl.ds(off, n * BLK), :], keys_ref[pl.ds(off, n * BLK), :], vt, None, carry)

    def att_special(carry, first, n_left):
        offs = [pl.multiple_of(kb * BLK, BLK) for kb in special_blocks]
        left_off = pl.multiple_of(first * BLK, BLK)
        left = lambda ref: [ref[pl.ds(left_off, n_left * BLK), :]] if n_left else []
        rows = jnp.concatenate(left(ckv_ref) + [ckv_ref[pl.ds(o, BLK), :] for o in offs], axis=0)
        keys = jnp.concatenate(left(keys_ref) + [keys_ref[pl.ds(o, BLK), :] for o in offs], axis=0)
        vt = jnp.concatenate([ckvt_ref[first + u] for u in range(n_left)] + [ckvt_ref[kb] for kb in special_blocks],
                             axis=1)
        no_bias = [jnp.zeros((n_left * BLK, A_HEADS * BLK), F32)] if n_left else []
        bias = jnp.concatenate(no_bias + [bias_ref[t_meta], bias_ref[t_prev], bias_ref[T_DIAG]], axis=0)
        return att_update(rows, keys, vt, bias, carry)

    m0 = jnp.full((1, GW), NEG, F32)
    l0 = jnp.zeros((1, GW), F32)
    stats = _visit_key_blocks(i, att_group, att_special, tuple((m0, l0) for _ in range(NG)), sizes=ATT_FAR_SIZES)
    l = jnp.concatenate([stats[g][1] for g in range(NG)], axis=1)

    olat = (acc_ref[...] / l).astype(BF16)
    ots = [jnp.dot(wuvt_ref[h], olat[:, h * BLK:(h + 1) * BLK], preferred_element_type=F32) for h in range(A_HEADS)]
    for h in range(A_HEADS):
        o_ref[:, h * BLK:(h + 1) * BLK] = ots[h].T


def _attn_a(q_a, idxp, iwt, ckv, ckvt, ik, wuk, wuvt, bias_a):
    qrow = lambda b, i: b * NKB + 1 + i
    return pl.pallas_call(
        _attn_a_kernel,
        grid=(BATCH, NQB),
        in_specs=[
            pl.BlockSpec((BLK, A_WIDTH), lambda b, i: (qrow(b, i), 0)),
            pl.BlockSpec((BLK, IDX_HEADS * IDX_DIM), lambda b, i: (qrow(b, i), 0)),
            pl.BlockSpec((IDX_HEADS, BLK), lambda b, i: (0, qrow(b, i))),
            pl.BlockSpec((None, TP, KV_RANK), lambda b, i: (b, 0, 0)),
            pl.BlockSpec((None, NKB, KV_RANK, BLK), lambda b, i: (b, 0, 0, 0)),
            pl.BlockSpec((None, TP, IDX_DIM), lambda b, i: (b, 0, 0)),
            pl.BlockSpec((A_HEADS, KV_RANK, A_HEAD_DIM), lambda b, i: (0, 0, 0)),
            pl.BlockSpec((A_HEADS, A_HEAD_DIM, KV_RANK), lambda b, i: (0, 0, 0)),
            pl.BlockSpec((5, BLK, A_HEADS * BLK), lambda b, i: (0, 0, 0)),
        ],
        out_specs=pl.BlockSpec((BLK, A_WIDTH), lambda b, i: (b * NQB + i, 0)),
        out_shape=jax.ShapeDtypeStruct((BATCH * SEQ, A_WIDTH), F32),
        scratch_shapes=[pltpu.VMEM(((NKB + FAR - 1) * BLK, BLK), jnp.int32),
                        pltpu.VMEM((KV_RANK, A_HEADS * BLK), BF16),
                        pltpu.VMEM((KV_RANK, A_HEADS * BLK), F32),
                        pltpu.VMEM((IDX_DIM, IDX_HEADS * BLK), F32)],
        compiler_params=pltpu.CompilerParams(
            dimension_semantics=("arbitrary", "arbitrary"), vmem_limit_bytes=VMEM_LIMIT),
        name="attn_a",
    )(q_a, idxp, iwt, ckv.reshape(BATCH, TP, KV_RANK), ckvt.reshape(BATCH, NKB, KV_RANK, BLK),
      ik.reshape(BATCH, TP, IDX_DIM), wuk, wuvt, bias_a)


def _attn_b_kernel(lam_ref, q_ref, k_ref, vt_ref, bias_ref, subw_ref, o_ref, acc_ref, *, lam_init):
    i = pl.program_id(2)
    t_meta = jnp.where(i == 0, T_META0, T_METAFAR)
    t_prev = jnp.where(i == 0, T_NONE, T_PREV)
    special_blocks = (0, i, i + 1)
    lp = lam_ref[...]
    lam = (jnp.exp(jnp.sum(lp[0:1] * lp[1:2], axis=-1, keepdims=True))
           - jnp.exp(jnp.sum(lp[2:3] * lp[3:4], axis=-1, keepdims=True)) + lam_init)

    lane = lax.broadcasted_iota(jnp.int32, (BLK, BLK), 1)
    qbd = []
    for hh in range(HPS):
        q = q_ref[:, hh * BLK:(hh + 1) * BLK]
        zq = jnp.zeros_like(q)
        qbd.append(jnp.concatenate([jnp.where(lane < B_QK_DIM, q, zq), jnp.where(lane >= B_QK_DIM, q, zq)], axis=0))

    acc_ref[...] = jnp.zeros_like(acc_ref)

    def update_all(rows, vts, biases, carry):
        ss = [_split_rows_dot(rows[hh], qbd[hh], NT_DIMS) for hh in range(HPS)]
        if biases is not None:
            ss = [ss[hh] + jnp.concatenate([biases[hh], biases[hh]], axis=1) for hh in range(HPS)]
        steps = [_online_softmax_step(ss[hh], *carry[hh]) for hh in range(HPS)]
        pvs = [jnp.dot(vts[hh], steps[hh][3].astype(BF16), preferred_element_type=F32) for hh in range(HPS)]
        for hh in range(HPS):
            acc_ref[hh] = acc_ref[hh] * steps[hh][2] + pvs[hh]
        return tuple((steps[hh][0], steps[hh][1]) for hh in range(HPS))

    def group(kb, n, carry):
        off = pl.multiple_of(kb * BLK, BLK)
        rows = [k_ref[pl.ds(off, n * BLK), hh * BLK:(hh + 1) * BLK] for hh in range(HPS)]
        vts = [jnp.concatenate([vt_ref[kb + u, hh] for u in range(n)], axis=1) if n > 1 else vt_ref[kb, hh]
               for hh in range(HPS)]
        return update_all(rows, vts, None, carry)

    def special(carry, first, n_left):
        offs = [pl.multiple_of(kb * BLK, BLK) for kb in special_blocks]
        left_off = pl.multiple_of(first * BLK, BLK)
        cols = lambda hh: slice(hh * BLK, (hh + 1) * BLK)
        left = lambda hh: [k_ref[pl.ds(left_off, n_left * BLK), cols(hh)]] if n_left else []
        rows = [jnp.concatenate(left(hh) + [k_ref[pl.ds(o, BLK), cols(hh)] for o in offs], axis=0)
                for hh in range(HPS)]
        vts = [jnp.concatenate([vt_ref[first + u, hh] for u in range(n_left)]
                               + [vt_ref[kb, hh] for kb in special_blocks], axis=1) for hh in range(HPS)]
        no_bias = [jnp.zeros((n_left * BLK, BLK), F32)] if n_left else []
        biases = [jnp.concatenate(no_bias + [bias_ref[t_meta, hh], bias_ref[t_prev, hh], bias_ref[T_DIAG, hh]], axis=0)
                  for hh in range(HPS)]
        return update_all(rows, vts, biases, carry)

    m0 = jnp.full((1, 2 * BLK), NEG, F32)
    l0 = jnp.zeros((1, 2 * BLK), F32)
    stats = _visit_key_blocks(i, group, special, tuple((m0, l0) for _ in range(HPS)), sizes=ATT_FAR_SIZES)

    for hh in range(HPS):
        a = acc_ref[hh] / stats[hh][1]
        o = a[:, :BLK] - lam * a[:, BLK:]
        ms = jnp.mean(o * o, axis=0, keepdims=True)
        y = o * lax.rsqrt(ms + EPS) * subw_ref[...] * (1.0 - lam_init)
        o_ref[:, hh * BLK:(hh + 1) * BLK] = y.T


def _attn_b(qkv_b, vt, bias_b, lam_p, subw, lam_init):
    qrow = lambda b, g, i: b * NKB + 1 + i
    wide = HPS * BLK
    qcol0 = 0
    kcol0 = 2 * B_HEADS * B_QK_DIM // wide
    return pl.pallas_call(
        functools.partial(_attn_b_kernel, lam_init=lam_init),
        grid=(BATCH, B_HEADS // HPS, NQB),
        in_specs=[
            pl.BlockSpec((4, B_QK_DIM), lambda b, g, i: (0, 0)),
            pl.BlockSpec((BLK, wide), lambda b, g, i: (qrow(b, g, i), qcol0 + g)),
            pl.BlockSpec((None, TP, wide), lambda b, g, i: (b, 0, kcol0 + g)),
            pl.BlockSpec((None, NKB, HPS, B_V_DIM, BLK), lambda b, g, i: (b, 0, g, 0, 0)),
            pl.BlockSpec((5, HPS, BLK, BLK), lambda b, g, i: (0, g, 0, 0)),
            pl.BlockSpec((B_V_DIM, BLK), lambda b, g, i: (0, 0)),
        ],
        out_specs=pl.BlockSpec((BLK, wide), lambda b, g, i: (b * NQB + i, g)),
        out_shape=jax.ShapeDtypeStruct((BATCH * SEQ, B_WIDTH), F32),
        scratch_shapes=[pltpu.VMEM((HPS, B_V_DIM, 2 * BLK), F32)],
        compiler_params=pltpu.CompilerParams(
            dimension_semantics=("arbitrary", "arbitrary", "arbitrary"), vmem_limit_bytes=VMEM_LIMIT),
        name="attn_b",
    )(lam_p, qkv_b, qkv_b.reshape(BATCH, TP, -1), vt, bias_b, subw)


def _out_kernel(oa_ref, za_ref, ob_ref, zb_ref, ga_ref, gb_ref, x_ref, woa_ref, wob_ref, wout_ref, pw_ref, o_ref):
    a = (oa_ref[...] * jax.nn.silu(za_ref[...])).astype(BF16)
    ya = jnp.dot(a, woa_ref[...], preferred_element_type=F32)
    b = (ob_ref[...] * jax.nn.silu(zb_ref[...])).astype(BF16)
    yb = jnp.dot(b, wob_ref[...], preferred_element_type=F32)
    mix = jax.nn.sigmoid(ga_ref[...]) * ya + jax.nn.sigmoid(gb_ref[...]) * yb
    out = jnp.dot(mix.astype(BF16), wout_ref[...], preferred_element_type=F32)
    ms = jnp.mean(out * out, axis=-1, keepdims=True)
    o_ref[...] = x_ref[...] + out * lax.rsqrt(ms + EPS) * pw_ref[...]


def _out_stage(o_a, o_b, z_a, z_b, gates, x2, woa, wob, wout, pw):
    tm = 2 * BLK
    const = lambda g: (0, 0)
    return pl.pallas_call(
        _out_kernel,
        grid=(BATCH * SEQ // tm,),
        in_specs=[
            pl.BlockSpec((tm, A_WIDTH), lambda g: (g, 0)),
            pl.BlockSpec((tm, A_WIDTH), lambda g: (g, 0)),
            pl.BlockSpec((tm, B_WIDTH), lambda g: (g, 0)),
            pl.BlockSpec((tm, B_WIDTH), lambda g: (g, 0)),
            pl.BlockSpec((tm, D_MODEL), lambda g: (g, 0)),
            pl.BlockSpec((tm, D_MODEL), lambda g: (g, 1)),
            pl.BlockSpec((tm, D_MODEL), lambda g: (g, 0)),
            pl.BlockSpec((A_WIDTH, D_MODEL), const, pipeline_mode=pl.Buffered(1)),
            pl.BlockSpec((B_WIDTH, D_MODEL), const, pipeline_mode=pl.Buffered(1)),
            pl.BlockSpec((D_MODEL, D_MODEL), const, pipeline_mode=pl.Buffered(1)),
            pl.BlockSpec((1, D_MODEL), const),
        ],
        out_specs=pl.BlockSpec((tm, D_MODEL), lambda g: (g, 0)),
        out_shape=jax.ShapeDtypeStruct((BATCH * SEQ, D_MODEL), F32),
        compiler_params=pltpu.CompilerParams(
            dimension_semantics=("arbitrary",), vmem_limit_bytes=VMEM_LIMIT),
        name="out_stage",
    )(o_a, z_a, o_b, z_b, gates, gates, x2, woa, wob, wout, pw)


def kernel(x, meta_tokens, rel_bias, pre_norm_w, w_in, kv_norm_w, w_uk, w_uv, idx_k_norm_w, idx_k_norm_b,
           diff_lambda, diff_subln_w, w_o_a, w_o_b, w_out, post_norm_w):
    assert x.shape == (BATCH, SEQ, D_MODEL) and w_in.shape[0] == 1
    layer = 0
    lam_init = 0.8 - 0.6 * math.exp(-0.3 * layer)

    meta_block = jnp.concatenate([meta_tokens.astype(F32), jnp.zeros((BLK - N_META, D_MODEL), F32)], axis=0)

    wt = w_in[0].T
    w_head, w_rest = _wprep_head(wt), _wprep_rest(wt)
    wd = A_WIDTH
    assert all(IN_SIZES[k] == wd for k in (0, 2, 3, 6, 7, 8, 9)) and IN_SIZES[10] == IN_SIZES[11] == 2 * wd
    scale_qb = jnp.concatenate([jnp.full((1, wd), B_QK_DIM ** -0.5 * LOG2E, F32), jnp.ones((1, 2 * wd), F32)], axis=1)

    u, u32, u_f = _prenorm(x.reshape(BATCH * SEQ, D_MODEL), meta_block, pre_norm_w[0][None].astype(F32))
    tm, tmf = ROWS // 8, BATCH * SEQ // 8
    q_a = _matmul(u, w_head, 0, wd, BF16, tm, wd, "proj_q_a")
    lat = _matmul(u, w_head, 2 * wd, KV_RANK, F32, tm, KV_RANK, "proj_latent")
    qkv_b = _matmul(u, w_rest, 0, 3 * wd, BF16, tm, wd, "proj_qkv_b", col_scale=scale_qb)
    w_idx = wt[IN_OFFS[3]:IN_OFFS[3] + IDX_ROWS]
    idxp = _matmul(u32, w_idx, 0, IDX_ROWS, F32, tm, IDX_ROWS, "proj_indexer")
    z_a = _matmul(u_f, w_head, wd, wd, F32, tmf, wd, "proj_z_a")
    z_b = _matmul(u_f, w_rest, 3 * wd, wd, F32, tmf, wd, "proj_z_b")
    gates = _matmul(u_f, w_rest, 4 * wd, 4 * wd, F32, tmf, wd, "proj_gates")

    ckv, ckvt, ik, iwt, vt = _kvprep(lat, idxp, qkv_b, 2 * wd, kv_norm_w[0][None].astype(F32),
                                     idx_k_norm_w[0][None].astype(F32), idx_k_norm_b[0][None].astype(F32))

    bias = _bias_tiles(rel_bias) * LOG2E
    bias_a = jnp.transpose(bias[:, :A_HEADS], (0, 2, 1, 3)).reshape(5, BLK, A_HEADS * BLK)
    bias_b = bias[:, A_HEADS:]

    wuk = jnp.transpose(w_uk[0], (1, 0, 2)).astype(BF16)
    wuvt = jnp.transpose(w_uv[0], (1, 2, 0)).astype(BF16)
    o_a = _attn_a(q_a, idxp, iwt, ckv, ckvt, ik, wuk, wuvt, bias_a)

    vt = vt.reshape(BATCH, NKB, B_HEADS, B_V_DIM, BLK)
    subw = jnp.broadcast_to(diff_subln_w[0].astype(F32)[:, None], (B_V_DIM, BLK))
    o_b = _attn_b(qkv_b, vt, bias_b, diff_lambda[0].astype(F32), subw, lam_init)

    out = _out_stage(o_a, o_b, z_a, z_b, gates, x.reshape(BATCH * SEQ, D_MODEL),
                     w_o_a[0].astype(BF16), w_o_b[0].astype(BF16), w_out[0].astype(BF16),
                     post_norm_w[0][None].astype(F32))
    return out.reshape(BATCH, SEQ, D_MODEL)
```

```python
import functools
import math

import numpy as np
import jax
import jax.numpy as jnp
from jax import lax
from jax.experimental import pallas as pl
from jax.experimental.pallas import tpu as pltpu

D_MODEL = 2048
BATCH = 2
SEQ = 4096
CHUNK = 64
N_META = 16
N_BUCKETS = 32
MAX_DISTANCE = 128
A_HEADS = 8
A_HEAD_DIM = 128
KV_RANK = 256
IDX_HEADS = 16
IDX_DIM = 64
TOPK = 256
B_HEADS = 8
B_QK_DIM = 64
B_V_DIM = 128
A_WIDTH = A_HEADS * A_HEAD_DIM
B_WIDTH = B_HEADS * B_V_DIM
IN_SIZES = (A_WIDTH, KV_RANK, A_WIDTH, IDX_HEADS * IDX_DIM, IDX_DIM, IDX_HEADS,
            2 * B_HEADS * B_QK_DIM, 2 * B_HEADS * B_QK_DIM, B_WIDTH, B_WIDTH,
            D_MODEL, D_MODEL)
EPS = 1e-6

BLK = 128
NQB = SEQ // BLK
NKB = NQB + 1
TP = NKB * BLK
ROWS = BATCH * TP
FAR = 4
HPS = 8
ATT_FAR_SIZES = (2 * FAR, FAR)
NEG = -1e30
INT_MIN = -2 ** 31
LOG2E = math.log2(math.e)
VMEM_LIMIT = 56 * 1024 * 1024

F32 = jnp.float32
BF16 = jnp.bfloat16
NT_DIMS = (((1,), (1,)), ((), ()))
NN_DIMS = (((1,), (0,)), ((), ()))


def _t5_bucket_np(rel):
    nb = N_BUCKETS // 2
    max_exact = nb // 2
    ret = np.where(rel > 0, nb, 0)
    n = np.abs(rel)
    nf = np.maximum(n, 1).astype(np.float32)
    large = max_exact + (np.log(nf / np.float32(max_exact))
                         / np.float32(math.log(MAX_DISTANCE / max_exact))
                         * np.float32(nb - max_exact)).astype(np.int32)
    large = np.minimum(large, nb - 1)
    return ret + np.where(n < max_exact, n, large)


T_DIAG, T_PREV, T_META0, T_METAFAR, T_NONE = range(5)


def _bias_tiles(rel_bias):
    a = np.arange(BLK)[:, None]
    b = np.arange(BLK)[None, :]
    nowhere = np.zeros((BLK, BLK), bool)
    pad_rows = (a >= N_META) | nowhere
    rels = np.stack([a - b, a - b - BLK, a - N_META - b])
    dis = np.stack([(a >= CHUNK) & (b < CHUNK), nowhere, pad_rows, pad_rows, ~nowhere])
    idx = _t5_bucket_np(rels)
    far_bucket = N_BUCKETS // 2 - 1
    assert _t5_bucket_np(np.array([-BLK - 1]))[0] == far_bucket
    rb = rel_bias.astype(F32)
    heads = A_HEADS + B_HEADS
    tiles = jnp.zeros((3, heads, BLK, BLK), F32)
    for k in np.unique(idx):
        tiles = jnp.where((idx == k)[:, None], rb[k][None, :, None, None], tiles)
    tiles = tiles - rb[far_bucket][None, :, None, None]
    tiles = jnp.concatenate([tiles, jnp.zeros((2, heads, BLK, BLK), F32)], axis=0)
    return jnp.where(dis[:, None], NEG, tiles)


def _visit_key_blocks(i, group_fn, special_fn, carry, sizes=(FAR,)):
    n_far = jnp.maximum(i - 1, 0)
    start = jnp.int32(1)
    for size in sizes:
        shift = size.bit_length() - 1
        assert size == 1 << shift
        n_groups = lax.shift_right_logical(1 + n_far - start, shift)
        carry = lax.fori_loop(0, n_groups, lambda c, cr, s=start, z=size: group_fn(s + z * c, z, cr), carry)
        start = start + size * n_groups
    branches = [functools.partial(special_fn, first=start, n_left=r) for r in range(sizes[-1])]
    return lax.switch(1 + n_far - start, branches, carry)


def _prenorm_kernel(x_ref, meta_ref, w_ref, o_ref, o32_ref, of_ref):
    def norm(x):
        ms = jnp.mean(x * x, axis=-1, keepdims=True)
        return x * lax.rsqrt(ms + EPS) * w_ref[...]

    is_meta = lax.rem(pl.program_id(0), NKB) == 0

    @pl.when(is_meta)
    def _():
        u = norm(meta_ref[...])
        o32_ref[...] = u
        o_ref[...] = u.astype(o_ref.dtype)

    @pl.when(jnp.logical_not(is_meta))
    def _():
        u = norm(x_ref[...])
        o32_ref[...] = u
        o_ref[...] = u.astype(o_ref.dtype)
        of_ref[...] = u.astype(of_ref.dtype)


def _prenorm(x2, meta_block, w):
    frame_block = lambda r: (jnp.maximum(r - r // NKB - 1, 0), 0)
    return pl.pallas_call(
        _prenorm_kernel,
        grid=(ROWS // BLK,),
        in_specs=[pl.BlockSpec((BLK, D_MODEL), frame_block),
                  pl.BlockSpec((BLK, D_MODEL), lambda r: (0, 0)),
                  pl.BlockSpec((1, D_MODEL), lambda r: (0, 0))],
        out_specs=[pl.BlockSpec((BLK, D_MODEL), lambda r: (r, 0)),
                   pl.BlockSpec((BLK, D_MODEL), lambda r: (r, 0)),
                   pl.BlockSpec((BLK, D_MODEL), frame_block)],
        out_shape=[jax.ShapeDtypeStruct((ROWS, D_MODEL), BF16),
                   jax.ShapeDtypeStruct((ROWS, D_MODEL), F32),
                   jax.ShapeDtypeStruct((BATCH * SEQ, D_MODEL), BF16)],
        compiler_params=pltpu.CompilerParams(dimension_semantics=("arbitrary",)),
        name="prenorm",
    )(x2, meta_block, w)


IN_OFFS = tuple(int(v) for v in np.concatenate([[0], np.cumsum(IN_SIZES)]))
HEAD_BLK = 2 * BLK
HEAD_ORDER = (0, 2, 1)
IDX_ROWS = IN_SIZES[3] + HEAD_BLK
W_REST = IN_OFFS[-1] - IN_OFFS[6]
REST_BLK = 4 * BLK


def _head_blocks():
    blocks = []
    for k in HEAD_ORDER:
        assert IN_OFFS[k] % HEAD_BLK == 0 and IN_SIZES[k] % HEAD_BLK == 0
        blocks += list(range(IN_OFFS[k] // HEAD_BLK, IN_OFFS[k + 1] // HEAD_BLK))
    return blocks


def _cast_kernel(w_ref, o_ref):
    o_ref[...] = w_ref[...].astype(o_ref.dtype)


def _wprep_head(wt):
    src = _head_blocks()

    def src_block(t):
        b = jnp.int32(src[-1])
        for pos in range(len(src) - 2, -1, -1):
            b = jnp.where(t == pos, src[pos], b)
        return b, 0
    return pl.pallas_call(
        _cast_kernel,
        grid=(len(src),),
        in_specs=[pl.BlockSpec((HEAD_BLK, D_MODEL), src_block)],
        out_specs=pl.BlockSpec((HEAD_BLK, D_MODEL), lambda t: (t, 0)),
        out_shape=jax.ShapeDtypeStruct((len(src) * HEAD_BLK, D_MODEL), BF16),
        compiler_params=pltpu.CompilerParams(dimension_semantics=("arbitrary",)),
        name="wprep_head",
    )(wt)


def _shift_cast_kernel(w_ref, o_ref, carry_ref, *, shift, n_out):
    s = pl.program_id(0)

    @pl.when(s > 0)
    def _():
        o_ref[...] = jnp.concatenate([carry_ref[...], w_ref[:shift]], axis=0).astype(o_ref.dtype)

    @pl.when(s < n_out)
    def _():
        carry_ref[...] = w_ref[shift:]


def _wprep_rest(wt):
    base, shift = divmod(IN_OFFS[6], REST_BLK)
    n_out = W_REST // REST_BLK
    assert shift % 8 == 0 and W_REST % REST_BLK == 0 and (base + n_out) * REST_BLK + shift == wt.shape[0]
    return pl.pallas_call(
        functools.partial(_shift_cast_kernel, shift=shift, n_out=n_out),
        grid=(n_out + 1,),
        in_specs=[pl.BlockSpec((REST_BLK, D_MODEL), lambda s: (base + s, 0))],
        out_specs=pl.BlockSpec((REST_BLK, D_MODEL), lambda s: (jnp.maximum(s - 1, 0), 0)),
        out_shape=jax.ShapeDtypeStruct((W_REST, D_MODEL), BF16),
        scratch_shapes=[pltpu.VMEM((REST_BLK - shift, D_MODEL), F32)],
        compiler_params=pltpu.CompilerParams(dimension_semantics=("arbitrary",), vmem_limit_bytes=VMEM_LIMIT),
        name="wprep_rest",
    )(wt)


def _mm_kernel(a_ref, w_ref, o_ref):
    acc = lax.dot_general(a_ref[...], w_ref[...], NT_DIMS, preferred_element_type=F32)
    o_ref[...] = acc.astype(o_ref.dtype)


def _mm_scaled_kernel(a_ref, w_ref, cs_ref, o_ref):
    acc = lax.dot_general(a_ref[...], w_ref[...], NT_DIMS, preferred_element_type=F32)
    o_ref[...] = (acc * cs_ref[...]).astype(o_ref.dtype)


def _matmul(a, wt, col0, n, out_dtype, tm, tn, name, col_scale=None):
    m, k = a.shape
    assert col0 % tn == 0 and n % tn == 0 and m % tm == 0 and wt.shape[1] == k
    c0 = col0 // tn
    resident = {"pipeline_mode": pl.Buffered(1)} if n == tn else {}
    in_specs = [pl.BlockSpec((tm, k), lambda i, j: (i, 0)),
                pl.BlockSpec((tn, k), lambda i, j: (c0 + j, 0), **resident)]
    args = (a, wt)
    if col_scale is not None:
        in_specs.append(pl.BlockSpec((1, tn), lambda i, j: (0, j)))
        args += (col_scale,)
    return pl.pallas_call(
        _mm_kernel if col_scale is None else _mm_scaled_kernel,
        grid=(m // tm, n // tn),
        in_specs=in_specs,
        out_specs=pl.BlockSpec((tm, tn), lambda i, j: (i, j)),
        out_shape=jax.ShapeDtypeStruct((m, n), out_dtype),
        compiler_params=pltpu.CompilerParams(
            dimension_semantics=("arbitrary", "arbitrary"), vmem_limit_bytes=VMEM_LIMIT),
        name=name,
    )(*args)


def _kvprep_kernel(c_ref, t_ref, v_ref, kvw_ref, ikw_ref, ikb_ref, ckv_ref, ckvt_ref, ik_ref, iwt_ref, vt_ref):
    for blk in range(2):
        for h in range(B_HEADS):
            vh = v_ref[blk * BLK:(blk + 1) * BLK, h * B_V_DIM:(h + 1) * B_V_DIM]
            vt_ref[blk, h] = vh.astype(F32).T.astype(BF16)
    ckv = c_ref[...]
    ms = jnp.mean(ckv * ckv, axis=-1, keepdims=True)
    ckvn = ckv * lax.rsqrt(ms + EPS) * kvw_ref[...]
    ckv_ref[...] = ckvn.astype(BF16)
    ckvt_ref[0] = ckvn[:BLK].T.astype(BF16)
    ckvt_ref[1] = ckvn[BLK:].T.astype(BF16)
    tail = t_ref[:, :BLK]
    ik = tail[:, :IDX_DIM]
    mu = jnp.mean(ik, axis=-1, keepdims=True)
    var = jnp.mean(jnp.square(ik - mu), axis=-1, keepdims=True)
    ik_ref[...] = (ik - mu) * lax.rsqrt(var + EPS) * ikw_ref[...] + ikb_ref[...]
    iwt_ref[...] = (tail * (IDX_HEADS ** -0.5 * IDX_DIM ** -0.5)).T[IDX_DIM:IDX_DIM + IDX_HEADS, :]


def _kvprep(c, idxp, kv_b, vcol, kvw, ikw, ikb):
    tm = 2 * BLK
    assert vcol % B_WIDTH == 0 and c.shape[1] == KV_RANK and idxp.shape[1] == IDX_ROWS
    return pl.pallas_call(
        _kvprep_kernel,
        grid=(ROWS // tm,),
        in_specs=[pl.BlockSpec((tm, KV_RANK), lambda i: (i, 0)),
                  pl.BlockSpec((tm, HEAD_BLK), lambda i: (i, IN_SIZES[3] // HEAD_BLK)),
                  pl.BlockSpec((tm, B_WIDTH), lambda i: (i, vcol // B_WIDTH)),
                  pl.BlockSpec((1, KV_RANK), lambda i: (0, 0)),
                  pl.BlockSpec((1, IDX_DIM), lambda i: (0, 0)),
                  pl.BlockSpec((1, IDX_DIM), lambda i: (0, 0))],
        out_specs=[pl.BlockSpec((tm, KV_RANK), lambda i: (i, 0)),
                   pl.BlockSpec((2, KV_RANK, BLK), lambda i: (i, 0, 0)),
                   pl.BlockSpec((tm, IDX_DIM), lambda i: (i, 0)),
                   pl.BlockSpec((IDX_HEADS, tm), lambda i: (0, i)),
                   pl.BlockSpec((2, B_HEADS, B_V_DIM, BLK), lambda i: (i, 0, 0, 0))],
        out_shape=[jax.ShapeDtypeStruct((ROWS, KV_RANK), BF16),
                   jax.ShapeDtypeStruct((ROWS // BLK, KV_RANK, BLK), BF16),
                   jax.ShapeDtypeStruct((ROWS, IDX_DIM), F32),
                   jax.ShapeDtypeStruct((IDX_HEADS, ROWS), F32),
                   jax.ShapeDtypeStruct((ROWS // BLK, B_HEADS, B_V_DIM, BLK), BF16)],
        name="kvprep",
    )(c, idxp, kv_b, kvw, ikw, ikb)


def _split_rows_dot(lhs, rhs, dims):
    rows = lhs.shape[0]
    if rows < 2 * BLK:
        return lax.dot_general(lhs, rhs, dims, preferred_element_type=F32)
    half = rows // 2
    return jnp.concatenate([lax.dot_general(lhs[:half], rhs, dims, preferred_element_type=F32),
                            lax.dot_general(lhs[half:], rhs, dims, preferred_element_type=F32)], axis=0)


def _online_softmax_step(s, m, l):
    m_new = jnp.maximum(m, jnp.max(s, axis=0, keepdims=True))
    alpha = jnp.exp2(m - m_new)
    p = jnp.exp2(s - m_new)
    return m_new, alpha * l + jnp.sum(p, axis=0, keepdims=True), alpha, p


def _attn_a_kernel(qa_ref, iq_ref, iwt_ref, ckv_ref, ckvt_ref, ik_ref, wuk_ref, wuvt_ref, bias_ref,
                   o_ref, keys_ref, qlat_ref, acc_ref, iqt_ref):
    i = pl.program_id(1)
    nkb = i + 2
    NG = A_HEADS // 2
    GW = 2 * BLK
    t_meta = jnp.where(i == 0, T_META0, T_METAFAR)
    t_prev = jnp.where(i == 0, T_NONE, T_PREV)
    special_blocks = (0, i, i + 1)

    for h in range(A_HEADS):
        qh = qa_ref[:, h * BLK:(h + 1) * BLK]
        ql = lax.dot_general(wuk_ref[h], qh, NT_DIMS, preferred_element_type=F32)
        qlat_ref[:, h * BLK:(h + 1) * BLK] = (ql * (A_HEAD_DIM ** -0.5 * LOG2E)).astype(BF16)

    for pr in range(IDX_HEADS // 2):
        t = iq_ref[:, pr * BLK:(pr + 1) * BLK].T
        iqt_ref[:, (2 * pr) * BLK:(2 * pr + 1) * BLK] = t[:IDX_DIM]
        iqt_ref[:, (2 * pr + 1) * BLK:(2 * pr + 2) * BLK] = t[IDX_DIM:]

    iwt = iwt_ref[...]
    row = lax.broadcasted_iota(jnp.int32, (BLK, BLK), 0)
    lane = lax.broadcasted_iota(jnp.int32, (BLK, BLK), 1)

    def idx_keys(ikrows):
        sc = jnp.zeros((ikrows.shape[0], BLK), F32)
        for pr in range(IDX_HEADS // 2):
            s2 = jnp.dot(ikrows, iqt_ref[:, pr * 2 * BLK:(pr + 1) * 2 * BLK], preferred_element_type=F32)
            sc = sc + jnp.maximum(s2[:, :BLK], 0.0) * iwt[2 * pr:2 * pr + 1, :]
            sc = sc + jnp.maximum(s2[:, BLK:], 0.0) * iwt[2 * pr + 1:2 * pr + 2, :]
        bits = lax.bitcast_convert_type(sc, jnp.int32)
        return bits ^ ((bits >> 31) & 0x7FFFFFFF)

    def put_keys(off, n, key):
        keys_ref[pl.ds(off, n * BLK), :] = key

    def idx_group(kb, n, carry):
        off = pl.multiple_of(kb * BLK, BLK)
        put_keys(off, n, idx_keys(ik_ref[pl.ds(off, n * BLK), :]))
        return carry

    def idx_special(carry, first, n_left):
        offs = [pl.multiple_of(kb * BLK, BLK) for kb in special_blocks]
        left_off = pl.multiple_of(first * BLK, BLK)
        left = [ik_ref[pl.ds(left_off, n_left * BLK), :]] if n_left else []
        key = idx_keys(jnp.concatenate(left + [ik_ref[pl.ds(o, BLK), :] for o in offs], axis=0))
        if n_left:
            put_keys(left_off, n_left, key[:n_left * BLK])
        key = key[n_left * BLK:]
        put_keys(offs[1], 1, key[BLK:2 * BLK])
        put_keys(offs[2], 1, jnp.where((row >= CHUNK) & (lane < CHUNK), INT_MIN, key[2 * BLK:]))
        put_keys(offs[0], 1, jnp.where(row >= N_META, INT_MIN, key[:BLK]))
        return carry

    _visit_key_blocks(i, idx_group, idx_special, 0)
    put_keys(pl.multiple_of(nkb * BLK, BLK), FAR - 1, jnp.full(((FAR - 1) * BLK, BLK), INT_MIN, jnp.int32))
    n_search = lax.shift_right_logical(nkb + FAR - 1, 2)
    crow = lax.broadcasted_iota(jnp.int32, (FAR * BLK, BLK), 0)
    chunk_off = lambda c: pl.multiple_of(c * FAR * BLK, FAR * BLK)

    def count(pred_fn):
        def body(c, acc8):
            k = keys_ref[pl.ds(chunk_off(c), FAR * BLK), :]
            hit = pred_fn(k, chunk_off(c)).astype(jnp.int32)
            return acc8 + jnp.sum(hit.reshape(FAR * BLK // 8, 8, BLK), axis=0)
        acc8 = lax.fori_loop(0, n_search, body, jnp.zeros((8, BLK), jnp.int32))
        return jnp.sum(acc8, axis=0, keepdims=True)

    zero = jnp.zeros((1, BLK), jnp.int32)
    c0 = count(lambda k, off: k >= zero)
    prefix = jnp.where(c0 >= TOPK, 0, INT_MIN).astype(jnp.int32)

    def bit_body(t, carry):
        prefix, n_ge = carry
        cand = prefix | jnp.left_shift(jnp.int32(1), 30 - t)
        c = count(lambda k, off: k >= cand)
        return jnp.where(c >= TOPK, cand, prefix), jnp.where(c >= TOPK, c, n_ge)

    thr, n_ge = lax.fori_loop(0, 31, bit_body, (prefix, c0))
    full = thr == INT_MIN
    tied = jnp.logical_and(jnp.logical_not(full), n_ge > TOPK)

    @pl.when(jnp.max(tied.astype(jnp.int32)) > 0)
    def _():
        need = TOPK - count(lambda k, off: k > thr)

        def jbit(t, j):
            cand = j | jnp.left_shift(jnp.int32(1), 12 - t)
            c = count(lambda k, off: (k == thr) & ((off + crow) < cand))
            return jnp.where(c < need, cand, j)
        jmax = lax.fori_loop(0, 13, jbit, jnp.zeros((1, BLK), jnp.int32))

        def strike(c, carry):
            k = keys_ref[pl.ds(chunk_off(c), FAR * BLK), :]
            surplus = tied & (k == thr) & ((chunk_off(c) + crow) > jmax)
            keys_ref[pl.ds(chunk_off(c), FAR * BLK), :] = jnp.where(surplus, INT_MIN, k)
            return carry
        lax.fori_loop(0, n_search, strike, 0)

    thr_sel = jnp.where(full, INT_MIN + 1, thr)

    acc_ref[...] = jnp.zeros_like(acc_ref)

    def att_update(rows, keys, vt, bias, carry):
        selb = jnp.where(keys >= thr_sel, 0.0, NEG)
        add = jnp.concatenate([selb, selb], axis=1)
        gs = [slice(g * GW, (g + 1) * GW) for g in range(NG)]
        ss = [_split_rows_dot(rows, qlat_ref[:, gs[g]], NN_DIMS) for g in range(NG)]
        ss = [ss[g] + (add if bias is None else add + bias[:, gs[g]]) for g in range(NG)]
        steps = [_online_softmax_step(ss[g], *carry[g]) for g in range(NG)]
        pvs = [jnp.dot(vt, steps[g][3].astype(BF16), preferred_element_type=F32) for g in range(NG)]
        for g in range(NG):
            acc_ref[:, gs[g]] = acc_ref[:, gs[g]] * steps[g][2] + pvs[g]
        return tuple((steps[g][0], steps[g][1]) for g in range(NG))

    def att_group(kb, n, carry):
        off = pl.multiple_of(kb * BLK, BLK)
        vt = jnp.concatenate([ckvt_ref[kb + u] for u in range(n)], axis=1) if n > 1 else ckvt_ref[kb]
        return att_update(ckv_ref[pl.ds(off, n * BLK), :], keys_ref[pl.ds(off, n * BLK), :], vt, None, carry)

    def att_special(carry, first, n_left):
        offs = [pl.multiple_of(kb * BLK, BLK) for kb in special_blocks]
        left_off = pl.multiple_of(first * BLK, BLK)
        left = lambda ref: [ref[pl.ds(left_off, n_left * BLK), :]] if n_left else []
        rows = jnp.concatenate(left(ckv_ref) + [ckv_ref[pl.ds(o, BLK), :] for o in offs], axis=0)
        keys = jnp.concatenate(left(keys_ref) + [keys_ref[pl.ds(o, BLK), :] for o in offs], axis=0)
        vt = jnp.concatenate([ckvt_ref[first + u] for u in range(n_left)] + [ckvt_ref[kb] for kb in special_blocks],
                             axis=1)
        no_bias = [jnp.zeros((n_left * BLK, A_HEADS * BLK), F32)] if n_left else []
        bias = jnp.concatenate(no_bias + [bias_ref[t_meta], bias_ref[t_prev], bias_ref[T_DIAG]], axis=0)
        return att_update(rows, keys, vt, bias, carry)

    m0 = jnp.full((1, GW), NEG, F32)
    l0 = jnp.zeros((1, GW), F32)
    stats = _visit_key_blocks(i, att_group, att_special, tuple((m0, l0) for _ in range(NG)), sizes=ATT_FAR_SIZES)
    l = jnp.concatenate([stats[g][1] for g in range(NG)], axis=1)

    olat = acc_ref[...].astype(BF16)
    ots = [jnp.dot(wuvt_ref[h], olat[:, h * BLK:(h + 1) * BLK], preferred_element_type=F32) for h in range(A_HEADS)]
    for h in range(A_HEADS):
        o_ref[:, h * BLK:(h + 1) * BLK] = (ots[h] / l[:, h * BLK:(h + 1) * BLK]).T


def _attn_a(q_a, idxp, iwt, ckv, ckvt, ik, wuk, wuvt, bias_a):
    qrow = lambda b, i: b * NKB + 1 + i
    return pl.pallas_call(
        _attn_a_kernel,
        grid=(BATCH, NQB),
        in_specs=[
            pl.BlockSpec((BLK, A_WIDTH), lambda b, i: (qrow(b, i), 0)),
            pl.BlockSpec((BLK, IDX_HEADS * IDX_DIM), lambda b, i: (qrow(b, i), 0)),
            pl.BlockSpec((IDX_HEADS, BLK), lambda b, i: (0, qrow(b, i))),
            pl.BlockSpec((None, TP, KV_RANK), lambda b, i: (b, 0, 0)),
            pl.BlockSpec((None, NKB, KV_RANK, BLK), lambda b, i: (b, 0, 0, 0)),
            pl.BlockSpec((None, TP, IDX_DIM), lambda b, i: (b, 0, 0)),
            pl.BlockSpec((A_HEADS, KV_RANK, A_HEAD_DIM), lambda b, i: (0, 0, 0)),
            pl.BlockSpec((A_HEADS, A_HEAD_DIM, KV_RANK), lambda b, i: (0, 0, 0)),
            pl.BlockSpec((5, BLK, A_HEADS * BLK), lambda b, i: (0, 0, 0)),
        ],
        out_specs=pl.BlockSpec((BLK, A_WIDTH), lambda b, i: (b * NQB + i, 0)),
        out_shape=jax.ShapeDtypeStruct((BATCH * SEQ, A_WIDTH), F32),
        scratch_shapes=[pltpu.VMEM(((NKB + FAR - 1) * BLK, BLK), jnp.int32),
                        pltpu.VMEM((KV_RANK, A_HEADS * BLK), BF16),
                        pltpu.VMEM((KV_RANK, A_HEADS * BLK), F32),
                        pltpu.VMEM((IDX_DIM, IDX_HEADS * BLK), F32)],
        compiler_params=pltpu.CompilerParams(
            dimension_semantics=("arbitrary", "arbitrary"), vmem_limit_bytes=VMEM_LIMIT),
        name="attn_a",
    )(q_a, idxp, iwt, ckv.reshape(BATCH, TP, KV_RANK), ckvt.reshape(BATCH, NKB, KV_RANK, BLK),
      ik.reshape(BATCH, TP, IDX_DIM), wuk, wuvt, bias_a)


def _attn_b_kernel(lam_ref, q_ref, k_ref, vt_ref, bias_ref, subw_ref, o_ref, acc_ref, *, lam_init):
    i = pl.program_id(2)
    t_meta = jnp.where(i == 0, T_META0, T_METAFAR)
    t_prev = jnp.where(i == 0, T_NONE, T_PREV)
    special_blocks = (0, i, i + 1)
    lp = lam_ref[...]
    lam = (jnp.exp(jnp.sum(lp[0:1] * lp[1:2], axis=-1, keepdims=True))
           - jnp.exp(jnp.sum(lp[2:3] * lp[3:4], axis=-1, keepdims=True)) + lam_init)

    lane = lax.broadcasted_iota(jnp.int32, (BLK, BLK), 1)
    qbd = []
    for hh in range(HPS):
        q = q_ref[:, hh * BLK:(hh + 1) * BLK]
        zq = jnp.zeros_like(q)
        qbd.append(jnp.concatenate([jnp.where(lane < B_QK_DIM, q, zq), jnp.where(lane >= B_QK_DIM, q, zq)], axis=0))

    acc_ref[...] = jnp.zeros_like(acc_ref)

    def update_all(rows, vts, biases, carry):
        ss = [_split_rows_dot(rows[hh], qbd[hh], NT_DIMS) for hh in range(HPS)]
        if biases is not None:
            ss = [ss[hh] + jnp.concatenate([biases[hh], biases[hh]], axis=1) for hh in range(HPS)]
        steps = [_online_softmax_step(ss[hh], *carry[hh]) for hh in range(HPS)]
        pvs = [jnp.dot(vts[hh], steps[hh][3].astype(BF16), preferred_element_type=F32) for hh in range(HPS)]
        for hh in range(HPS):
            acc_ref[hh] = acc_ref[hh] * steps[hh][2] + pvs[hh]
        return tuple((steps[hh][0], steps[hh][1]) for hh in range(HPS))

    def group(kb, n, carry):
        off = pl.multiple_of(kb * BLK, BLK)
        rows = [k_ref[pl.ds(off, n * BLK), hh * BLK:(hh + 1) * BLK] for hh in range(HPS)]
        vts = [jnp.concatenate([vt_ref[kb + u, hh] for u in range(n)], axis=1) if n > 1 else vt_ref[kb, hh]
               for hh in range(HPS)]
        return update_all(rows, vts, None, carry)

    def special(carry, first, n_left):
        offs = [pl.multiple_of(kb * BLK, BLK) for kb in special_blocks]
        left_off = pl.multiple_of(first * BLK, BLK)
        cols = lambda hh: slice(hh * BLK, (hh + 1) * BLK)
        left = lambda hh: [k_ref[pl.ds(left_off, n_left * BLK), cols(hh)]] if n_left else []
        rows = [jnp.concatenate(left(hh) + [k_ref[pl.ds(o, BLK), cols(hh)] for o in offs], axis=0)
                for hh in range(HPS)]
        vts = [jnp.concatenate([vt_ref[first + u, hh] for u in range(n_left)]
                               + [vt_ref[kb, hh] for kb in special_blocks], axis=1) for hh in range(HPS)]
        no_bias = [jnp.zeros((n_left * BLK, BLK), F32)] if n_left else []
        biases = [jnp.concatenate(no_bias + [bias_ref[t_meta, hh], bias_ref[t_prev, hh], bias_ref[T_DIAG, hh]], axis=0)
                  for hh in range(HPS)]
        return update_all(rows, vts, biases, carry)

    m0 = jnp.full((1, 2 * BLK), NEG, F32)
    l0 = jnp.zeros((1, 2 * BLK), F32)
    stats = _visit_key_blocks(i, group, special, tuple((m0, l0) for _ in range(HPS)), sizes=ATT_FAR_SIZES)

    for hh in range(HPS):
        a = acc_ref[hh] / stats[hh][1]
        o = a[:, :BLK] - lam * a[:, BLK:]
        ms = jnp.mean(o * o, axis=0, keepdims=True)
        y = o * lax.rsqrt(ms + EPS) * subw_ref[...] * (1.0 - lam_init)
        o_ref[:, hh * BLK:(hh + 1) * BLK] = y.T


def _attn_b(qkv_b, vt, bias_b, lam_p, subw, lam_init):
    qrow = lambda b, g, i: b * NKB + 1 + i
    wide = HPS * BLK
    qcol0 = 0
    kcol0 = 2 * B_HEADS * B_QK_DIM // wide
    return pl.pallas_call(
        functools.partial(_attn_b_kernel, lam_init=lam_init),
        grid=(BATCH, B_HEADS // HPS, NQB),
        in_specs=[
            pl.BlockSpec((4, B_QK_DIM), lambda b, g, i: (0, 0)),
            pl.BlockSpec((BLK, wide), lambda b, g, i: (qrow(b, g, i), qcol0 + g)),
            pl.BlockSpec((None, TP, wide), lambda b, g, i: (b, 0, kcol0 + g)),
            pl.BlockSpec((None, NKB, HPS, B_V_DIM, BLK), lambda b, g, i: (b, 0, g, 0, 0)),
            pl.BlockSpec((5, HPS, BLK, BLK), lambda b, g, i: (0, g, 0, 0)),
            pl.BlockSpec((B_V_DIM, BLK), lambda b, g, i: (0, 0)),
        ],
        out_specs=pl.BlockSpec((BLK, wide), lambda b, g, i: (b * NQB + i, g)),
        out_shape=jax.ShapeDtypeStruct((BATCH * SEQ, B_WIDTH), F32),
        scratch_shapes=[pltpu.VMEM((HPS, B_V_DIM, 2 * BLK), F32)],
        compiler_params=pltpu.CompilerParams(
            dimension_semantics=("arbitrary", "arbitrary", "arbitrary"), vmem_limit_bytes=VMEM_LIMIT),
        name="attn_b",
    )(lam_p, qkv_b, qkv_b.reshape(BATCH, TP, -1), vt, bias_b, subw)


def _out_kernel(oa_ref, za_ref, ob_ref, zb_ref, ga_ref, gb_ref, x_ref, woa_ref, wob_ref, wout_ref, pw_ref, o_ref):
    a = (oa_ref[...] * jax.nn.silu(za_ref[...])).astype(BF16)
    ya = jnp.dot(a, woa_ref[...], preferred_element_type=F32)
    b = (ob_ref[...] * jax.nn.silu(zb_ref[...])).astype(BF16)
    yb = jnp.dot(b, wob_ref[...], preferred_element_type=F32)
    mix = jax.nn.sigmoid(ga_ref[...]) * ya + jax.nn.sigmoid(gb_ref[...]) * yb
    out = jnp.dot(mix.astype(BF16), wout_ref[...], preferred_element_type=F32)
    ms = jnp.mean(out * out, axis=-1, keepdims=True)
    o_ref[...] = x_ref[...] + out * lax.rsqrt(ms + EPS) * pw_ref[...]


def _out_stage(o_a, o_b, z_a, z_b, gates, x2, woa, wob, wout, pw):
    tm = 2 * BLK
    const = lambda g: (0, 0)
    return pl.pallas_call(
        _out_kernel,
        grid=(BATCH * SEQ // tm,),
        in_specs=[
            pl.BlockSpec((tm, A_WIDTH), lambda g: (g, 0)),
            pl.BlockSpec((tm, A_WIDTH), lambda g: (g, 0)),
            pl.BlockSpec((tm, B_WIDTH), lambda g: (g, 0)),
            pl.BlockSpec((tm, B_WIDTH), lambda g: (g, 0)),
            pl.BlockSpec((tm, D_MODEL), lambda g: (g, 0)),
            pl.BlockSpec((tm, D_MODEL), lambda g: (g, 1)),
            pl.BlockSpec((tm, D_MODEL), lambda g: (g, 0)),
            pl.BlockSpec((A_WIDTH, D_MODEL), const, pipeline_mode=pl.Buffered(1)),
            pl.BlockSpec((B_WIDTH, D_MODEL), const, pipeline_mode=pl.Buffered(1)),
            pl.BlockSpec((D_MODEL, D_MODEL), const, pipeline_mode=pl.Buffered(1)),
            pl.BlockSpec((1, D_MODEL), const),
        ],
        out_specs=pl.BlockSpec((tm, D_MODEL), lambda g: (g, 0)),
        out_shape=jax.ShapeDtypeStruct((BATCH * SEQ, D_MODEL), F32),
        compiler_params=pltpu.CompilerParams(
            dimension_semantics=("arbitrary",), vmem_limit_bytes=VMEM_LIMIT),
        name="out_stage",
    )(o_a, z_a, o_b, z_b, gates, gates, x2, woa, wob, wout, pw)


def kernel(x, meta_tokens, rel_bias, pre_norm_w, w_in, kv_norm_w, w_uk, w_uv, idx_k_norm_w, idx_k_norm_b,
           diff_lambda, diff_subln_w, w_o_a, w_o_b, w_out, post_norm_w):
    assert x.shape == (BATCH, SEQ, D_MODEL) and w_in.shape[0] == 1
    layer = 0
    lam_init = 0.8 - 0.6 * math.exp(-0.3 * layer)

    meta_block = jnp.concatenate([meta_tokens.astype(F32), jnp.zeros((BLK - N_META, D_MODEL), F32)], axis=0)

    wt = w_in[0].T
    w_head, w_rest = _wprep_head(wt), _wprep_rest(wt)
    wd = A_WIDTH
    assert all(IN_SIZES[k] == wd for k in (0, 2, 3, 6, 7, 8, 9)) and IN_SIZES[10] == IN_SIZES[11] == 2 * wd
    scale_qb = jnp.concatenate([jnp.full((1, wd), B_QK_DIM ** -0.5 * LOG2E, F32), jnp.ones((1, 2 * wd), F32)], axis=1)

    u, u32, u_f = _prenorm(x.reshape(BATCH * SEQ, D_MODEL), meta_block, pre_norm_w[0][None].astype(F32))
    tm, tmf = ROWS // 8, BATCH * SEQ // 4
    q_a = _matmul(u, w_head, 0, wd, BF16, 2 * tm, wd, "proj_q_a")
    lat = _matmul(u, w_head, 2 * wd, KV_RANK, F32, tm, KV_RANK, "proj_latent")
    qkv_b = _matmul(u, w_rest, 0, 3 * wd, BF16, 2 * tm, wd, "proj_qkv_b", col_scale=scale_qb)
    w_idx = wt[IN_OFFS[3]:IN_OFFS[3] + IDX_ROWS]
    idxp = _matmul(u32, w_idx, 0, IDX_ROWS, F32, tm, IDX_ROWS, "proj_indexer")
    z_a = _matmul(u_f, w_head, wd, wd, F32, tmf, wd, "proj_z_a")
    z_b = _matmul(u_f, w_rest, 3 * wd, wd, F32, tmf, wd, "proj_z_b")
    gates = _matmul(u_f, w_rest, 4 * wd, 4 * wd, F32, tmf, wd, "proj_gates")

    ckv, ckvt, ik, iwt, vt = _kvprep(lat, idxp, qkv_b, 2 * wd, kv_norm_w[0][None].astype(F32),
                                     idx_k_norm_w[0][None].astype(F32), idx_k_norm_b[0][None].astype(F32))

    bias = _bias_tiles(rel_bias) * LOG2E
    bias_a = jnp.transpose(bias[:, :A_HEADS], (0, 2, 1, 3)).reshape(5, BLK, A_HEADS * BLK)
    bias_b = bias[:, A_HEADS:]

    wuk = jnp.transpose(w_uk[0], (1, 0, 2)).astype(BF16)
    wuvt = jnp.transpose(w_uv[0], (1, 2, 0)).astype(BF16)
    o_a = _attn_a(q_a, idxp, iwt, ckv, ckvt, ik, wuk, wuvt, bias_a)

    vt = vt.reshape(BATCH, NKB, B_HEADS, B_V_DIM, BLK)
    subw = jnp.broadcast_to(diff_subln_w[0].astype(F32)[:, None], (B_V_DIM, BLK))
    o_b = _attn_b(qkv_b, vt, bias_b, diff_lambda[0].astype(F32), subw, lam_init)

    out = _out_stage(o_a, o_b, z_a, z_b, gates, x.reshape(BATCH * SEQ, D_MODEL),
                     w_o_a[0].astype(BF16), w_o_b[0].astype(BF16), w_out[0].astype(BF16),
                     post_norm_w[0][None].astype(F32))
    return out.reshape(BATCH, SEQ, D_MODEL)
```

```python
import functools
import math

import numpy as np
import jax
import jax.numpy as jnp
from jax import lax
from jax.experimental import pallas as pl
from jax.experimental.pallas import tpu as pltpu

D_MODEL = 2048
BATCH = 2
SEQ = 4096
CHUNK = 64
N_META = 16
N_BUCKETS = 32
MAX_DISTANCE = 128
A_HEADS = 8
A_HEAD_DIM = 128
KV_RANK = 256
IDX_HEADS = 16
IDX_DIM = 64
TOPK = 256
B_HEADS = 8
B_QK_DIM = 64
B_V_DIM = 128
A_WIDTH = A_HEADS * A_HEAD_DIM
B_WIDTH = B_HEADS * B_V_DIM
IN_SIZES = (A_WIDTH, KV_RANK, A_WIDTH, IDX_HEADS * IDX_DIM, IDX_DIM, IDX_HEADS,
            2 * B_HEADS * B_QK_DIM, 2 * B_HEADS * B_QK_DIM, B_WIDTH, B_WIDTH,
            D_MODEL, D_MODEL)
EPS = 1e-6

BLK = 128
NQB = SEQ // BLK
NKB = NQB + 1
TP = NKB * BLK
ROWS = BATCH * TP
FAR = 4
HPS = 8
ATT_FAR_SIZES = (2 * FAR, FAR)
NEG = -1e30
INT_MIN = -2 ** 31
LATE_BITS = 4
LOG2E = math.log2(math.e)
VMEM_LIMIT = 56 * 1024 * 1024

F32 = jnp.float32
BF16 = jnp.bfloat16
NT_DIMS = (((1,), (1,)), ((), ()))
NN_DIMS = (((1,), (0,)), ((), ()))


def _t5_bucket_np(rel):
    nb = N_BUCKETS // 2
    max_exact = nb // 2
    ret = np.where(rel > 0, nb, 0)
    n = np.abs(rel)
    nf = np.maximum(n, 1).astype(np.float32)
    large = max_exact + (np.log(nf / np.float32(max_exact))
                         / np.float32(math.log(MAX_DISTANCE / max_exact))
                         * np.float32(nb - max_exact)).astype(np.int32)
    large = np.minimum(large, nb - 1)
    return ret + np.where(n < max_exact, n, large)


T_DIAG, T_PREV, T_META0, T_METAFAR, T_NONE = range(5)


def _bias_tiles(rel_bias):
    a = np.arange(BLK)[:, None]
    b = np.arange(BLK)[None, :]
    nowhere = np.zeros((BLK, BLK), bool)
    pad_rows = (a >= N_META) | nowhere
    rels = np.stack([a - b, a - b - BLK, a - N_META - b])
    dis = np.stack([(a >= CHUNK) & (b < CHUNK), nowhere, pad_rows, pad_rows, ~nowhere])
    idx = _t5_bucket_np(rels)
    far_bucket = N_BUCKETS // 2 - 1
    assert _t5_bucket_np(np.array([-BLK - 1]))[0] == far_bucket
    rb = rel_bias.astype(F32)
    heads = A_HEADS + B_HEADS
    tiles = jnp.zeros((3, heads, BLK, BLK), F32)
    for k in np.unique(idx):
        tiles = jnp.where((idx == k)[:, None], rb[k][None, :, None, None], tiles)
    tiles = tiles - rb[far_bucket][None, :, None, None]
    tiles = jnp.concatenate([tiles, jnp.zeros((2, heads, BLK, BLK), F32)], axis=0)
    return jnp.where(dis[:, None], NEG, tiles)


def _visit_key_blocks(i, group_fn, special_fn, carry, sizes=(FAR,)):
    n_far = jnp.maximum(i - 1, 0)
    start = jnp.int32(1)
    for size in sizes:
        shift = size.bit_length() - 1
        assert size == 1 << shift
        n_groups = lax.shift_right_logical(1 + n_far - start, shift)
        carry = lax.fori_loop(0, n_groups, lambda c, cr, s=start, z=size: group_fn(s + z * c, z, cr), carry)
        start = start + size * n_groups
    branches = [functools.partial(special_fn, first=start, n_left=r) for r in range(sizes[-1])]
    return lax.switch(1 + n_far - start, branches, carry)


def _prenorm_kernel(x_ref, meta_ref, w_ref, o_ref, o32_ref, of_ref):
    def norm(x):
        ms = jnp.mean(x * x, axis=-1, keepdims=True)
        return x * lax.rsqrt(ms + EPS) * w_ref[...]

    is_meta = lax.rem(pl.program_id(0), NKB) == 0

    @pl.when(is_meta)
    def _():
        u = norm(meta_ref[...])
        o32_ref[...] = u
        o_ref[...] = u.astype(o_ref.dtype)

    @pl.when(jnp.logical_not(is_meta))
    def _():
        u = norm(x_ref[...])
        o32_ref[...] = u
        o_ref[...] = u.astype(o_ref.dtype)
        of_ref[...] = u.astype(of_ref.dtype)


def _prenorm(x2, meta_block, w):
    frame_block = lambda r: (jnp.maximum(r - r // NKB - 1, 0), 0)
    return pl.pallas_call(
        _prenorm_kernel,
        grid=(ROWS // BLK,),
        in_specs=[pl.BlockSpec((BLK, D_MODEL), frame_block),
                  pl.BlockSpec((BLK, D_MODEL), lambda r: (0, 0)),
                  pl.BlockSpec((1, D_MODEL), lambda r: (0, 0))],
        out_specs=[pl.BlockSpec((BLK, D_MODEL), lambda r: (r, 0)),
                   pl.BlockSpec((BLK, D_MODEL), lambda r: (r, 0)),
                   pl.BlockSpec((BLK, D_MODEL), frame_block)],
        out_shape=[jax.ShapeDtypeStruct((ROWS, D_MODEL), BF16),
                   jax.ShapeDtypeStruct((ROWS, D_MODEL), F32),
                   jax.ShapeDtypeStruct((BATCH * SEQ, D_MODEL), BF16)],
        compiler_params=pltpu.CompilerParams(dimension_semantics=("arbitrary",)),
        name="prenorm",
    )(x2, meta_block, w)


IN_OFFS = tuple(int(v) for v in np.concatenate([[0], np.cumsum(IN_SIZES)]))
HEAD_BLK = 2 * BLK
HEAD_ORDER = (0, 2, 1)
IDX_ROWS = IN_SIZES[3] + HEAD_BLK
W_REST = IN_OFFS[-1] - IN_OFFS[6]
REST_BLK = 4 * BLK


def _head_blocks():
    blocks = []
    for k in HEAD_ORDER:
        assert IN_OFFS[k] % HEAD_BLK == 0 and IN_SIZES[k] % HEAD_BLK == 0
        blocks += list(range(IN_OFFS[k] // HEAD_BLK, IN_OFFS[k + 1] // HEAD_BLK))
    return blocks


def _cast_kernel(w_ref, o_ref):
    o_ref[...] = w_ref[...].astype(o_ref.dtype)


def _wprep_head(wt):
    src = _head_blocks()

    def src_block(t):
        b = jnp.int32(src[-1])
        for pos in range(len(src) - 2, -1, -1):
            b = jnp.where(t == pos, src[pos], b)
        return b, 0
    return pl.pallas_call(
        _cast_kernel,
        grid=(len(src),),
        in_specs=[pl.BlockSpec((HEAD_BLK, D_MODEL), src_block)],
        out_specs=pl.BlockSpec((HEAD_BLK, D_MODEL), lambda t: (t, 0)),
        out_shape=jax.ShapeDtypeStruct((len(src) * HEAD_BLK, D_MODEL), BF16),
        compiler_params=pltpu.CompilerParams(dimension_semantics=("arbitrary",)),
        name="wprep_head",
    )(wt)


def _shift_cast_kernel(w_ref, o_ref, carry_ref, *, shift, n_out):
    s = pl.program_id(0)

    @pl.when(s > 0)
    def _():
        o_ref[...] = jnp.concatenate([carry_ref[...], w_ref[:shift]], axis=0).astype(o_ref.dtype)

    @pl.when(s < n_out)
    def _():
        carry_ref[...] = w_ref[shift:]


def _wprep_rest(wt):
    base, shift = divmod(IN_OFFS[6], REST_BLK)
    n_out = W_REST // REST_BLK
    assert shift % 8 == 0 and W_REST % REST_BLK == 0 and (base + n_out) * REST_BLK + shift == wt.shape[0]
    return pl.pallas_call(
        functools.partial(_shift_cast_kernel, shift=shift, n_out=n_out),
        grid=(n_out + 1,),
        in_specs=[pl.BlockSpec((REST_BLK, D_MODEL), lambda s: (base + s, 0))],
        out_specs=pl.BlockSpec((REST_BLK, D_MODEL), lambda s: (jnp.maximum(s - 1, 0), 0)),
        out_shape=jax.ShapeDtypeStruct((W_REST, D_MODEL), BF16),
        scratch_shapes=[pltpu.VMEM((REST_BLK - shift, D_MODEL), F32)],
        compiler_params=pltpu.CompilerParams(dimension_semantics=("arbitrary",), vmem_limit_bytes=VMEM_LIMIT),
        name="wprep_rest",
    )(wt)


def _mm_kernel(a_ref, w_ref, o_ref):
    acc = lax.dot_general(a_ref[...], w_ref[...], NT_DIMS, preferred_element_type=F32)
    o_ref[...] = acc.astype(o_ref.dtype)


def _mm_scaled_kernel(a_ref, w_ref, cs_ref, o_ref):
    acc = lax.dot_general(a_ref[...], w_ref[...], NT_DIMS, preferred_element_type=F32)
    o_ref[...] = (acc * cs_ref[...]).astype(o_ref.dtype)


def _matmul(a, wt, col0, n, out_dtype, tm, tn, name, col_scale=None):
    m, k = a.shape
    assert col0 % tn == 0 and n % tn == 0 and m % tm == 0 and wt.shape[1] == k
    c0 = col0 // tn
    resident = {"pipeline_mode": pl.Buffered(1)} if n == tn else {}
    in_specs = [pl.BlockSpec((tm, k), lambda i, j: (i, 0)),
                pl.BlockSpec((tn, k), lambda i, j: (c0 + j, 0), **resident)]
    args = (a, wt)
    if col_scale is not None:
        in_specs.append(pl.BlockSpec((1, tn), lambda i, j: (0, j)))
        args += (col_scale,)
    return pl.pallas_call(
        _mm_kernel if col_scale is None else _mm_scaled_kernel,
        grid=(m // tm, n // tn),
        in_specs=in_specs,
        out_specs=pl.BlockSpec((tm, tn), lambda i, j: (i, j)),
        out_shape=jax.ShapeDtypeStruct((m, n), out_dtype),
        compiler_params=pltpu.CompilerParams(
            dimension_semantics=("arbitrary", "arbitrary"), vmem_limit_bytes=VMEM_LIMIT),
        name=name,
    )(*args)


def _kvprep_kernel(c_ref, t_ref, v_ref, kvw_ref, ikw_ref, ikb_ref, ckv_ref, ckvt_ref, ik_ref, iwt_ref, vt_ref):
    for blk in range(2):
        for h in range(B_HEADS):
            vh = v_ref[blk * BLK:(blk + 1) * BLK, h * B_V_DIM:(h + 1) * B_V_DIM]
            vt_ref[blk, h] = vh.astype(F32).T.astype(BF16)
    ckv = c_ref[...]
    ms = jnp.mean(ckv * ckv, axis=-1, keepdims=True)
    ckvn = ckv * lax.rsqrt(ms + EPS) * kvw_ref[...]
    ckv_ref[...] = ckvn.astype(BF16)
    ckvt_ref[0] = ckvn[:BLK].T.astype(BF16)
    ckvt_ref[1] = ckvn[BLK:].T.astype(BF16)
    tail = t_ref[:, :BLK]
    ik = tail[:, :IDX_DIM]
    mu = jnp.mean(ik, axis=-1, keepdims=True)
    var = jnp.mean(jnp.square(ik - mu), axis=-1, keepdims=True)
    ik_ref[...] = (ik - mu) * lax.rsqrt(var + EPS) * ikw_ref[...] + ikb_ref[...]
    iwt_ref[...] = (tail * (IDX_HEADS ** -0.5 * IDX_DIM ** -0.5)).T[IDX_DIM:IDX_DIM + IDX_HEADS, :]


def _kvprep(c, idxp, kv_b, vcol, kvw, ikw, ikb):
    tm = 2 * BLK
    assert vcol % B_WIDTH == 0 and c.shape[1] == KV_RANK and idxp.shape[1] == IDX_ROWS
    return pl.pallas_call(
        _kvprep_kernel,
        grid=(ROWS // tm,),
        in_specs=[pl.BlockSpec((tm, KV_RANK), lambda i: (i, 0)),
                  pl.BlockSpec((tm, HEAD_BLK), lambda i: (i, IN_SIZES[3] // HEAD_BLK)),
                  pl.BlockSpec((tm, B_WIDTH), lambda i: (i, vcol // B_WIDTH)),
                  pl.BlockSpec((1, KV_RANK), lambda i: (0, 0)),
                  pl.BlockSpec((1, IDX_DIM), lambda i: (0, 0)),
                  pl.BlockSpec((1, IDX_DIM), lambda i: (0, 0))],
        out_specs=[pl.BlockSpec((tm, KV_RANK), lambda i: (i, 0)),
                   pl.BlockSpec((2, KV_RANK, BLK), lambda i: (i, 0, 0)),
                   pl.BlockSpec((tm, IDX_DIM), lambda i: (i, 0)),
                   pl.BlockSpec((IDX_HEADS, tm), lambda i: (0, i)),
                   pl.BlockSpec((2, B_HEADS, B_V_DIM, BLK), lambda i: (i, 0, 0, 0))],
        out_shape=[jax.ShapeDtypeStruct((ROWS, KV_RANK), BF16),
                   jax.ShapeDtypeStruct((ROWS // BLK, KV_RANK, BLK), BF16),
                   jax.ShapeDtypeStruct((ROWS, IDX_DIM), F32),
                   jax.ShapeDtypeStruct((IDX_HEADS, ROWS), F32),
                   jax.ShapeDtypeStruct((ROWS // BLK, B_HEADS, B_V_DIM, BLK), BF16)],
        name="kvprep",
    )(c, idxp, kv_b, kvw, ikw, ikb)


def _split_rows_dot(lhs, rhs, dims):
    rows = lhs.shape[0]
    if rows < 2 * BLK:
        return lax.dot_general(lhs, rhs, dims, preferred_element_type=F32)
    half = rows // 2
    return jnp.concatenate([lax.dot_general(lhs[:half], rhs, dims, preferred_element_type=F32),
                            lax.dot_general(lhs[half:], rhs, dims, preferred_element_type=F32)], axis=0)


def _online_softmax_step(s, m, l):
    m_new = jnp.maximum(m, jnp.max(s, axis=0, keepdims=True))
    alpha = jnp.exp2(m - m_new)
    p = jnp.exp2(s - m_new)
    return m_new, alpha * l + jnp.sum(p, axis=0, keepdims=True), alpha, p


def _attn_a_kernel(qa_ref, iq_ref, iwt_ref, ckv_ref, ckvt_ref, ik_ref, wuk_ref, wuvt_ref, bias_ref,
                   o_ref, keys_ref, qlat_ref, acc_ref, iqt_ref, thr_ref, nge_ref):
    i = pl.program_id(1)
    nkb = i + 2
    NG = A_HEADS // 2
    GW = 2 * BLK
    t_meta = jnp.where(i == 0, T_META0, T_METAFAR)
    t_prev = jnp.where(i == 0, T_NONE, T_PREV)
    special_blocks = (0, i, i + 1)

    for h in range(A_HEADS):
        qh = qa_ref[:, h * BLK:(h + 1) * BLK]
        ql = lax.dot_general(wuk_ref[h], qh, NT_DIMS, preferred_element_type=F32)
        qlat_ref[:, h * BLK:(h + 1) * BLK] = (ql * (A_HEAD_DIM ** -0.5 * LOG2E)).astype(BF16)

    for pr in range(IDX_HEADS // 2):
        t = iq_ref[:, pr * BLK:(pr + 1) * BLK].T
        iqt_ref[:, (2 * pr) * BLK:(2 * pr + 1) * BLK] = t[:IDX_DIM]
        iqt_ref[:, (2 * pr + 1) * BLK:(2 * pr + 2) * BLK] = t[IDX_DIM:]

    iwt = iwt_ref[...]
    row = lax.broadcasted_iota(jnp.int32, (BLK, BLK), 0)
    lane = lax.broadcasted_iota(jnp.int32, (BLK, BLK), 1)

    def idx_keys(ikrows):
        sc = jnp.zeros((ikrows.shape[0], BLK), F32)
        for pr in range(IDX_HEADS // 2):
            s2 = jnp.dot(ikrows, iqt_ref[:, pr * 2 * BLK:(pr + 1) * 2 * BLK], preferred_element_type=F32)
            sc = sc + jnp.maximum(s2[:, :BLK], 0.0) * iwt[2 * pr:2 * pr + 1, :]
            sc = sc + jnp.maximum(s2[:, BLK:], 0.0) * iwt[2 * pr + 1:2 * pr + 2, :]
        bits = lax.bitcast_convert_type(sc, jnp.int32)
        return bits ^ ((bits >> 31) & 0x7FFFFFFF)

    def put_keys(off, n, key):
        keys_ref[pl.ds(off, n * BLK), :] = key

    def idx_group(kb, n, carry):
        off = pl.multiple_of(kb * BLK, BLK)
        put_keys(off, n, idx_keys(ik_ref[pl.ds(off, n * BLK), :]))
        return carry

    def idx_special(carry, first, n_left):
        offs = [pl.multiple_of(kb * BLK, BLK) for kb in special_blocks]
        left_off = pl.multiple_of(first * BLK, BLK)
        left = [ik_ref[pl.ds(left_off, n_left * BLK), :]] if n_left else []
        key = idx_keys(jnp.concatenate(left + [ik_ref[pl.ds(o, BLK), :] for o in offs], axis=0))
        if n_left:
            put_keys(left_off, n_left, key[:n_left * BLK])
        key = key[n_left * BLK:]
        put_keys(offs[1], 1, key[BLK:2 * BLK])
        put_keys(offs[2], 1, jnp.where((row >= CHUNK) & (lane < CHUNK), INT_MIN, key[2 * BLK:]))
        put_keys(offs[0], 1, jnp.where(row >= N_META, INT_MIN, key[:BLK]))
        return carry

    _visit_key_blocks(i, idx_group, idx_special, 0)
    put_keys(pl.multiple_of(nkb * BLK, BLK), FAR - 1, jnp.full(((FAR - 1) * BLK, BLK), INT_MIN, jnp.int32))
    n_search = lax.shift_right_logical(nkb + FAR - 1, 2)
    crow = lax.broadcasted_iota(jnp.int32, (FAR * BLK, BLK), 0)
    chunk_off = lambda c: pl.multiple_of(c * FAR * BLK, FAR * BLK)

    def count(pred_fn):
        def body(c, acc8):
            k = keys_ref[pl.ds(chunk_off(c), FAR * BLK), :]
            hit = pred_fn(k, chunk_off(c)).astype(jnp.int32)
            return acc8 + jnp.sum(hit.reshape(FAR * BLK // 8, 8, BLK), axis=0)
        acc8 = lax.fori_loop(0, n_search, body, jnp.zeros((8, BLK), jnp.int32))
        return jnp.sum(acc8, axis=0, keepdims=True)

    zero = jnp.zeros((1, BLK), jnp.int32)
    c0 = count(lambda k, off: k >= zero)
    prefix = jnp.where(c0 >= TOPK, 0, INT_MIN).astype(jnp.int32)

    def bit_body(t, carry):
        prefix, n_ge = carry
        cand = prefix | jnp.left_shift(jnp.int32(1), 30 - t)
        c = count(lambda k, off: k >= cand)
        return jnp.where(c >= TOPK, cand, prefix), jnp.where(c >= TOPK, c, n_ge)

    head_passes = 31 - LATE_BITS
    thr_ref[...], nge_ref[...] = lax.fori_loop(0, head_passes, bit_body, (prefix, c0))

    @pl.when(jnp.max(nge_ref[...]) > TOPK)
    def _():
        thr_ref[...], nge_ref[...] = lax.fori_loop(head_passes, 31, bit_body, (thr_ref[...], nge_ref[...]))

    thr, n_ge = thr_ref[...], nge_ref[...]
    full = thr == INT_MIN
    tied = jnp.logical_and(jnp.logical_not(full), n_ge > TOPK)

    @pl.when(jnp.max(tied.astype(jnp.int32)) > 0)
    def _():
        need = TOPK - count(lambda k, off: k > thr)

        def jbit(t, j):
            cand = j | jnp.left_shift(jnp.int32(1), 12 - t)
            c = count(lambda k, off: (k == thr) & ((off + crow) < cand))
            return jnp.where(c < need, cand, j)
        jmax = lax.fori_loop(0, 13, jbit, jnp.zeros((1, BLK), jnp.int32))

        def strike(c, carry):
            k = keys_ref[pl.ds(chunk_off(c), FAR * BLK), :]
            surplus = tied & (k == thr) & ((chunk_off(c) + crow) > jmax)
            keys_ref[pl.ds(chunk_off(c), FAR * BLK), :] = jnp.where(surplus, INT_MIN, k)
            return carry
        lax.fori_loop(0, n_search, strike, 0)

    thr_sel = jnp.where(full, INT_MIN + 1, thr)

    acc_ref[...] = jnp.zeros_like(acc_ref)

    def att_update(rows, keys, vt, bias, carry):
        selb = jnp.where(keys >= thr_sel, 0.0, NEG)
        add = jnp.concatenate([selb, selb], axis=1)
        gs = [slice(g * GW, (g + 1) * GW) for g in range(NG)]
        ss = [_split_rows_dot(rows, qlat_ref[:, gs[g]], NN_DIMS) for g in range(NG)]
        ss = [ss[g] + (add if bias is None else add + bias[:, gs[g]]) for g in range(NG)]
        steps = [_online_softmax_step(ss[g], *carry[g]) for g in range(NG)]
        pvs = [jnp.dot(vt, steps[g][3].astype(BF16), preferred_element_type=F32) for g in range(NG)]
        for g in range(NG):
            acc_ref[:, gs[g]] = acc_ref[:, gs[g]] * steps[g][2] + pvs[g]
        return tuple((steps[g][0], steps[g][1]) for g in range(NG))

    def att_group(kb, n, carry):
        off = pl.multiple_of(kb * BLK, BLK)
        vt = jnp.concatenate([ckvt_ref[kb + u] for u in range(n)], axis=1) if n > 1 else ckvt_ref[kb]
        return att_update(ckv_ref[pl.ds(off, n * BLK), :], keys_ref[pl.ds(off, n * BLK), :], vt, None, carry)

    def att_special(carry, first, n_left):
        offs = [pl.multiple_of(kb * BLK, BLK) for kb in special_blocks]
        left_off = pl.multiple_of(first * BLK, BLK)
        left = lambda ref: [ref[pl.ds(left_off, n_left * BLK), :]] if n_left else []
        rows = jnp.concatenate(left(ckv_ref) + [ckv_ref[pl.ds(o, BLK), :] for o in offs], axis=0)
        keys = jnp.concatenate(left(keys_ref) + [keys_ref[pl.ds(o, BLK), :] for o in offs], axis=0)
        vt = jnp.concatenate([ckvt_ref[first + u] for u in range(n_left)] + [ckvt_ref[kb] for kb in special_blocks],
                             axis=1)
        no_bias = [jnp.zeros((n_left * BLK, A_HEADS * BLK), F32)] if n_left else []
        bias = jnp.concatenate(no_bias + [bias_ref[t_meta], bias_ref[t_prev], bias_ref[T_DIAG]], axis=0)
        return att_update(rows, keys, vt, bias, carry)

    m0 = jnp.full((1, GW), NEG, F32)
    l0 = jnp.zeros((1, GW), F32)
    stats = _visit_key_blocks(i, att_group, att_special, tuple((m0, l0) for _ in range(NG)), sizes=ATT_FAR_SIZES)
    l = jnp.concatenate([stats[g][1] for g in range(NG)], axis=1)

    olat = (acc_ref[...] / l).astype(BF16)
    ots = [jnp.dot(wuvt_ref[h], olat[:, h * BLK:(h + 1) * BLK], preferred_element_type=F32) for h in range(A_HEADS)]
    for h in range(A_HEADS):
        o_ref[:, h * BLK:(h + 1) * BLK] = ots[h].T


def _attn_a(q_a, idxp, iwt, ckv, ckvt, ik, wuk, wuvt, bias_a):
    qrow = lambda b, i: b * NKB + 1 + i
    return pl.pallas_call(
        _attn_a_kernel,
        grid=(BATCH, NQB),
        in_specs=[
            pl.BlockSpec((BLK, A_WIDTH), lambda b, i: (qrow(b, i), 0)),
            pl.BlockSpec((BLK, IDX_HEADS * IDX_DIM), lambda b, i: (qrow(b, i), 0)),
            pl.BlockSpec((IDX_HEADS, BLK), lambda b, i: (0, qrow(b, i))),
            pl.BlockSpec((None, TP, KV_RANK), lambda b, i: (b, 0, 0)),
            pl.BlockSpec((None, NKB, KV_RANK, BLK), lambda b, i: (b, 0, 0, 0)),
            pl.BlockSpec((None, TP, IDX_DIM), lambda b, i: (b, 0, 0)),
            pl.BlockSpec((A_HEADS, KV_RANK, A_HEAD_DIM), lambda b, i: (0, 0, 0)),
            pl.BlockSpec((A_HEADS, A_HEAD_DIM, KV_RANK), lambda b, i: (0, 0, 0)),
            pl.BlockSpec((5, BLK, A_HEADS * BLK), lambda b, i: (0, 0, 0)),
        ],
        out_specs=pl.BlockSpec((BLK, A_WIDTH), lambda b, i: (b * NQB + i, 0)),
        out_shape=jax.ShapeDtypeStruct((BATCH * SEQ, A_WIDTH), F32),
        scratch_shapes=[pltpu.VMEM(((NKB + FAR - 1) * BLK, BLK), jnp.int32),
                        pltpu.VMEM((KV_RANK, A_HEADS * BLK), BF16),
                        pltpu.VMEM((KV_RANK, A_HEADS * BLK), F32),
                        pltpu.VMEM((IDX_DIM, IDX_HEADS * BLK), F32),
                        pltpu.VMEM((1, BLK), jnp.int32),
                        pltpu.VMEM((1, BLK), jnp.int32)],
        compiler_params=pltpu.CompilerParams(
            dimension_semantics=("arbitrary", "arbitrary"), vmem_limit_bytes=VMEM_LIMIT),
        name="attn_a",
    )(q_a, idxp, iwt, ckv.reshape(BATCH, TP, KV_RANK), ckvt.reshape(BATCH, NKB, KV_RANK, BLK),
      ik.reshape(BATCH, TP, IDX_DIM), wuk, wuvt, bias_a)


def _attn_b_kernel(lam_ref, q_ref, k_ref, vt_ref, bias_ref, subw_ref, o_ref, acc_ref, *, lam_init):
    i = pl.program_id(2)
    t_meta = jnp.where(i == 0, T_META0, T_METAFAR)
    t_prev = jnp.where(i == 0, T_NONE, T_PREV)
    special_blocks = (0, i, i + 1)
    lp = lam_ref[...]
    lam = (jnp.exp(jnp.sum(lp[0:1] * lp[1:2], axis=-1, keepdims=True))
           - jnp.exp(jnp.sum(lp[2:3] * lp[3:4], axis=-1, keepdims=True)) + lam_init)

    lane = lax.broadcasted_iota(jnp.int32, (BLK, BLK), 1)
    qbd = []
    for hh in range(HPS):
        q = q_ref[:, hh * BLK:(hh + 1) * BLK]
        zq = jnp.zeros_like(q)
        qbd.append(jnp.concatenate([jnp.where(lane < B_QK_DIM, q, zq), jnp.where(lane >= B_QK_DIM, q, zq)], axis=0))

    acc_ref[...] = jnp.zeros_like(acc_ref)

    def update_all(rows, vts, biases, carry):
        ss = [_split_rows_dot(rows[hh], qbd[hh], NT_DIMS) for hh in range(HPS)]
        if biases is not None:
            ss = [ss[hh] + jnp.concatenate([biases[hh], biases[hh]], axis=1) for hh in range(HPS)]
        steps = [_online_softmax_step(ss[hh], *carry[hh]) for hh in range(HPS)]
        pvs = [jnp.dot(vts[hh], steps[hh][3].astype(BF16), preferred_element_type=F32) for hh in range(HPS)]
        for hh in range(HPS):
            acc_ref[hh] = acc_ref[hh] * steps[hh][2] + pvs[hh]
        return tuple((steps[hh][0], steps[hh][1]) for hh in range(HPS))

    def group(kb, n, carry):
        off = pl.multiple_of(kb * BLK, BLK)
        rows = [k_ref[pl.ds(off, n * BLK), hh * BLK:(hh + 1) * BLK] for hh in range(HPS)]
        vts = [jnp.concatenate([vt_ref[kb + u, hh] for u in range(n)], axis=1) if n > 1 else vt_ref[kb, hh]
               for hh in range(HPS)]
        return update_all(rows, vts, None, carry)

    def special(carry, first, n_left):
        offs = [pl.multiple_of(kb * BLK, BLK) for kb in special_blocks]
        left_off = pl.multiple_of(first * BLK, BLK)
        cols = lambda hh: slice(hh * BLK, (hh + 1) * BLK)
        left = lambda hh: [k_ref[pl.ds(left_off, n_left * BLK), cols(hh)]] if n_left else []
        rows = [jnp.concatenate(left(hh) + [k_ref[pl.ds(o, BLK), cols(hh)] for o in offs], axis=0)
                for hh in range(HPS)]
        vts = [jnp.concatenate([vt_ref[first + u, hh] for u in range(n_left)]
                               + [vt_ref[kb, hh] for kb in special_blocks], axis=1) for hh in range(HPS)]
        no_bias = [jnp.zeros((n_left * BLK, BLK), F32)] if n_left else []
        biases = [jnp.concatenate(no_bias + [bias_ref[t_meta, hh], bias_ref[t_prev, hh], bias_ref[T_DIAG, hh]], axis=0)
                  for hh in range(HPS)]
        return update_all(rows, vts, biases, carry)

    m0 = jnp.full((1, 2 * BLK), NEG, F32)
    l0 = jnp.zeros((1, 2 * BLK), F32)
    stats = _visit_key_blocks(i, group, special, tuple((m0, l0) for _ in range(HPS)), sizes=ATT_FAR_SIZES)

    for hh in range(HPS):
        a = acc_ref[hh] / stats[hh][1]
        o = a[:, :BLK] - lam * a[:, BLK:]
        ms = jnp.mean(o * o, axis=0, keepdims=True)
        y = o * lax.rsqrt(ms + EPS) * subw_ref[...] * (1.0 - lam_init)
        o_ref[:, hh * BLK:(hh + 1) * BLK] = y.T


def _attn_b(qkv_b, vt, bias_b, lam_p, subw, lam_init):
    qrow = lambda b, g, i: b * NKB + 1 + i
    wide = HPS * BLK
    qcol0 = 0
    kcol0 = 2 * B_HEADS * B_QK_DIM // wide
    return pl.pallas_call(
        functools.partial(_attn_b_kernel, lam_init=lam_init),
        grid=(BATCH, B_HEADS // HPS, NQB),
        in_specs=[
            pl.BlockSpec((4, B_QK_DIM), lambda b, g, i: (0, 0)),
            pl.BlockSpec((BLK, wide), lambda b, g, i: (qrow(b, g, i), qcol0 + g)),
            pl.BlockSpec((None, TP, wide), lambda b, g, i: (b, 0, kcol0 + g)),
            pl.BlockSpec((None, NKB, HPS, B_V_DIM, BLK), lambda b, g, i: (b, 0, g, 0, 0)),
            pl.BlockSpec((5, HPS, BLK, BLK), lambda b, g, i: (0, g, 0, 0)),
            pl.BlockSpec((B_V_DIM, BLK), lambda b, g, i: (0, 0)),
        ],
        out_specs=pl.BlockSpec((BLK, wide), lambda b, g, i: (b * NQB + i, g)),
        out_shape=jax.ShapeDtypeStruct((BATCH * SEQ, B_WIDTH), F32),
        scratch_shapes=[pltpu.VMEM((HPS, B_V_DIM, 2 * BLK), F32)],
        compiler_params=pltpu.CompilerParams(
            dimension_semantics=("arbitrary", "arbitrary", "arbitrary"), vmem_limit_bytes=VMEM_LIMIT),
        name="attn_b",
    )(lam_p, qkv_b, qkv_b.reshape(BATCH, TP, -1), vt, bias_b, subw)


def _out_kernel(oa_ref, za_ref, ob_ref, zb_ref, ga_ref, gb_ref, x_ref, woa_ref, wob_ref, wout_ref, pw_ref, o_ref):
    a = (oa_ref[...] * jax.nn.silu(za_ref[...])).astype(BF16)
    ya = jnp.dot(a, woa_ref[...], preferred_element_type=F32)
    b = (ob_ref[...] * jax.nn.silu(zb_ref[...])).astype(BF16)
    yb = jnp.dot(b, wob_ref[...], preferred_element_type=F32)
    mix = jax.nn.sigmoid(ga_ref[...]) * ya + jax.nn.sigmoid(gb_ref[...]) * yb
    out = jnp.dot(mix.astype(BF16), wout_ref[...], preferred_element_type=F32)
    ms = jnp.mean(out * out, axis=-1, keepdims=True)
    o_ref[...] = x_ref[...] + out * lax.rsqrt(ms + EPS) * pw_ref[...]


def _out_stage(o_a, o_b, z_a, z_b, gates, x2, woa, wob, wout, pw):
    tm = 2 * BLK
    const = lambda g: (0, 0)
    return pl.pallas_call(
        _out_kernel,
        grid=(BATCH * SEQ // tm,),
        in_specs=[
            pl.BlockSpec((tm, A_WIDTH), lambda g: (g, 0)),
            pl.BlockSpec((tm, A_WIDTH), lambda g: (g, 0)),
            pl.BlockSpec((tm, B_WIDTH), lambda g: (g, 0)),
            pl.BlockSpec((tm, B_WIDTH), lambda g: (g, 0)),
            pl.BlockSpec((tm, D_MODEL), lambda g: (g, 0)),
            pl.BlockSpec((tm, D_MODEL), lambda g: (g, 1)),
            pl.BlockSpec((tm, D_MODEL), lambda g: (g, 0)),
            pl.BlockSpec((A_WIDTH, D_MODEL), const, pipeline_mode=pl.Buffered(1)),
            pl.BlockSpec((B_WIDTH, D_MODEL), const, pipeline_mode=pl.Buffered(1)),
            pl.BlockSpec((D_MODEL, D_MODEL), const, pipeline_mode=pl.Buffered(1)),
            pl.BlockSpec((1, D_MODEL), const),
        ],
        out_specs=pl.BlockSpec((tm, D_MODEL), lambda g: (g, 0)),
        out_shape=jax.ShapeDtypeStruct((BATCH * SEQ, D_MODEL), F32),
        compiler_params=pltpu.CompilerParams(
            dimension_semantics=("arbitrary",), vmem_limit_bytes=VMEM_LIMIT),
        name="out_stage",
    )(o_a, z_a, o_b, z_b, gates, gates, x2, woa, wob, wout, pw)


def kernel(x, meta_tokens, rel_bias, pre_norm_w, w_in, kv_norm_w, w_uk, w_uv, idx_k_norm_w, idx_k_norm_b,
           diff_lambda, diff_subln_w, w_o_a, w_o_b, w_out, post_norm_w):
    assert x.shape == (BATCH, SEQ, D_MODEL) and w_in.shape[0] == 1
    layer = 0
    lam_init = 0.8 - 0.6 * math.exp(-0.3 * layer)

    meta_block = jnp.concatenate([meta_tokens.astype(F32), jnp.zeros((BLK - N_META, D_MODEL), F32)], axis=0)

    wt = w_in[0].T
    w_head, w_rest = _wprep_head(wt), _wprep_rest(wt)
    wd = A_WIDTH
    assert all(IN_SIZES[k] == wd for k in (0, 2, 3, 6, 7, 8, 9)) and IN_SIZES[10] == IN_SIZES[11] == 2 * wd
    scale_qb = jnp.concatenate([jnp.full((1, wd), B_QK_DIM ** -0.5 * LOG2E, F32), jnp.ones((1, 2 * wd), F32)], axis=1)

    u, u32, u_f = _prenorm(x.reshape(BATCH * SEQ, D_MODEL), meta_block, pre_norm_w[0][None].astype(F32))
    tm, tmf = ROWS // 8, BATCH * SEQ // 8
    q_a = _matmul(u, w_head, 0, wd, BF16, tm, wd, "proj_q_a")
    lat = _matmul(u, w_head, 2 * wd, KV_RANK, F32, tm, KV_RANK, "proj_latent")
    qkv_b = _matmul(u, w_rest, 0, 3 * wd, BF16, tm, wd, "proj_qkv_b", col_scale=scale_qb)
    w_idx = wt[IN_OFFS[3]:IN_OFFS[3] + IDX_ROWS]
    idxp = _matmul(u32, w_idx, 0, IDX_ROWS, F32, tm, IDX_ROWS, "proj_indexer")
    z_a = _matmul(u_f, w_head, wd, wd, F32, tmf, wd, "proj_z_a")
    z_b = _matmul(u_f, w_rest, 3 * wd, wd, F32, tmf, wd, "proj_z_b")
    gates = _matmul(u_f, w_rest, 4 * wd, 4 * wd, F32, tmf, wd, "proj_gates")

    ckv, ckvt, ik, iwt, vt = _kvprep(lat, idxp, qkv_b, 2 * wd, kv_norm_w[0][None].astype(F32),
                                     idx_k_norm_w[0][None].astype(F32), idx_k_norm_b[0][None].astype(F32))

    bias = _bias_tiles(rel_bias) * LOG2E
    bias_a = jnp.transpose(bias[:, :A_HEADS], (0, 2, 1, 3)).reshape(5, BLK, A_HEADS * BLK)
    bias_b = bias[:, A_HEADS:]

    wuk = jnp.transpose(w_uk[0], (1, 0, 2)).astype(BF16)
    wuvt = jnp.transpose(w_uv[0], (1, 2, 0)).astype(BF16)
    o_a = _attn_a(q_a, idxp, iwt, ckv, ckvt, ik, wuk, wuvt, bias_a)

    vt = vt.reshape(BATCH, NKB, B_HEADS, B_V_DIM, BLK)
    subw = jnp.broadcast_to(diff_subln_w[0].astype(F32)[:, None], (B_V_DIM, BLK))
    o_b = _attn_b(qkv_b, vt, bias_b, diff_lambda[0].astype(F32), subw, lam_init)

    out = _out_stage(o_a, o_b, z_a, z_b, gates, x.reshape(BATCH * SEQ, D_MODEL),
                     w_o_a[0].astype(BF16), w_o_b[0].astype(BF16), w_out[0].astype(BF16),
                     post_norm_w[0][None].astype(F32))
    return out.reshape(BATCH, SEQ, D_MODEL)
```

```python
import functools
import math

import numpy as np
import jax
import jax.numpy as jnp
from jax import lax
from jax.experimental import pallas as pl
from jax.experimental.pallas import tpu as pltpu

D_MODEL = 2048
BATCH = 2
SEQ = 4096
CHUNK = 64
N_META = 16
N_BUCKETS = 32
MAX_DISTANCE = 128
A_HEADS = 8
A_HEAD_DIM = 128
KV_RANK = 256
IDX_HEADS = 16
IDX_DIM = 64
TOPK = 256
B_HEADS = 8
B_QK_DIM = 64
B_V_DIM = 128
A_WIDTH = A_HEADS * A_HEAD_DIM
B_WIDTH = B_HEADS * B_V_DIM
IN_SIZES = (A_WIDTH, KV_RANK, A_WIDTH, IDX_HEADS * IDX_DIM, IDX_DIM, IDX_HEADS,
            2 * B_HEADS * B_QK_DIM, 2 * B_HEADS * B_QK_DIM, B_WIDTH, B_WIDTH,
            D_MODEL, D_MODEL)
EPS = 1e-6

BLK = 128
NQB = SEQ // BLK
NKB = NQB + 1
TP = NKB * BLK
ROWS = BATCH * TP
FAR = 4
HPS = 8
ATT_FAR_SIZES = (2 * FAR, FAR)
NEG = -1e30
INT_MIN = -2 ** 31
LATE_BITS = 6
LOG2E = math.log2(math.e)
V7X_VMEM_BYTES = 64 * 2 ** 20
VMEM_LIMIT = V7X_VMEM_BYTES * 7 // 8

F32 = jnp.float32
BF16 = jnp.bfloat16
NT_DIMS = (((1,), (1,)), ((), ()))
NN_DIMS = (((1,), (0,)), ((), ()))


def _t5_bucket_np(rel):
    nb = N_BUCKETS // 2
    max_exact = nb // 2
    ret = np.where(rel > 0, nb, 0)
    n = np.abs(rel)
    nf = np.maximum(n, 1).astype(np.float32)
    large = max_exact + (np.log(nf / np.float32(max_exact))
                         / np.float32(math.log(MAX_DISTANCE / max_exact))
                         * np.float32(nb - max_exact)).astype(np.int32)
    large = np.minimum(large, nb - 1)
    return ret + np.where(n < max_exact, n, large)


T_DIAG, T_PREV, T_META0, T_METAFAR, T_NONE = range(5)


def _bias_tiles(rel_bias):
    a = np.arange(BLK)[:, None]
    b = np.arange(BLK)[None, :]
    nowhere = np.zeros((BLK, BLK), bool)
    pad_rows = (a >= N_META) | nowhere
    rels = np.stack([a - b, a - b - BLK, a - N_META - b])
    dis = np.stack([(a >= CHUNK) & (b < CHUNK), nowhere, pad_rows, pad_rows, ~nowhere])
    idx = _t5_bucket_np(rels)
    far_bucket = N_BUCKETS // 2 - 1
    assert _t5_bucket_np(np.array([-BLK - 1]))[0] == far_bucket
    rb = rel_bias.astype(F32)
    heads = A_HEADS + B_HEADS
    tiles = jnp.zeros((3, heads, BLK, BLK), F32)
    for k in np.unique(idx):
        tiles = jnp.where((idx == k)[:, None], rb[k][None, :, None, None], tiles)
    tiles = tiles - rb[far_bucket][None, :, None, None]
    tiles = jnp.concatenate([tiles, jnp.zeros((2, heads, BLK, BLK), F32)], axis=0)
    return jnp.where(dis[:, None], NEG, tiles)


def _visit_key_blocks(i, group_fn, special_fn, carry, sizes=(FAR,)):
    n_far = jnp.maximum(i - 1, 0)
    start = jnp.int32(1)
    for size in sizes:
        shift = size.bit_length() - 1
        assert size == 1 << shift
        n_groups = lax.shift_right_logical(1 + n_far - start, shift)
        carry = lax.fori_loop(0, n_groups, lambda c, cr, s=start, z=size: group_fn(s + z * c, z, cr), carry)
        start = start + size * n_groups
    branches = [functools.partial(special_fn, first=start, n_left=r) for r in range(sizes[-1])]
    return lax.switch(1 + n_far - start, branches, carry)


def _prenorm_kernel(x_ref, meta_ref, w_ref, o_ref, o32_ref, of_ref):
    def norm(x):
        ms = jnp.mean(x * x, axis=-1, keepdims=True)
        return x * lax.rsqrt(ms + EPS) * w_ref[...]

    is_meta = lax.rem(pl.program_id(0), NKB) == 0

    @pl.when(is_meta)
    def _():
        u = norm(meta_ref[...])
        o32_ref[...] = u
        o_ref[...] = u.astype(o_ref.dtype)

    @pl.when(jnp.logical_not(is_meta))
    def _():
        u = norm(x_ref[...])
        o32_ref[...] = u
        o_ref[...] = u.astype(o_ref.dtype)
        of_ref[...] = u.astype(of_ref.dtype)


def _prenorm(x2, meta_block, w):
    frame_block = lambda r: (jnp.maximum(r - r // NKB - 1, 0), 0)
    return pl.pallas_call(
        _prenorm_kernel,
        grid=(ROWS // BLK,),
        in_specs=[pl.BlockSpec((BLK, D_MODEL), frame_block),
                  pl.BlockSpec((BLK, D_MODEL), lambda r: (0, 0)),
                  pl.BlockSpec((1, D_MODEL), lambda r: (0, 0))],
        out_specs=[pl.BlockSpec((BLK, D_MODEL), lambda r: (r, 0)),
                   pl.BlockSpec((BLK, D_MODEL), lambda r: (r, 0)),
                   pl.BlockSpec((BLK, D_MODEL), frame_block)],
        out_shape=[jax.ShapeDtypeStruct((ROWS, D_MODEL), BF16),
                   jax.ShapeDtypeStruct((ROWS, D_MODEL), F32),
                   jax.ShapeDtypeStruct((BATCH * SEQ, D_MODEL), BF16)],
        compiler_params=pltpu.CompilerParams(dimension_semantics=("arbitrary",)),
        name="prenorm",
    )(x2, meta_block, w)


IN_OFFS = tuple(int(v) for v in np.concatenate([[0], np.cumsum(IN_SIZES)]))
HEAD_BLK = 2 * BLK
HEAD_ORDER = (0, 2, 1)
IDX_ROWS = IN_SIZES[3] + HEAD_BLK
W_REST = IN_OFFS[-1] - IN_OFFS[6]
REST_BLK = 4 * BLK


def _head_blocks():
    blocks = []
    for k in HEAD_ORDER:
        assert IN_OFFS[k] % HEAD_BLK == 0 and IN_SIZES[k] % HEAD_BLK == 0
        blocks += list(range(IN_OFFS[k] // HEAD_BLK, IN_OFFS[k + 1] // HEAD_BLK))
    return blocks


def _cast_kernel(w_ref, o_ref):
    o_ref[...] = w_ref[...].astype(o_ref.dtype)


def _wprep_head(wt):
    src = _head_blocks()

    def src_block(t):
        b = jnp.int32(src[-1])
        for pos in range(len(src) - 2, -1, -1):
            b = jnp.where(t == pos, src[pos], b)
        return b, 0
    return pl.pallas_call(
        _cast_kernel,
        grid=(len(src),),
        in_specs=[pl.BlockSpec((HEAD_BLK, D_MODEL), src_block)],
        out_specs=pl.BlockSpec((HEAD_BLK, D_MODEL), lambda t: (t, 0)),
        out_shape=jax.ShapeDtypeStruct((len(src) * HEAD_BLK, D_MODEL), BF16),
        compiler_params=pltpu.CompilerParams(dimension_semantics=("arbitrary",)),
        name="wprep_head",
    )(wt)


def _shift_cast_kernel(w_ref, o_ref, carry_ref, *, shift, n_out):
    s = pl.program_id(0)

    @pl.when(s > 0)
    def _():
        o_ref[...] = jnp.concatenate([carry_ref[...], w_ref[:shift]], axis=0).astype(o_ref.dtype)

    @pl.when(s < n_out)
    def _():
        carry_ref[...] = w_ref[shift:]


def _wprep_rest(wt):
    base, shift = divmod(IN_OFFS[6], REST_BLK)
    n_out = W_REST // REST_BLK
    assert shift % 8 == 0 and W_REST % REST_BLK == 0 and (base + n_out) * REST_BLK + shift == wt.shape[0]
    return pl.pallas_call(
        functools.partial(_shift_cast_kernel, shift=shift, n_out=n_out),
        grid=(n_out + 1,),
        in_specs=[pl.BlockSpec((REST_BLK, D_MODEL), lambda s: (base + s, 0))],
        out_specs=pl.BlockSpec((REST_BLK, D_MODEL), lambda s: (jnp.maximum(s - 1, 0), 0)),
        out_shape=jax.ShapeDtypeStruct((W_REST, D_MODEL), BF16),
        scratch_shapes=[pltpu.VMEM((REST_BLK - shift, D_MODEL), F32)],
        compiler_params=pltpu.CompilerParams(dimension_semantics=("arbitrary",), vmem_limit_bytes=VMEM_LIMIT),
        name="wprep_rest",
    )(wt)


def _mm_kernel(a_ref, w_ref, o_ref):
    acc = lax.dot_general(a_ref[...], w_ref[...], NT_DIMS, preferred_element_type=F32)
    o_ref[...] = acc.astype(o_ref.dtype)


def _mm_scaled_kernel(a_ref, w_ref, cs_ref, o_ref):
    acc = lax.dot_general(a_ref[...], w_ref[...], NT_DIMS, preferred_element_type=F32)
    o_ref[...] = (acc * cs_ref[...]).astype(o_ref.dtype)


def _matmul(a, wt, col0, n, out_dtype, tm, tn, name, col_scale=None):
    m, k = a.shape
    assert col0 % tn == 0 and n % tn == 0 and m % tm == 0 and wt.shape[1] == k
    c0 = col0 // tn
    resident = {"pipeline_mode": pl.Buffered(1)} if n == tn else {}
    in_specs = [pl.BlockSpec((tm, k), lambda i, j: (i, 0)),
                pl.BlockSpec((tn, k), lambda i, j: (c0 + j, 0), **resident)]
    args = (a, wt)
    if col_scale is not None:
        in_specs.append(pl.BlockSpec((1, tn), lambda i, j: (0, j)))
        args += (col_scale,)
    return pl.pallas_call(
        _mm_kernel if col_scale is None else _mm_scaled_kernel,
        grid=(m // tm, n // tn),
        in_specs=in_specs,
        out_specs=pl.BlockSpec((tm, tn), lambda i, j: (i, j)),
        out_shape=jax.ShapeDtypeStruct((m, n), out_dtype),
        compiler_params=pltpu.CompilerParams(
            dimension_semantics=("arbitrary", "arbitrary"), vmem_limit_bytes=VMEM_LIMIT),
        name=name,
    )(*args)


def _kvprep_kernel(c_ref, t_ref, v_ref, kvw_ref, ikw_ref, ikb_ref, ckv_ref, ckvt_ref, ik_ref, iwt_ref, vt_ref):
    for blk in range(2):
        for h in range(B_HEADS):
            vh = v_ref[blk * BLK:(blk + 1) * BLK, h * B_V_DIM:(h + 1) * B_V_DIM]
            vt_ref[blk, h] = vh.astype(F32).T.astype(BF16)
    ckv = c_ref[...]
    ms = jnp.mean(ckv * ckv, axis=-1, keepdims=True)
    ckvn = ckv * lax.rsqrt(ms + EPS) * kvw_ref[...]
    ckv_ref[...] = ckvn.astype(BF16)
    ckvt_ref[0] = ckvn[:BLK].T.astype(BF16)
    ckvt_ref[1] = ckvn[BLK:].T.astype(BF16)
    tail = t_ref[:, :BLK]
    ik = tail[:, :IDX_DIM]
    mu = jnp.mean(ik, axis=-1, keepdims=True)
    var = jnp.mean(jnp.square(ik - mu), axis=-1, keepdims=True)
    ik_ref[...] = (ik - mu) * lax.rsqrt(var + EPS) * ikw_ref[...] + ikb_ref[...]
    iwt_ref[...] = (tail * (IDX_HEADS ** -0.5 * IDX_DIM ** -0.5)).T[IDX_DIM:IDX_DIM + IDX_HEADS, :]


def _kvprep(c, idxp, kv_b, vcol, kvw, ikw, ikb):
    tm = 2 * BLK
    assert vcol % B_WIDTH == 0 and c.shape[1] == KV_RANK and idxp.shape[1] == IDX_ROWS
    return pl.pallas_call(
        _kvprep_kernel,
        grid=(ROWS // tm,),
        in_specs=[pl.BlockSpec((tm, KV_RANK), lambda i: (i, 0)),
                  pl.BlockSpec((tm, HEAD_BLK), lambda i: (i, IN_SIZES[3] // HEAD_BLK)),
                  pl.BlockSpec((tm, B_WIDTH), lambda i: (i, vcol // B_WIDTH)),
                  pl.BlockSpec((1, KV_RANK), lambda i: (0, 0)),
                  pl.BlockSpec((1, IDX_DIM), lambda i: (0, 0)),
                  pl.BlockSpec((1, IDX_DIM), lambda i: (0, 0))],
        out_specs=[pl.BlockSpec((tm, KV_RANK), lambda i: (i, 0)),
                   pl.BlockSpec((2, KV_RANK, BLK), lambda i: (i, 0, 0)),
                   pl.BlockSpec((tm, IDX_DIM), lambda i: (i, 0)),
                   pl.BlockSpec((IDX_HEADS, tm), lambda i: (0, i)),
                   pl.BlockSpec((2, B_HEADS, B_V_DIM, BLK), lambda i: (i, 0, 0, 0))],
        out_shape=[jax.ShapeDtypeStruct((ROWS, KV_RANK), BF16),
                   jax.ShapeDtypeStruct((ROWS // BLK, KV_RANK, BLK), BF16),
                   jax.ShapeDtypeStruct((ROWS, IDX_DIM), F32),
                   jax.ShapeDtypeStruct((IDX_HEADS, ROWS), F32),
                   jax.ShapeDtypeStruct((ROWS // BLK, B_HEADS, B_V_DIM, BLK), BF16)],
        name="kvprep",
    )(c, idxp, kv_b, kvw, ikw, ikb)


def _split_rows_dot(lhs, rhs, dims):
    rows = lhs.shape[0]
    if rows < 2 * BLK:
        return lax.dot_general(lhs, rhs, dims, preferred_element_type=F32)
    half = rows // 2
    return jnp.concatenate([lax.dot_general(lhs[:half], rhs, dims, preferred_element_type=F32),
                            lax.dot_general(lhs[half:], rhs, dims, preferred_element_type=F32)], axis=0)


def _online_softmax_step(s, m, l):
    m_new = jnp.maximum(m, jnp.max(s, axis=0, keepdims=True))
    alpha = jnp.exp2(m - m_new)
    p = jnp.exp2(s - m_new)
    return m_new, alpha * l + jnp.sum(p, axis=0, keepdims=True), alpha, p


def _attn_a_kernel(qa_ref, iq_ref, iwt_ref, ckv_ref, ckvt_ref, ik_ref, wuk_ref, wuvt_ref, bias_ref,
                   o_ref, keys_ref, qlat_ref, acc_ref, iqt_ref, thr_ref, nge_ref):
    i = pl.program_id(1)
    nkb = i + 2
    NG = A_HEADS // 2
    GW = 2 * BLK
    t_meta = jnp.where(i == 0, T_META0, T_METAFAR)
    t_prev = jnp.where(i == 0, T_NONE, T_PREV)
    special_blocks = (0, i, i + 1)

    for h in range(A_HEADS):
        qh = qa_ref[:, h * BLK:(h + 1) * BLK]
        ql = lax.dot_general(wuk_ref[h], qh, NT_DIMS, preferred_element_type=F32)
        qlat_ref[:, h * BLK:(h + 1) * BLK] = (ql * (A_HEAD_DIM ** -0.5 * LOG2E)).astype(BF16)

    for pr in range(IDX_HEADS // 2):
        t = iq_ref[:, pr * BLK:(pr + 1) * BLK].T
        iqt_ref[:, (2 * pr) * BLK:(2 * pr + 1) * BLK] = t[:IDX_DIM]
        iqt_ref[:, (2 * pr + 1) * BLK:(2 * pr + 2) * BLK] = t[IDX_DIM:]

    iwt = iwt_ref[...]
    row = lax.broadcasted_iota(jnp.int32, (BLK, BLK), 0)
    lane = lax.broadcasted_iota(jnp.int32, (BLK, BLK), 1)

    def idx_keys(ikrows):
        sc = jnp.zeros((ikrows.shape[0], BLK), F32)
        for pr in range(IDX_HEADS // 2):
            s2 = jnp.dot(ikrows, iqt_ref[:, pr * 2 * BLK:(pr + 1) * 2 * BLK], preferred_element_type=F32)
            sc = sc + jnp.maximum(s2[:, :BLK], 0.0) * iwt[2 * pr:2 * pr + 1, :]
            sc = sc + jnp.maximum(s2[:, BLK:], 0.0) * iwt[2 * pr + 1:2 * pr + 2, :]
        bits = lax.bitcast_convert_type(sc, jnp.int32)
        return bits ^ ((bits >> 31) & 0x7FFFFFFF)

    def put_keys(off, n, key):
        keys_ref[pl.ds(off, n * BLK), :] = key

    def idx_group(kb, n, carry):
        off = pl.multiple_of(kb * BLK, BLK)
        put_keys(off, n, idx_keys(ik_ref[pl.ds(off, n * BLK), :]))
        return carry

    def idx_special(carry, first, n_left):
        offs = [pl.multiple_of(kb * BLK, BLK) for kb in special_blocks]
        left_off = pl.multiple_of(first * BLK, BLK)
        left = [ik_ref[pl.ds(left_off, n_left * BLK), :]] if n_left else []
        key = idx_keys(jnp.concatenate(left + [ik_ref[pl.ds(o, BLK), :] for o in offs], axis=0))
        if n_left:
            put_keys(left_off, n_left, key[:n_left * BLK])
        key = key[n_left * BLK:]
        put_keys(offs[1], 1, key[BLK:2 * BLK])
        put_keys(offs[2], 1, jnp.where((row >= CHUNK) & (lane < CHUNK), INT_MIN, key[2 * BLK:]))
        put_keys(offs[0], 1, jnp.where(row >= N_META, INT_MIN, key[:BLK]))
        return carry

    _visit_key_blocks(i, idx_group, idx_special, 0, sizes=ATT_FAR_SIZES)
    put_keys(pl.multiple_of(nkb * BLK, BLK), FAR - 1, jnp.full(((FAR - 1) * BLK, BLK), INT_MIN, jnp.int32))
    n_search = lax.shift_right_logical(nkb + FAR - 1, 2)
    crow = lax.broadcasted_iota(jnp.int32, (FAR * BLK, BLK), 0)
    chunk_off = lambda c: pl.multiple_of(c * FAR * BLK, FAR * BLK)

    def count(pred_fn):
        def body(c, acc8):
            k = keys_ref[pl.ds(chunk_off(c), FAR * BLK), :]
            hit = pred_fn(k, chunk_off(c)).astype(jnp.int32)
            return acc8 + jnp.sum(hit.reshape(FAR * BLK // 8, 8, BLK), axis=0)
        acc8 = lax.fori_loop(0, n_search, body, jnp.zeros((8, BLK), jnp.int32))
        return jnp.sum(acc8, axis=0, keepdims=True)

    zero = jnp.zeros((1, BLK), jnp.int32)
    c0 = count(lambda k, off: k >= zero)
    prefix = jnp.where(c0 >= TOPK, 0, INT_MIN).astype(jnp.int32)

    def bit_body(t, carry):
        prefix, n_ge = carry
        cand = prefix | jnp.left_shift(jnp.int32(1), 30 - t)
        c = count(lambda k, off: k >= cand)
        return jnp.where(c >= TOPK, cand, prefix), jnp.where(c >= TOPK, c, n_ge)

    head_passes = 31 - LATE_BITS
    thr_ref[...], nge_ref[...] = lax.fori_loop(0, head_passes, bit_body, (prefix, c0))

    @pl.when(jnp.max(nge_ref[...]) > TOPK)
    def _():
        thr_ref[...], nge_ref[...] = lax.fori_loop(head_passes, 31, bit_body, (thr_ref[...], nge_ref[...]))

    thr, n_ge = thr_ref[...], nge_ref[...]
    full = thr == INT_MIN
    tied = jnp.logical_and(jnp.logical_not(full), n_ge > TOPK)

    @pl.when(jnp.max(tied.astype(jnp.int32)) > 0)
    def _():
        need = TOPK - count(lambda k, off: k > thr)

        def jbit(t, j):
            cand = j | jnp.left_shift(jnp.int32(1), 12 - t)
            c = count(lambda k, off: (k == thr) & ((off + crow) < cand))
            return jnp.where(c < need, cand, j)
        jmax = lax.fori_loop(0, 13, jbit, jnp.zeros((1, BLK), jnp.int32))

        def strike(c, carry):
            k = keys_ref[pl.ds(chunk_off(c), FAR * BLK), :]
            surplus = tied & (k == thr) & ((chunk_off(c) + crow) > jmax)
            keys_ref[pl.ds(chunk_off(c), FAR * BLK), :] = jnp.where(surplus, INT_MIN, k)
            return carry
        lax.fori_loop(0, n_search, strike, 0)

    thr_sel = jnp.where(full, INT_MIN + 1, thr)

    acc_ref[...] = jnp.zeros_like(acc_ref)

    def att_update(rows, keys, vt, bias, carry):
        selb = jnp.where(keys >= thr_sel, 0.0, NEG)
        add = jnp.concatenate([selb, selb], axis=1)
        gs = [slice(g * GW, (g + 1) * GW) for g in range(NG)]
        ss = [_split_rows_dot(rows, qlat_ref[:, gs[g]], NN_DIMS) for g in range(NG)]
        ss = [ss[g] + (add if bias is None else add + bias[:, gs[g]]) for g in range(NG)]
        steps = [_online_softmax_step(ss[g], *carry[g]) for g in range(NG)]
        pvs = [jnp.dot(vt, steps[g][3].astype(BF16), preferred_element_type=F32) for g in range(NG)]
        for g in range(NG):
            acc_ref[:, gs[g]] = acc_ref[:, gs[g]] * steps[g][2] + pvs[g]
        return tuple((steps[g][0], steps[g][1]) for g in range(NG))

    def att_group(kb, n, carry):
        off = pl.multiple_of(kb * BLK, BLK)
        vt = jnp.concatenate([ckvt_ref[kb + u] for u in range(n)], axis=1) if n > 1 else ckvt_ref[kb]
        return att_update(ckv_ref[pl.ds(off, n * BLK), :], keys_ref[pl.ds(off, n * BLK), :], vt, None, carry)

    def att_special(carry, first, n_left):
        offs = [pl.multiple_of(kb * BLK, BLK) for kb in special_blocks]
        left_off = pl.multiple_of(first * BLK, BLK)
        left = lambda ref: [ref[pl.ds(left_off, n_left * BLK), :]] if n_left else []
        rows = jnp.concatenate(left(ckv_ref) + [ckv_ref[pl.ds(o, BLK), :] for o in offs], axis=0)
        keys = jnp.concatenate(left(keys_ref) + [keys_ref[pl.ds(o, BLK), :] for o in offs], axis=0)
        vt = jnp.concatenate([ckvt_ref[first + u] for u in range(n_left)] + [ckvt_ref[kb] for kb in special_blocks],
                             axis=1)
        no_bias = [jnp.zeros((n_left * BLK, A_HEADS * BLK), F32)] if n_left else []
        bias = jnp.concatenate(no_bias + [bias_ref[t_meta], bias_ref[t_prev], bias_ref[T_DIAG]], axis=0)
        return att_update(rows, keys, vt, bias, carry)

    m0 = jnp.full((1, GW), NEG, F32)
    l0 = jnp.zeros((1, GW), F32)
    stats = _visit_key_blocks(i, att_group, att_special, tuple((m0, l0) for _ in range(NG)), sizes=ATT_FAR_SIZES)
    l = jnp.concatenate([stats[g][1] for g in range(NG)], axis=1)

    olat = (acc_ref[...] / l).astype(BF16)
    ots = [jnp.dot(wuvt_ref[h], olat[:, h * BLK:(h + 1) * BLK], preferred_element_type=F32) for h in range(A_HEADS)]
    for h in range(A_HEADS):
        o_ref[:, h * BLK:(h + 1) * BLK] = ots[h].T


def _attn_a(q_a, idxp, iwt, ckv, ckvt, ik, wuk, wuvt, bias_a):
    qrow = lambda b, i: b * NKB + 1 + i
    return pl.pallas_call(
        _attn_a_kernel,
        grid=(BATCH, NQB),
        in_specs=[
            pl.BlockSpec((BLK, A_WIDTH), lambda b, i: (qrow(b, i), 0)),
            pl.BlockSpec((BLK, IDX_HEADS * IDX_DIM), lambda b, i: (qrow(b, i), 0)),
            pl.BlockSpec((IDX_HEADS, BLK), lambda b, i: (0, qrow(b, i))),
            pl.BlockSpec((None, TP, KV_RANK), lambda b, i: (b, 0, 0)),
            pl.BlockSpec((None, NKB, KV_RANK, BLK), lambda b, i: (b, 0, 0, 0)),
            pl.BlockSpec((None, TP, IDX_DIM), lambda b, i: (b, 0, 0)),
            pl.BlockSpec((A_HEADS, KV_RANK, A_HEAD_DIM), lambda b, i: (0, 0, 0)),
            pl.BlockSpec((A_HEADS, A_HEAD_DIM, KV_RANK), lambda b, i: (0, 0, 0)),
            pl.BlockSpec((5, BLK, A_HEADS * BLK), lambda b, i: (0, 0, 0)),
        ],
        out_specs=pl.BlockSpec((BLK, A_WIDTH), lambda b, i: (b * NQB + i, 0)),
        out_shape=jax.ShapeDtypeStruct((BATCH * SEQ, A_WIDTH), F32),
        scratch_shapes=[pltpu.VMEM(((NKB + FAR - 1) * BLK, BLK), jnp.int32),
                        pltpu.VMEM((KV_RANK, A_HEADS * BLK), BF16),
                        pltpu.VMEM((KV_RANK, A_HEADS * BLK), F32),
                        pltpu.VMEM((IDX_DIM, IDX_HEADS * BLK), F32),
                        pltpu.VMEM((1, BLK), jnp.int32),
                        pltpu.VMEM((1, BLK), jnp.int32)],
        compiler_params=pltpu.CompilerParams(
            dimension_semantics=("arbitrary", "arbitrary"), vmem_limit_bytes=VMEM_LIMIT),
        name="attn_a",
    )(q_a, idxp, iwt, ckv.reshape(BATCH, TP, KV_RANK), ckvt.reshape(BATCH, NKB, KV_RANK, BLK),
      ik.reshape(BATCH, TP, IDX_DIM), wuk, wuvt, bias_a)


def _attn_b_kernel(lam_ref, q_ref, k_ref, vt_ref, bias_ref, subw_ref, o_ref, acc_ref, *, lam_init):
    i = pl.program_id(2)
    t_meta = jnp.where(i == 0, T_META0, T_METAFAR)
    t_prev = jnp.where(i == 0, T_NONE, T_PREV)
    special_blocks = (0, i, i + 1)
    lp = lam_ref[...]
    lam = (jnp.exp(jnp.sum(lp[0:1] * lp[1:2], axis=-1, keepdims=True))
           - jnp.exp(jnp.sum(lp[2:3] * lp[3:4], axis=-1, keepdims=True)) + lam_init)

    lane = lax.broadcasted_iota(jnp.int32, (BLK, BLK), 1)
    qbd = []
    for hh in range(HPS):
        q = q_ref[:, hh * BLK:(hh + 1) * BLK]
        zq = jnp.zeros_like(q)
        qbd.append(jnp.concatenate([jnp.where(lane < B_QK_DIM, q, zq), jnp.where(lane >= B_QK_DIM, q, zq)], axis=0))

    acc_ref[...] = jnp.zeros_like(acc_ref)

    def update_all(rows, vts, biases, carry):
        ss = [_split_rows_dot(rows[hh], qbd[hh], NT_DIMS) for hh in range(HPS)]
        if biases is not None:
            ss = [ss[hh] + jnp.concatenate([biases[hh], biases[hh]], axis=1) for hh in range(HPS)]
        steps = [_online_softmax_step(ss[hh], *carry[hh]) for hh in range(HPS)]
        pvs = [jnp.dot(vts[hh], steps[hh][3].astype(BF16), preferred_element_type=F32) for hh in range(HPS)]
        for hh in range(HPS):
            acc_ref[hh] = acc_ref[hh] * steps[hh][2] + pvs[hh]
        return tuple((steps[hh][0], steps[hh][1]) for hh in range(HPS))

    def group(kb, n, carry):
        off = pl.multiple_of(kb * BLK, BLK)
        rows = [k_ref[pl.ds(off, n * BLK), hh * BLK:(hh + 1) * BLK] for hh in range(HPS)]
        vts = [jnp.concatenate([vt_ref[kb + u, hh] for u in range(n)], axis=1) if n > 1 else vt_ref[kb, hh]
               for hh in range(HPS)]
        return update_all(rows, vts, None, carry)

    def special(carry, first, n_left):
        offs = [pl.multiple_of(kb * BLK, BLK) for kb in special_blocks]
        left_off = pl.multiple_of(first * BLK, BLK)
        cols = lambda hh: slice(hh * BLK, (hh + 1) * BLK)
        left = lambda hh: [k_ref[pl.ds(left_off, n_left * BLK), cols(hh)]] if n_left else []
        rows = [jnp.concatenate(left(hh) + [k_ref[pl.ds(o, BLK), cols(hh)] for o in offs], axis=0)
                for hh in range(HPS)]
        vts = [jnp.concatenate([vt_ref[first + u, hh] for u in range(n_left)]
                               + [vt_ref[kb, hh] for kb in special_blocks], axis=1) for hh in range(HPS)]
        no_bias = [jnp.zeros((n_left * BLK, BLK), F32)] if n_left else []
        biases = [jnp.concatenate(no_bias + [bias_ref[t_meta, hh], bias_ref[t_prev, hh], bias_ref[T_DIAG, hh]], axis=0)
                  for hh in range(HPS)]
        return update_all(rows, vts, biases, carry)

    m0 = jnp.full((1, 2 * BLK), NEG, F32)
    l0 = jnp.zeros((1, 2 * BLK), F32)
    stats = _visit_key_blocks(i, group, special, tuple((m0, l0) for _ in range(HPS)), sizes=ATT_FAR_SIZES)

    for hh in range(HPS):
        a = acc_ref[hh] / stats[hh][1]
        o = a[:, :BLK] - lam * a[:, BLK:]
        ms = jnp.mean(o * o, axis=0, keepdims=True)
        y = o * lax.rsqrt(ms + EPS) * subw_ref[...] * (1.0 - lam_init)
        o_ref[:, hh * BLK:(hh + 1) * BLK] = y.T


def _attn_b(qkv_b, vt, bias_b, lam_p, subw, lam_init):
    qrow = lambda b, g, i: b * NKB + 1 + i
    wide = HPS * BLK
    qcol0 = 0
    kcol0 = 2 * B_HEADS * B_QK_DIM // wide
    return pl.pallas_call(
        functools.partial(_attn_b_kernel, lam_init=lam_init),
        grid=(BATCH, B_HEADS // HPS, NQB),
        in_specs=[
            pl.BlockSpec((4, B_QK_DIM), lambda b, g, i: (0, 0)),
            pl.BlockSpec((BLK, wide), lambda b, g, i: (qrow(b, g, i), qcol0 + g)),
            pl.BlockSpec((None, TP, wide), lambda b, g, i: (b, 0, kcol0 + g)),
            pl.BlockSpec((None, NKB, HPS, B_V_DIM, BLK), lambda b, g, i: (b, 0, g, 0, 0)),
            pl.BlockSpec((5, HPS, BLK, BLK), lambda b, g, i: (0, g, 0, 0)),
            pl.BlockSpec((B_V_DIM, BLK), lambda b, g, i: (0, 0)),
        ],
        out_specs=pl.BlockSpec((BLK, wide), lambda b, g, i: (b * NQB + i, g)),
        out_shape=jax.ShapeDtypeStruct((BATCH * SEQ, B_WIDTH), F32),
        scratch_shapes=[pltpu.VMEM((HPS, B_V_DIM, 2 * BLK), F32)],
        compiler_params=pltpu.CompilerParams(
            dimension_semantics=("arbitrary", "arbitrary", "arbitrary"), vmem_limit_bytes=VMEM_LIMIT),
        name="attn_b",
    )(lam_p, qkv_b, qkv_b.reshape(BATCH, TP, -1), vt, bias_b, subw)


def _out_kernel(oa_ref, za_ref, ob_ref, zb_ref, ga_ref, gb_ref, x_ref, woa_ref, wob_ref, wout_ref, pw_ref, o_ref):
    a = (oa_ref[...] * jax.nn.silu(za_ref[...])).astype(BF16)
    ya = jnp.dot(a, woa_ref[...], preferred_element_type=F32)
    b = (ob_ref[...] * jax.nn.silu(zb_ref[...])).astype(BF16)
    yb = jnp.dot(b, wob_ref[...], preferred_element_type=F32)
    mix = jax.nn.sigmoid(ga_ref[...]) * ya + jax.nn.sigmoid(gb_ref[...]) * yb
    out = jnp.dot(mix.astype(BF16), wout_ref[...], preferred_element_type=F32)
    ms = jnp.mean(out * out, axis=-1, keepdims=True)
    o_ref[...] = x_ref[...] + out * lax.rsqrt(ms + EPS) * pw_ref[...]


def _out_stage(o_a, o_b, z_a, z_b, gates, x2, woa, wob, wout, pw):
    tm = 2 * BLK
    const = lambda g: (0, 0)
    return pl.pallas_call(
        _out_kernel,
        grid=(BATCH * SEQ // tm,),
        in_specs=[
            pl.BlockSpec((tm, A_WIDTH), lambda g: (g, 0)),
            pl.BlockSpec((tm, A_WIDTH), lambda g: (g, 0)),
            pl.BlockSpec((tm, B_WIDTH), lambda g: (g, 0)),
            pl.BlockSpec((tm, B_WIDTH), lambda g: (g, 0)),
            pl.BlockSpec((tm, D_MODEL), lambda g: (g, 0)),
            pl.BlockSpec((tm, D_MODEL), lambda g: (g, 1)),
            pl.BlockSpec((tm, D_MODEL), lambda g: (g, 0)),
            pl.BlockSpec((A_WIDTH, D_MODEL), const, pipeline_mode=pl.Buffered(1)),
            pl.BlockSpec((B_WIDTH, D_MODEL), const, pipeline_mode=pl.Buffered(1)),
            pl.BlockSpec((D_MODEL, D_MODEL), const, pipeline_mode=pl.Buffered(1)),
            pl.BlockSpec((1, D_MODEL), const),
        ],
        out_specs=pl.BlockSpec((tm, D_MODEL), lambda g: (g, 0)),
        out_shape=jax.ShapeDtypeStruct((BATCH * SEQ, D_MODEL), F32),
        compiler_params=pltpu.CompilerParams(
            dimension_semantics=("arbitrary",), vmem_limit_bytes=VMEM_LIMIT),
        name="out_stage",
    )(o_a, z_a, o_b, z_b, gates, gates, x2, woa, wob, wout, pw)


def kernel(x, meta_tokens, rel_bias, pre_norm_w, w_in, kv_norm_w, w_uk, w_uv, idx_k_norm_w, idx_k_norm_b,
           diff_lambda, diff_subln_w, w_o_a, w_o_b, w_out, post_norm_w):
    assert x.shape == (BATCH, SEQ, D_MODEL) and w_in.shape[0] == 1
    layer = 0
    lam_init = 0.8 - 0.6 * math.exp(-0.3 * layer)

    meta_block = jnp.concatenate([meta_tokens.astype(F32), jnp.zeros((BLK - N_META, D_MODEL), F32)], axis=0)

    wt = w_in[0].T
    w_head, w_rest = _wprep_head(wt), _wprep_rest(wt)
    wd = A_WIDTH
    assert all(IN_SIZES[k] == wd for k in (0, 2, 3, 6, 7, 8, 9)) and IN_SIZES[10] == IN_SIZES[11] == 2 * wd
    scale_qb = jnp.concatenate([jnp.full((1, wd), B_QK_DIM ** -0.5 * LOG2E, F32), jnp.ones((1, 2 * wd), F32)], axis=1)

    u, u32, u_f = _prenorm(x.reshape(BATCH * SEQ, D_MODEL), meta_block, pre_norm_w[0][None].astype(F32))
    tm, tmf = ROWS // 8, BATCH * SEQ // 8
    q_a = _matmul(u, w_head, 0, wd, BF16, tm, wd, "proj_q_a")
    lat = _matmul(u, w_head, 2 * wd, KV_RANK, F32, tm, KV_RANK, "proj_latent")
    qkv_b = _matmul(u, w_rest, 0, 3 * wd, BF16, tm, wd, "proj_qkv_b", col_scale=scale_qb)
    w_idx = wt[IN_OFFS[3]:IN_OFFS[3] + IDX_ROWS]
    idxp = _matmul(u32, w_idx, 0, IDX_ROWS, F32, tm, IDX_ROWS, "proj_indexer")
    z_a = _matmul(u_f, w_head, wd, wd, F32, tmf, wd, "proj_z_a")
    z_b = _matmul(u_f, w_rest, 3 * wd, wd, F32, tmf, wd, "proj_z_b")
    gates = _matmul(u_f, w_rest, 4 * wd, 4 * wd, F32, tmf, wd, "proj_gates")

    ckv, ckvt, ik, iwt, vt = _kvprep(lat, idxp, qkv_b, 2 * wd, kv_norm_w[0][None].astype(F32),
                                     idx_k_norm_w[0][None].astype(F32), idx_k_norm_b[0][None].astype(F32))

    bias = _bias_tiles(rel_bias) * LOG2E
    bias_a = jnp.transpose(bias[:, :A_HEADS], (0, 2, 1, 3)).reshape(5, BLK, A_HEADS * BLK)
    bias_b = bias[:, A_HEADS:]

    wuk = jnp.transpose(w_uk[0], (1, 0, 2)).astype(BF16)
    wuvt = jnp.transpose(w_uv[0], (1, 2, 0)).astype(BF16)
    o_a = _attn_a(q_a, idxp, iwt, ckv, ckvt, ik, wuk, wuvt, bias_a)

    vt = vt.reshape(BATCH, NKB, B_HEADS, B_V_DIM, BLK)
    subw = jnp.broadcast_to(diff_subln_w[0].astype(F32)[:, None], (B_V_DIM, BLK))
    o_b = _attn_b(qkv_b, vt, bias_b, diff_lambda[0].astype(F32), subw, lam_init)

    out = _out_stage(o_a, o_b, z_a, z_b, gates, x.reshape(BATCH * SEQ, D_MODEL),
                     w_o_a[0].astype(BF16), w_o_b[0].astype(BF16), w_out[0].astype(BF16),
                     post_norm_w[0][None].astype(F32))
    return out.reshape(BATCH, SEQ, D_MODEL)
```

```python
import functools
import math

import numpy as np
import jax
import jax.numpy as jnp
from jax import lax
from jax.experimental import pallas as pl
from jax.experimental.pallas import tpu as pltpu

D_MODEL = 2048
BATCH = 2
SEQ = 4096
CHUNK = 64
N_META = 16
N_BUCKETS = 32
MAX_DISTANCE = 128
A_HEADS = 8
A_HEAD_DIM = 128
KV_RANK = 256
IDX_HEADS = 16
IDX_DIM = 64
TOPK = 256
B_HEADS = 8
B_QK_DIM = 64
B_V_DIM = 128
A_WIDTH = A_HEADS * A_HEAD_DIM
B_WIDTH = B_HEADS * B_V_DIM
IN_SIZES = (A_WIDTH, KV_RANK, A_WIDTH, IDX_HEADS * IDX_DIM, IDX_DIM, IDX_HEADS,
            2 * B_HEADS * B_QK_DIM, 2 * B_HEADS * B_QK_DIM, B_WIDTH, B_WIDTH,
            D_MODEL, D_MODEL)
EPS = 1e-6

BLK = 128
NQB = SEQ // BLK
NKB = NQB + 1
TP = NKB * BLK
ROWS = BATCH * TP
FAR = 4
HPS = 8
ATT_FAR_SIZES = (2 * FAR, FAR)
IDX_FAR_SIZES = (4 * FAR, 2 * FAR, FAR)
NEG = -1e30
INT_MIN = -2 ** 31
LATE_BITS = 6
LOG2E = math.log2(math.e)
V7X_VMEM_BYTES = 64 * 2 ** 20
VMEM_LIMIT = V7X_VMEM_BYTES * 7 // 8

F32 = jnp.float32
BF16 = jnp.bfloat16
NT_DIMS = (((1,), (1,)), ((), ()))
NN_DIMS = (((1,), (0,)), ((), ()))


def _t5_bucket_np(rel):
    nb = N_BUCKETS // 2
    max_exact = nb // 2
    ret = np.where(rel > 0, nb, 0)
    n = np.abs(rel)
    nf = np.maximum(n, 1).astype(np.float32)
    large = max_exact + (np.log(nf / np.float32(max_exact))
                         / np.float32(math.log(MAX_DISTANCE / max_exact))
                         * np.float32(nb - max_exact)).astype(np.int32)
    large = np.minimum(large, nb - 1)
    return ret + np.where(n < max_exact, n, large)


T_DIAG, T_PREV, T_META0, T_METAFAR, T_NONE = range(5)


def _bias_tiles(rel_bias):
    a = np.arange(BLK)[:, None]
    b = np.arange(BLK)[None, :]
    nowhere = np.zeros((BLK, BLK), bool)
    pad_rows = (a >= N_META) | nowhere
    rels = np.stack([a - b, a - b - BLK, a - N_META - b])
    dis = np.stack([(a >= CHUNK) & (b < CHUNK), nowhere, pad_rows, pad_rows, ~nowhere])
    idx = _t5_bucket_np(rels)
    far_bucket = N_BUCKETS // 2 - 1
    assert _t5_bucket_np(np.array([-BLK - 1]))[0] == far_bucket
    rb = rel_bias.astype(F32)
    heads = A_HEADS + B_HEADS
    tiles = jnp.zeros((3, heads, BLK, BLK), F32)
    for k in np.unique(idx):
        tiles = jnp.where((idx == k)[:, None], rb[k][None, :, None, None], tiles)
    tiles = tiles - rb[far_bucket][None, :, None, None]
    tiles = jnp.concatenate([tiles, jnp.zeros((2, heads, BLK, BLK), F32)], axis=0)
    return jnp.where(dis[:, None], NEG, tiles)


def _visit_key_blocks(i, group_fn, special_fn, carry, sizes=(FAR,)):
    n_far = jnp.maximum(i - 1, 0)
    start = jnp.int32(1)
    for size in sizes:
        shift = size.bit_length() - 1
        assert size == 1 << shift
        n_groups = lax.shift_right_logical(1 + n_far - start, shift)
        carry = lax.fori_loop(0, n_groups, lambda c, cr, s=start, z=size: group_fn(s + z * c, z, cr), carry)
        start = start + size * n_groups
    branches = [functools.partial(special_fn, first=start, n_left=r) for r in range(sizes[-1])]
    return lax.switch(1 + n_far - start, branches, carry)


def _prenorm_kernel(x_ref, meta_ref, w_ref, o_ref, o32_ref, of_ref):
    def norm(x):
        ms = jnp.mean(x * x, axis=-1, keepdims=True)
        return x * lax.rsqrt(ms + EPS) * w_ref[...]

    is_meta = lax.rem(pl.program_id(0), NKB) == 0

    @pl.when(is_meta)
    def _():
        u = norm(meta_ref[...])
        o32_ref[...] = u
        o_ref[...] = u.astype(o_ref.dtype)

    @pl.when(jnp.logical_not(is_meta))
    def _():
        u = norm(x_ref[...])
        o32_ref[...] = u
        o_ref[...] = u.astype(o_ref.dtype)
        of_ref[...] = u.astype(of_ref.dtype)


def _prenorm(x2, meta_block, w):
    frame_block = lambda r: (jnp.maximum(r - r // NKB - 1, 0), 0)
    return pl.pallas_call(
        _prenorm_kernel,
        grid=(ROWS // BLK,),
        in_specs=[pl.BlockSpec((BLK, D_MODEL), frame_block),
                  pl.BlockSpec((BLK, D_MODEL), lambda r: (0, 0)),
                  pl.BlockSpec((1, D_MODEL), lambda r: (0, 0))],
        out_specs=[pl.BlockSpec((BLK, D_MODEL), lambda r: (r, 0)),
                   pl.BlockSpec((BLK, D_MODEL), lambda r: (r, 0)),
                   pl.BlockSpec((BLK, D_MODEL), frame_block)],
        out_shape=[jax.ShapeDtypeStruct((ROWS, D_MODEL), BF16),
                   jax.ShapeDtypeStruct((ROWS, D_MODEL), F32),
                   jax.ShapeDtypeStruct((BATCH * SEQ, D_MODEL), BF16)],
        compiler_params=pltpu.CompilerParams(dimension_semantics=("arbitrary",)),
        name="prenorm",
    )(x2, meta_block, w)


IN_OFFS = tuple(int(v) for v in np.concatenate([[0], np.cumsum(IN_SIZES)]))
HEAD_BLK = 2 * BLK
HEAD_ORDER = (0, 2, 1)
IDX_ROWS = IN_SIZES[3] + HEAD_BLK
W_REST = IN_OFFS[-1] - IN_OFFS[6]
REST_BLK = 4 * BLK


def _head_blocks():
    blocks = []
    for k in HEAD_ORDER:
        assert IN_OFFS[k] % HEAD_BLK == 0 and IN_SIZES[k] % HEAD_BLK == 0
        blocks += list(range(IN_OFFS[k] // HEAD_BLK, IN_OFFS[k + 1] // HEAD_BLK))
    return blocks


def _cast_kernel(w_ref, o_ref):
    o_ref[...] = w_ref[...].astype(o_ref.dtype)


def _wprep_head(wt):
    src = _head_blocks()

    def src_block(t):
        b = jnp.int32(src[-1])
        for pos in range(len(src) - 2, -1, -1):
            b = jnp.where(t == pos, src[pos], b)
        return b, 0
    return pl.pallas_call(
        _cast_kernel,
        grid=(len(src),),
        in_specs=[pl.BlockSpec((HEAD_BLK, D_MODEL), src_block)],
        out_specs=pl.BlockSpec((HEAD_BLK, D_MODEL), lambda t: (t, 0)),
        out_shape=jax.ShapeDtypeStruct((len(src) * HEAD_BLK, D_MODEL), BF16),
        compiler_params=pltpu.CompilerParams(dimension_semantics=("arbitrary",)),
        name="wprep_head",
    )(wt)


def _shift_cast_kernel(w_ref, o_ref, carry_ref, *, shift, n_out):
    s = pl.program_id(0)

    @pl.when(s > 0)
    def _():
        o_ref[...] = jnp.concatenate([carry_ref[...], w_ref[:shift]], axis=0).astype(o_ref.dtype)

    @pl.when(s < n_out)
    def _():
        carry_ref[...] = w_ref[shift:]


def _wprep_rest(wt):
    base, shift = divmod(IN_OFFS[6], REST_BLK)
    n_out = W_REST // REST_BLK
    assert shift % 8 == 0 and W_REST % REST_BLK == 0 and (base + n_out) * REST_BLK + shift == wt.shape[0]
    return pl.pallas_call(
        functools.partial(_shift_cast_kernel, shift=shift, n_out=n_out),
        grid=(n_out + 1,),
        in_specs=[pl.BlockSpec((REST_BLK, D_MODEL), lambda s: (base + s, 0))],
        out_specs=pl.BlockSpec((REST_BLK, D_MODEL), lambda s: (jnp.maximum(s - 1, 0), 0)),
        out_shape=jax.ShapeDtypeStruct((W_REST, D_MODEL), BF16),
        scratch_shapes=[pltpu.VMEM((REST_BLK - shift, D_MODEL), F32)],
        compiler_params=pltpu.CompilerParams(dimension_semantics=("arbitrary",), vmem_limit_bytes=VMEM_LIMIT),
        name="wprep_rest",
    )(wt)


def _mm_kernel(a_ref, w_ref, o_ref):
    acc = lax.dot_general(a_ref[...], w_ref[...], NT_DIMS, preferred_element_type=F32)
    o_ref[...] = acc.astype(o_ref.dtype)


def _mm_scaled_kernel(a_ref, w_ref, cs_ref, o_ref):
    acc = lax.dot_general(a_ref[...], w_ref[...], NT_DIMS, preferred_element_type=F32)
    o_ref[...] = (acc * cs_ref[...]).astype(o_ref.dtype)


def _matmul(a, wt, col0, n, out_dtype, tm, tn, name, col_scale=None):
    m, k = a.shape
    assert col0 % tn == 0 and n % tn == 0 and m % tm == 0 and wt.shape[1] == k
    c0 = col0 // tn
    resident = {"pipeline_mode": pl.Buffered(1)} if n == tn else {}
    in_specs = [pl.BlockSpec((tm, k), lambda i, j: (i, 0)),
                pl.BlockSpec((tn, k), lambda i, j: (c0 + j, 0), **resident)]
    args = (a, wt)
    if col_scale is not None:
        in_specs.append(pl.BlockSpec((1, tn), lambda i, j: (0, j)))
        args += (col_scale,)
    return pl.pallas_call(
        _mm_kernel if col_scale is None else _mm_scaled_kernel,
        grid=(m // tm, n // tn),
        in_specs=in_specs,
        out_specs=pl.BlockSpec((tm, tn), lambda i, j: (i, j)),
        out_shape=jax.ShapeDtypeStruct((m, n), out_dtype),
        compiler_params=pltpu.CompilerParams(
            dimension_semantics=("arbitrary", "arbitrary"), vmem_limit_bytes=VMEM_LIMIT),
        name=name,
    )(*args)


def _kvprep_kernel(c_ref, t_ref, v_ref, kvw_ref, ikw_ref, ikb_ref, ckv_ref, ckvt_ref, ik_ref, iwt_ref, vt_ref):
    for blk in range(2):
        for h in range(B_HEADS):
            vh = v_ref[blk * BLK:(blk + 1) * BLK, h * B_V_DIM:(h + 1) * B_V_DIM]
            vt_ref[blk, h] = vh.astype(F32).T.astype(BF16)
    ckv = c_ref[...]
    ms = jnp.mean(ckv * ckv, axis=-1, keepdims=True)
    ckvn = ckv * lax.rsqrt(ms + EPS) * kvw_ref[...]
    ckv_ref[...] = ckvn.astype(BF16)
    ckvt_ref[0] = ckvn[:BLK].T.astype(BF16)
    ckvt_ref[1] = ckvn[BLK:].T.astype(BF16)
    tail = t_ref[:, :BLK]
    ik = tail[:, :IDX_DIM]
    mu = jnp.mean(ik, axis=-1, keepdims=True)
    var = jnp.mean(jnp.square(ik - mu), axis=-1, keepdims=True)
    ik_ref[...] = (ik - mu) * lax.rsqrt(var + EPS) * ikw_ref[...] + ikb_ref[...]
    iwt_ref[...] = (tail * (IDX_HEADS ** -0.5 * IDX_DIM ** -0.5)).T[IDX_DIM:IDX_DIM + IDX_HEADS, :]


def _kvprep(c, idxp, kv_b, vcol, kvw, ikw, ikb):
    tm = 2 * BLK
    assert vcol % B_WIDTH == 0 and c.shape[1] == KV_RANK and idxp.shape[1] == IDX_ROWS
    return pl.pallas_call(
        _kvprep_kernel,
        grid=(ROWS // tm,),
        in_specs=[pl.BlockSpec((tm, KV_RANK), lambda i: (i, 0)),
                  pl.BlockSpec((tm, HEAD_BLK), lambda i: (i, IN_SIZES[3] // HEAD_BLK)),
                  pl.BlockSpec((tm, B_WIDTH), lambda i: (i, vcol // B_WIDTH)),
                  pl.BlockSpec((1, KV_RANK), lambda i: (0, 0)),
                  pl.BlockSpec((1, IDX_DIM), lambda i: (0, 0)),
                  pl.BlockSpec((1, IDX_DIM), lambda i: (0, 0))],
        out_specs=[pl.BlockSpec((tm, KV_RANK), lambda i: (i, 0)),
                   pl.BlockSpec((2, KV_RANK, BLK), lambda i: (i, 0, 0)),
                   pl.BlockSpec((tm, IDX_DIM), lambda i: (i, 0)),
                   pl.BlockSpec((IDX_HEADS, tm), lambda i: (0, i)),
                   pl.BlockSpec((2, B_HEADS, B_V_DIM, BLK), lambda i: (i, 0, 0, 0))],
        out_shape=[jax.ShapeDtypeStruct((ROWS, KV_RANK), BF16),
                   jax.ShapeDtypeStruct((ROWS // BLK, KV_RANK, BLK), BF16),
                   jax.ShapeDtypeStruct((ROWS, IDX_DIM), F32),
                   jax.ShapeDtypeStruct((IDX_HEADS, ROWS), F32),
                   jax.ShapeDtypeStruct((ROWS // BLK, B_HEADS, B_V_DIM, BLK), BF16)],
        name="kvprep",
    )(c, idxp, kv_b, kvw, ikw, ikb)


def _split_rows_dot(lhs, rhs, dims):
    rows = lhs.shape[0]
    if rows < 2 * BLK:
        return lax.dot_general(lhs, rhs, dims, preferred_element_type=F32)
    half = rows // 2
    return jnp.concatenate([lax.dot_general(lhs[:half], rhs, dims, preferred_element_type=F32),
                            lax.dot_general(lhs[half:], rhs, dims, preferred_element_type=F32)], axis=0)


def _online_softmax_step(s, m, l):
    m_new = jnp.maximum(m, jnp.max(s, axis=0, keepdims=True))
    alpha = jnp.exp2(m - m_new)
    p = jnp.exp2(s - m_new)
    return m_new, alpha * l + jnp.sum(p, axis=0, keepdims=True), alpha, p


def _attn_a_kernel(qa_ref, iq_ref, iwt_ref, ckv_ref, ckvt_ref, ik_ref, wuk_ref, wuvt_ref, bias_ref,
                   o_ref, keys_ref, qlat_ref, acc_ref, iqt_ref, thr_ref, nge_ref):
    i = pl.program_id(1)
    nkb = i + 2
    NG = A_HEADS // 2
    GW = 2 * BLK
    t_meta = jnp.where(i == 0, T_META0, T_METAFAR)
    t_prev = jnp.where(i == 0, T_NONE, T_PREV)
    special_blocks = (0, i, i + 1)

    for h in range(A_HEADS):
        qh = qa_ref[:, h * BLK:(h + 1) * BLK]
        ql = lax.dot_general(wuk_ref[h], qh, NT_DIMS, preferred_element_type=F32)
        qlat_ref[:, h * BLK:(h + 1) * BLK] = (ql * (A_HEAD_DIM ** -0.5 * LOG2E)).astype(BF16)

    for pr in range(IDX_HEADS // 2):
        t = iq_ref[:, pr * BLK:(pr + 1) * BLK].T
        iqt_ref[:, (2 * pr) * BLK:(2 * pr + 1) * BLK] = t[:IDX_DIM]
        iqt_ref[:, (2 * pr + 1) * BLK:(2 * pr + 2) * BLK] = t[IDX_DIM:]

    iwt = iwt_ref[...]
    row = lax.broadcasted_iota(jnp.int32, (BLK, BLK), 0)
    lane = lax.broadcasted_iota(jnp.int32, (BLK, BLK), 1)

    def idx_keys(ikrows):
        sc = jnp.zeros((ikrows.shape[0], BLK), F32)
        for pr in range(IDX_HEADS // 2):
            s2 = jnp.dot(ikrows, iqt_ref[:, pr * 2 * BLK:(pr + 1) * 2 * BLK], preferred_element_type=F32)
            sc = sc + jnp.maximum(s2[:, :BLK], 0.0) * iwt[2 * pr:2 * pr + 1, :]
            sc = sc + jnp.maximum(s2[:, BLK:], 0.0) * iwt[2 * pr + 1:2 * pr + 2, :]
        bits = lax.bitcast_convert_type(sc, jnp.int32)
        return bits ^ ((bits >> 31) & 0x7FFFFFFF)

    def put_keys(off, n, key):
        keys_ref[pl.ds(off, n * BLK), :] = key

    def idx_group(kb, n, carry):
        off = pl.multiple_of(kb * BLK, BLK)
        put_keys(off, n, idx_keys(ik_ref[pl.ds(off, n * BLK), :]))
        return carry

    def idx_special(carry, first, n_left):
        offs = [pl.multiple_of(kb * BLK, BLK) for kb in special_blocks]
        left_off = pl.multiple_of(first * BLK, BLK)
        left = [ik_ref[pl.ds(left_off, n_left * BLK), :]] if n_left else []
        key = idx_keys(jnp.concatenate(left + [ik_ref[pl.ds(o, BLK), :] for o in offs], axis=0))
        if n_left:
            put_keys(left_off, n_left, key[:n_left * BLK])
        key = key[n_left * BLK:]
        put_keys(offs[1], 1, key[BLK:2 * BLK])
        put_keys(offs[2], 1, jnp.where((row >= CHUNK) & (lane < CHUNK), INT_MIN, key[2 * BLK:]))
        put_keys(offs[0], 1, jnp.where(row >= N_META, INT_MIN, key[:BLK]))
        return carry

    _visit_key_blocks(i, idx_group, idx_special, 0, sizes=IDX_FAR_SIZES)
    put_keys(pl.multiple_of(nkb * BLK, BLK), FAR - 1, jnp.full(((FAR - 1) * BLK, BLK), INT_MIN, jnp.int32))
    n_search = lax.shift_right_logical(nkb + FAR - 1, 2)
    crow = lax.broadcasted_iota(jnp.int32, (FAR * BLK, BLK), 0)
    chunk_off = lambda c: pl.multiple_of(c * FAR * BLK, FAR * BLK)

    def count(pred_fn):
        def body(c, acc8):
            k = keys_ref[pl.ds(chunk_off(c), FAR * BLK), :]
            hit = pred_fn(k, chunk_off(c)).astype(jnp.int32)
            return acc8 + jnp.sum(hit.reshape(FAR * BLK // 8, 8, BLK), axis=0)
        acc8 = lax.fori_loop(0, n_search, body, jnp.zeros((8, BLK), jnp.int32))
        return jnp.sum(acc8, axis=0, keepdims=True)

    zero = jnp.zeros((1, BLK), jnp.int32)
    c0 = count(lambda k, off: k >= zero)
    prefix = jnp.where(c0 >= TOPK, 0, INT_MIN).astype(jnp.int32)

    def bit_body(t, carry):
        prefix, n_ge = carry
        cand = prefix | jnp.left_shift(jnp.int32(1), 30 - t)
        c = count(lambda k, off: k >= cand)
        return jnp.where(c >= TOPK, cand, prefix), jnp.where(c >= TOPK, c, n_ge)

    head_passes = 31 - LATE_BITS
    thr_ref[...], nge_ref[...] = lax.fori_loop(0, head_passes, bit_body, (prefix, c0))

    @pl.when(jnp.max(nge_ref[...]) > TOPK)
    def _():
        thr_ref[...], nge_ref[...] = lax.fori_loop(head_passes, 31, bit_body, (thr_ref[...], nge_ref[...]))

    thr, n_ge = thr_ref[...], nge_ref[...]
    full = thr == INT_MIN
    tied = jnp.logical_and(jnp.logical_not(full), n_ge > TOPK)

    @pl.when(jnp.max(tied.astype(jnp.int32)) > 0)
    def _():
        need = TOPK - count(lambda k, off: k > thr)

        def jbit(t, j):
            cand = j | jnp.left_shift(jnp.int32(1), 12 - t)
            c = count(lambda k, off: (k == thr) & ((off + crow) < cand))
            return jnp.where(c < need, cand, j)
        jmax = lax.fori_loop(0, 13, jbit, jnp.zeros((1, BLK), jnp.int32))

        def strike(c, carry):
            k = keys_ref[pl.ds(chunk_off(c), FAR * BLK), :]
            surplus = tied & (k == thr) & ((chunk_off(c) + crow) > jmax)
            keys_ref[pl.ds(chunk_off(c), FAR * BLK), :] = jnp.where(surplus, INT_MIN, k)
            return carry
        lax.fori_loop(0, n_search, strike, 0)

    thr_sel = jnp.where(full, INT_MIN + 1, thr)

    acc_ref[...] = jnp.zeros_like(acc_ref)

    def att_update(rows, keys, vt, bias, carry):
        selb = jnp.where(keys >= thr_sel, 0.0, NEG)
        add = jnp.concatenate([selb, selb], axis=1)
        gs = [slice(g * GW, (g + 1) * GW) for g in range(NG)]
        ss = [_split_rows_dot(rows, qlat_ref[:, gs[g]], NN_DIMS) for g in range(NG)]
        ss = [ss[g] + (add if bias is None else add + bias[:, gs[g]]) for g in range(NG)]
        steps = [_online_softmax_step(ss[g], *carry[g]) for g in range(NG)]
        pvs = [jnp.dot(vt, steps[g][3].astype(BF16), preferred_element_type=F32) for g in range(NG)]
        for g in range(NG):
            acc_ref[:, gs[g]] = acc_ref[:, gs[g]] * steps[g][2] + pvs[g]
        return tuple((steps[g][0], steps[g][1]) for g in range(NG))

    def att_group(kb, n, carry):
        off = pl.multiple_of(kb * BLK, BLK)
        vt = jnp.concatenate([ckvt_ref[kb + u] for u in range(n)], axis=1) if n > 1 else ckvt_ref[kb]
        return att_update(ckv_ref[pl.ds(off, n * BLK), :], keys_ref[pl.ds(off, n * BLK), :], vt, None, carry)

    def att_special(carry, first, n_left):
        offs = [pl.multiple_of(kb * BLK, BLK) for kb in special_blocks]
        left_off = pl.multiple_of(first * BLK, BLK)
        left = lambda ref: [ref[pl.ds(left_off, n_left * BLK), :]] if n_left else []
        rows = jnp.concatenate(left(ckv_ref) + [ckv_ref[pl.ds(o, BLK), :] for o in offs], axis=0)
        keys = jnp.concatenate(left(keys_ref) + [keys_ref[pl.ds(o, BLK), :] for o in offs], axis=0)
        vt = jnp.concatenate([ckvt_ref[first + u] for u in range(n_left)] + [ckvt_ref[kb] for kb in special_blocks],
                             axis=1)
        no_bias = [jnp.zeros((n_left * BLK, A_HEADS * BLK), F32)] if n_left else []
        bias = jnp.concatenate(no_bias + [bias_ref[t_meta], bias_ref[t_prev], bias_ref[T_DIAG]], axis=0)
        return att_update(rows, keys, vt, bias, carry)

    m0 = jnp.full((1, GW), NEG, F32)
    l0 = jnp.zeros((1, GW), F32)
    stats = _visit_key_blocks(i, att_group, att_special, tuple((m0, l0) for _ in range(NG)), sizes=ATT_FAR_SIZES)
    l = jnp.concatenate([stats[g][1] for g in range(NG)], axis=1)

    olat = (acc_ref[...] / l).astype(BF16)
    ots = [jnp.dot(wuvt_ref[h], olat[:, h * BLK:(h + 1) * BLK], preferred_element_type=F32) for h in range(A_HEADS)]
    for h in range(A_HEADS):
        o_ref[:, h * BLK:(h + 1) * BLK] = ots[h].T


def _attn_a(q_a, idxp, iwt, ckv, ckvt, ik, wuk, wuvt, bias_a):
    qrow = lambda b, i: b * NKB + 1 + i
    return pl.pallas_call(
        _attn_a_kernel,
        grid=(BATCH, NQB),
        in_specs=[
            pl.BlockSpec((BLK, A_WIDTH), lambda b, i: (qrow(b, i), 0)),
            pl.BlockSpec((BLK, IDX_HEADS * IDX_DIM), lambda b, i: (qrow(b, i), 0)),
            pl.BlockSpec((IDX_HEADS, BLK), lambda b, i: (0, qrow(b, i))),
            pl.BlockSpec((None, TP, KV_RANK), lambda b, i: (b, 0, 0)),
            pl.BlockSpec((None, NKB, KV_RANK, BLK), lambda b, i: (b, 0, 0, 0)),
            pl.BlockSpec((None, TP, IDX_DIM), lambda b, i: (b, 0, 0)),
            pl.BlockSpec((A_HEADS, KV_RANK, A_HEAD_DIM), lambda b, i: (0, 0, 0)),
            pl.BlockSpec((A_HEADS, A_HEAD_DIM, KV_RANK), lambda b, i: (0, 0, 0)),
            pl.BlockSpec((5, BLK, A_HEADS * BLK), lambda b, i: (0, 0, 0)),
        ],
        out_specs=pl.BlockSpec((BLK, A_WIDTH), lambda b, i: (b * NQB + i, 0)),
        out_shape=jax.ShapeDtypeStruct((BATCH * SEQ, A_WIDTH), F32),
        scratch_shapes=[pltpu.VMEM(((NKB + FAR - 1) * BLK, BLK), jnp.int32),
                        pltpu.VMEM((KV_RANK, A_HEADS * BLK), BF16),
                        pltpu.VMEM((KV_RANK, A_HEADS * BLK), F32),
                        pltpu.VMEM((IDX_DIM, IDX_HEADS * BLK), F32),
                        pltpu.VMEM((1, BLK), jnp.int32),
                        pltpu.VMEM((1, BLK), jnp.int32)],
        compiler_params=pltpu.CompilerParams(
            dimension_semantics=("arbitrary", "arbitrary"), vmem_limit_bytes=VMEM_LIMIT),
        name="attn_a",
    )(q_a, idxp, iwt, ckv.reshape(BATCH, TP, KV_RANK), ckvt.reshape(BATCH, NKB, KV_RANK, BLK),
      ik.reshape(BATCH, TP, IDX_DIM), wuk, wuvt, bias_a)


def _attn_b_kernel(lam_ref, q_ref, k_ref, vt_ref, bias_ref, subw_ref, o_ref, acc_ref, *, lam_init):
    i = pl.program_id(2)
    t_meta = jnp.where(i == 0, T_META0, T_METAFAR)
    t_prev = jnp.where(i == 0, T_NONE, T_PREV)
    special_blocks = (0, i, i + 1)
    lp = lam_ref[...]
    lam = (jnp.exp(jnp.sum(lp[0:1] * lp[1:2], axis=-1, keepdims=True))
           - jnp.exp(jnp.sum(lp[2:3] * lp[3:4], axis=-1, keepdims=True)) + lam_init)

    lane = lax.broadcasted_iota(jnp.int32, (BLK, BLK), 1)
    qbd = []
    for hh in range(HPS):
        q = q_ref[:, hh * BLK:(hh + 1) * BLK]
        zq = jnp.zeros_like(q)
        qbd.append(jnp.concatenate([jnp.where(lane < B_QK_DIM, q, zq), jnp.where(lane >= B_QK_DIM, q, zq)], axis=0))

    acc_ref[...] = jnp.zeros_like(acc_ref)

    def update_all(rows, vts, biases, carry):
        ss = [_split_rows_dot(rows[hh], qbd[hh], NT_DIMS) for hh in range(HPS)]
        if biases is not None:
            ss = [ss[hh] + jnp.concatenate([biases[hh], biases[hh]], axis=1) for hh in range(HPS)]
        steps = [_online_softmax_step(ss[hh], *carry[hh]) for hh in range(HPS)]
        pvs = [jnp.dot(vts[hh], steps[hh][3].astype(BF16), preferred_element_type=F32) for hh in range(HPS)]
        for hh in range(HPS):
            acc_ref[hh] = acc_ref[hh] * steps[hh][2] + pvs[hh]
        return tuple((steps[hh][0], steps[hh][1]) for hh in range(HPS))

    def group(kb, n, carry):
        off = pl.multiple_of(kb * BLK, BLK)
        rows = [k_ref[pl.ds(off, n * BLK), hh * BLK:(hh + 1) * BLK] for hh in range(HPS)]
        vts = [jnp.concatenate([vt_ref[kb + u, hh] for u in range(n)], axis=1) if n > 1 else vt_ref[kb, hh]
               for hh in range(HPS)]
        return update_all(rows, vts, None, carry)

    def special(carry, first, n_left):
        offs = [pl.multiple_of(kb * BLK, BLK) for kb in special_blocks]
        left_off = pl.multiple_of(first * BLK, BLK)
        cols = lambda hh: slice(hh * BLK, (hh + 1) * BLK)
        left = lambda hh: [k_ref[pl.ds(left_off, n_left * BLK), cols(hh)]] if n_left else []
        rows = [jnp.concatenate(left(hh) + [k_ref[pl.ds(o, BLK), cols(hh)] for o in offs], axis=0)
                for hh in range(HPS)]
        vts = [jnp.concatenate([vt_ref[first + u, hh] for u in range(n_left)]
                               + [vt_ref[kb, hh] for kb in special_blocks], axis=1) for hh in range(HPS)]
        no_bias = [jnp.zeros((n_left * BLK, BLK), F32)] if n_left else []
        biases = [jnp.concatenate(no_bias + [bias_ref[t_meta, hh], bias_ref[t_prev, hh], bias_ref[T_DIAG, hh]], axis=0)
                  for hh in range(HPS)]
        return update_all(rows, vts, biases, carry)

    m0 = jnp.full((1, 2 * BLK), NEG, F32)
    l0 = jnp.zeros((1, 2 * BLK), F32)
    stats = _visit_key_blocks(i, group, special, tuple((m0, l0) for _ in range(HPS)), sizes=ATT_FAR_SIZES)

    for hh in range(HPS):
        a = acc_ref[hh] / stats[hh][1]
        o = a[:, :BLK] - lam * a[:, BLK:]
        ms = jnp.mean(o * o, axis=0, keepdims=True)
        y = o * lax.rsqrt(ms + EPS) * subw_ref[...] * (1.0 - lam_init)
        o_ref[:, hh * BLK:(hh + 1) * BLK] = y.T


def _attn_b(qkv_b, vt, bias_b, lam_p, subw, lam_init):
    qrow = lambda b, g, i: b * NKB + 1 + i
    wide = HPS * BLK
    qcol0 = 0
    kcol0 = 2 * B_HEADS * B_QK_DIM // wide
    return pl.pallas_call(
        functools.partial(_attn_b_kernel, lam_init=lam_init),
        grid=(BATCH, B_HEADS // HPS, NQB),
        in_specs=[
            pl.BlockSpec((4, B_QK_DIM), lambda b, g, i: (0, 0)),
            pl.BlockSpec((BLK, wide), lambda b, g, i: (qrow(b, g, i), qcol0 + g)),
            pl.BlockSpec((None, TP, wide), lambda b, g, i: (b, 0, kcol0 + g)),
            pl.BlockSpec((None, NKB, HPS, B_V_DIM, BLK), lambda b, g, i: (b, 0, g, 0, 0)),
            pl.BlockSpec((5, HPS, BLK, BLK), lambda b, g, i: (0, g, 0, 0)),
            pl.BlockSpec((B_V_DIM, BLK), lambda b, g, i: (0, 0)),
        ],
        out_specs=pl.BlockSpec((BLK, wide), lambda b, g, i: (b * NQB + i, g)),
        out_shape=jax.ShapeDtypeStruct((BATCH * SEQ, B_WIDTH), F32),
        scratch_shapes=[pltpu.VMEM((HPS, B_V_DIM, 2 * BLK), F32)],
        compiler_params=pltpu.CompilerParams(
            dimension_semantics=("arbitrary", "arbitrary", "arbitrary"), vmem_limit_bytes=VMEM_LIMIT),
        name="attn_b",
    )(lam_p, qkv_b, qkv_b.reshape(BATCH, TP, -1), vt, bias_b, subw)


def _out_kernel(oa_ref, za_ref, ob_ref, zb_ref, ga_ref, gb_ref, x_ref, woa_ref, wob_ref, wout_ref, pw_ref, o_ref):
    a = (oa_ref[...] * jax.nn.silu(za_ref[...])).astype(BF16)
    ya = jnp.dot(a, woa_ref[...], preferred_element_type=F32)
    b = (ob_ref[...] * jax.nn.silu(zb_ref[...])).astype(BF16)
    yb = jnp.dot(b, wob_ref[...], preferred_element_type=F32)
    mix = jax.nn.sigmoid(ga_ref[...]) * ya + jax.nn.sigmoid(gb_ref[...]) * yb
    out = jnp.dot(mix.astype(BF16), wout_ref[...], preferred_element_type=F32)
    ms = jnp.mean(out * out, axis=-1, keepdims=True)
    o_ref[...] = x_ref[...] + out * lax.rsqrt(ms + EPS) * pw_ref[...]


def _out_stage(o_a, o_b, z_a, z_b, gates, x2, woa, wob, wout, pw):
    tm = 2 * BLK
    const = lambda g: (0, 0)
    return pl.pallas_call(
        _out_kernel,
        grid=(BATCH * SEQ // tm,),
        in_specs=[
            pl.BlockSpec((tm, A_WIDTH), lambda g: (g, 0)),
            pl.BlockSpec((tm, A_WIDTH), lambda g: (g, 0)),
            pl.BlockSpec((tm, B_WIDTH), lambda g: (g, 0)),
            pl.BlockSpec((tm, B_WIDTH), lambda g: (g, 0)),
            pl.BlockSpec((tm, D_MODEL), lambda g: (g, 0)),
            pl.BlockSpec((tm, D_MODEL), lambda g: (g, 1)),
            pl.BlockSpec((tm, D_MODEL), lambda g: (g, 0)),
            pl.BlockSpec((A_WIDTH, D_MODEL), const, pipeline_mode=pl.Buffered(1)),
            pl.BlockSpec((B_WIDTH, D_MODEL), const, pipeline_mode=pl.Buffered(1)),
            pl.BlockSpec((D_MODEL, D_MODEL), const, pipeline_mode=pl.Buffered(1)),
            pl.BlockSpec((1, D_MODEL), const),
        ],
        out_specs=pl.BlockSpec((tm, D_MODEL), lambda g: (g, 0)),
        out_shape=jax.ShapeDtypeStruct((BATCH * SEQ, D_MODEL), F32),
        compiler_params=pltpu.CompilerParams(
            dimension_semantics=("arbitrary",), vmem_limit_bytes=VMEM_LIMIT),
        name="out_stage",
    )(o_a, z_a, o_b, z_b, gates, gates, x2, woa, wob, wout, pw)


def kernel(x, meta_tokens, rel_bias, pre_norm_w, w_in, kv_norm_w, w_uk, w_uv, idx_k_norm_w, idx_k_norm_b,
           diff_lambda, diff_subln_w, w_o_a, w_o_b, w_out, post_norm_w):
    assert x.shape == (BATCH, SEQ, D_MODEL) and w_in.shape[0] == 1
    layer = 0
    lam_init = 0.8 - 0.6 * math.exp(-0.3 * layer)

    meta_block = jnp.concatenate([meta_tokens.astype(F32), jnp.zeros((BLK - N_META, D_MODEL), F32)], axis=0)

    wt = w_in[0].T
    w_head, w_rest = _wprep_head(wt), _wprep_rest(wt)
    wd = A_WIDTH
    assert all(IN_SIZES[k] == wd for k in (0, 2, 3, 6, 7, 8, 9)) and IN_SIZES[10] == IN_SIZES[11] == 2 * wd
    scale_qb = jnp.concatenate([jnp.full((1, wd), B_QK_DIM ** -0.5 * LOG2E, F32), jnp.ones((1, 2 * wd), F32)], axis=1)

    u, u32, u_f = _prenorm(x.reshape(BATCH * SEQ, D_MODEL), meta_block, pre_norm_w[0][None].astype(F32))
    tm, tmf = ROWS // 8, BATCH * SEQ // 8
    q_a = _matmul(u, w_head, 0, wd, BF16, tm, wd, "proj_q_a")
    lat = _matmul(u, w_head, 2 * wd, KV_RANK, F32, tm, KV_RANK, "proj_latent")
    qkv_b = _matmul(u, w_rest, 0, 3 * wd, BF16, tm, wd, "proj_qkv_b", col_scale=scale_qb)
    w_idx = wt[IN_OFFS[3]:IN_OFFS[3] + IDX_ROWS]
    idxp = _matmul(u32, w_idx, 0, IDX_ROWS, F32, tm, IDX_ROWS, "proj_indexer")
    z_a = _matmul(u_f, w_head, wd, wd, F32, tmf, wd, "proj_z_a")
    z_b = _matmul(u_f, w_rest, 3 * wd, wd, F32, tmf, wd, "proj_z_b")
    gates = _matmul(u_f, w_rest, 4 * wd, 4 * wd, F32, tmf, wd, "proj_gates")

    ckv, ckvt, ik, iwt, vt = _kvprep(lat, idxp, qkv_b, 2 * wd, kv_norm_w[0][None].astype(F32),
                                     idx_k_norm_w[0][None].astype(F32), idx_k_norm_b[0][None].astype(F32))

    bias = _bias_tiles(rel_bias) * LOG2E
    bias_a = jnp.transpose(bias[:, :A_HEADS], (0, 2, 1, 3)).reshape(5, BLK, A_HEADS * BLK)
    bias_b = bias[:, A_HEADS:]

    wuk = jnp.transpose(w_uk[0], (1, 0, 2)).astype(BF16)
    wuvt = jnp.transpose(w_uv[0], (1, 2, 0)).astype(BF16)
    o_a = _attn_a(q_a, idxp, iwt, ckv, ckvt, ik, wuk, wuvt, bias_a)

    vt = vt.reshape(BATCH, NKB, B_HEADS, B_V_DIM, BLK)
    subw = jnp.broadcast_to(diff_subln_w[0].astype(F32)[:, None], (B_V_DIM, BLK))
    o_b = _attn_b(qkv_b, vt, bias_b, diff_lambda[0].astype(F32), subw, lam_init)

    out = _out_stage(o_a, o_b, z_a, z_b, gates, x.reshape(BATCH * SEQ, D_MODEL),
                     w_o_a[0].astype(BF16), w_o_b[0].astype(BF16), w_out[0].astype(BF16),
                     post_norm_w[0][None].astype(F32))
    return out.reshape(BATCH, SEQ, D_MODEL)
```
